```python
import jax
import jax.numpy as jnp
from jax import lax
import numpy as np

D_MODEL = 2048
BATCH = 1
SEQ = 16384
DEPTH = 2

RW_HEADS = 16
RW_HEAD = 64
RW_WIDTH = RW_HEADS * RW_HEAD
RW_DECAY_LORA = 96
RW_A_LORA = 96
RW_V_LORA = 64
RW_G_LORA = 256
RW_GN_EPS = 64e-5
RW_BASE_SIZES = (RW_WIDTH, RW_WIDTH, RW_WIDTH, RW_DECAY_LORA, RW_A_LORA, RW_G_LORA)
RW_COLS_FIRST = sum(RW_BASE_SIZES)
RW_COLS_DEEP = RW_COLS_FIRST + RW_V_LORA

NSA_HEADS = 16
NSA_KV_GROUPS = 4
NSA_HPG = NSA_HEADS // NSA_KV_GROUPS
NSA_HEAD = 64
NSA_WIDTH = NSA_HEADS * NSA_HEAD
NSA_KV = NSA_KV_GROUPS * NSA_HEAD
CMP_LEN = 32
CMP_STRIDE = 16
CMP_HIDDEN = 128
SEL_BLOCK = 64
SEL_TOPN = 16
WINDOW = 512
Q_BLOCK = 128
FORCE_BONUS = 1e4
NEG_INF = -1e30
NSA_SIZES = (NSA_WIDTH,) + (NSA_KV,) * 6 + (3 * NSA_HEADS,)
NSA_COLS = sum(NSA_SIZES)

RET_HEADS = 8
RET_HEAD = 128
RET_WIDTH = RET_HEADS * RET_HEAD
RET_CHUNK = 128
RET_THETA = 10000.0
RET_GN_EPS = 1e-5
RET_SIZES = (RET_WIDTH,) * 4
RET_COLS = sum(RET_SIZES)

IN_COLS_FIRST = RW_COLS_FIRST + NSA_COLS + RET_COLS + 3 * D_MODEL
IN_COLS_DEEP = RW_COLS_DEEP + NSA_COLS + RET_COLS + 3 * D_MODEL

N_GROUPS = 4
EXPERTS_PER_GROUP = 8
N_EXPERTS = N_GROUPS * EXPERTS_PER_GROUP
EXPERT_FF = 512
EXPERT_TOPK = 2
MOE_BLOCK = 128

DN_ALPHA = (2 * DEPTH) ** 0.25
DN_BETA = (8 * DEPTH) ** -0.25
LN_EPS = 1e-5

kernel_name = 'hybrid_rwkv7_nsa_retention_hmoe'


def _split(p, sizes):
    idx = [int(i) for i in np.cumsum(sizes)[:-1]]
    return jnp.split(p, idx, axis=-1)


def _layer_norm(x, g, b):
    xf = x.astype(jnp.float32)
    mu = jnp.mean(xf, -1, keepdims=True)
    var = jnp.mean(jnp.square(xf - mu), -1, keepdims=True)
    return ((xf - mu) * lax.rsqrt(var + LN_EPS)).astype(x.dtype) * g + b


def _head_norm(x, eps):
    x = x.astype(jnp.float32)
    mu = jnp.mean(x, -1, keepdims=True)
    var = jnp.mean(jnp.square(x - mu), -1, keepdims=True)
    return (x - mu) * lax.rsqrt(var + eps)


def _token_shift(z, mu):
    prev = jnp.pad(z, ((0, 0), (1, 0), (0, 0)))[:, :-1]
    return z + (prev - z) * mu


def _masked_softmax(s, mask):
    s = jnp.where(mask, s, NEG_INF)
    p = jax.nn.softmax(s, axis=-1)
    return jnp.where(mask, p, 0.0)


def _rwkv7_time_mix(p, mu, w0, w_up, a0, a_up, g_up, k_k, k_a, r_k, lnx_g, lnx_b, v_first, v0, v_up):
    B, S, _ = p.shape
    H, N = RW_HEADS, RW_HEAD
    f32 = jnp.float32
    p = _token_shift(p, mu)
    if v_first is None:
        r, k, v, xw, xa, xg = _split(p, RW_BASE_SIZES)
        v_first = v
    else:
        r, k, v, xw, xa, xg, xv = _split(p, RW_BASE_SIZES + (RW_V_LORA,))
        v = v + (v_first - v) * jax.nn.sigmoid(v0 + xv @ v_up)
    w_log = -jax.nn.softplus(-(w0 + jnp.tanh(xw) @ w_up)) - 0.5
    decay = jnp.exp(-jnp.exp(w_log.astype(f32)))
    a = jax.nn.sigmoid(a0 + xa @ a_up)
    g = jax.nn.sigmoid(xg) @ g_up
    kk = (k * k_k).astype(f32).reshape(B, S, H, N)
    kk = kk / jnp.maximum(jnp.linalg.norm(kk, axis=-1, keepdims=True), 1e-12)
    k = k * (1.0 + (a - 1.0) * k_a)
    heads = lambda t: t.astype(f32).reshape(B, S, H, N)
    r_h, k_h, v_h, a_h, w_h = heads(r), heads(k), heads(v), heads(a), heads(decay)

    def step(state, inp):
        r_t, w_t, k_t, v_t, kk_t, a_t = inp
        s_kk = jnp.einsum('bhij,bhj->bhi', state, kk_t)
        state = (state * w_t[:, :, None, :] - s_kk[..., :, None] * (kk_t * a_t)[..., None, :]
                 + v_t[..., :, None] * k_t[..., None, :])
        return state, jnp.einsum('bhij,bhj->bhi', state, r_t)

    seq_first = lambda t: jnp.moveaxis(t, 1, 0)
    xs = (seq_first(r_h), seq_first(w_h), seq_first(k_h), seq_first(v_h), seq_first(kk), seq_first(a_h))
    _, y = lax.scan(step, jnp.zeros((B, H, N, N), f32), xs)
    y = jnp.moveaxis(y, 0, 1)
    y = _head_norm(y, RW_GN_EPS).reshape(B, S, RW_WIDTH) * lnx_g + lnx_b
    bonus = jnp.sum(r_h * k_h * r_k, -1, keepdims=True) * v_h
    out = (y + bonus.reshape(B, S, RW_WIDTH)) * g.astype(f32)
    return out.astype(p.dtype), v_first


def _nsa_compress(t, pe, w1, b1, w2):
    B, S, G, DH = t.shape
    n_cmp = (S - CMP_LEN) // CMP_STRIDE + 1
    idx = jnp.arange(n_cmp)[:, None] * CMP_STRIDE + jnp.arange(CMP_LEN)[None, :]
    blocks = t[:, idx] + pe[:, None, :]
    blocks = jnp.swapaxes(blocks, 2, 3).reshape(B, n_cmp, G, CMP_LEN * DH)
    return jax.nn.gelu(blocks @ w1 + b1) @ w2


def _nsa(p, cmp_pe, cmp_w1, cmp_b1, cmp_w2):
    B, S, _ = p.shape
    G, HPG, DH = NSA_KV_GROUPS, NSA_HPG, NSA_HEAD
    f32 = jnp.float32
    q, kc, vc, ks, vs, kw, vw, gate = _split(p, NSA_SIZES)
    kv = lambda t: t.reshape(B, S, G, DH)
    q = q.reshape(B, S, G, HPG, DH) * (DH ** -0.5)
    gate = jax.nn.sigmoid(gate).reshape(B, S, G, HPG, 3)
    k_cmp = _nsa_compress(kv(kc), cmp_pe[0], cmp_w1[0], cmp_b1[0], cmp_w2[0])
    v_cmp = _nsa_compress(kv(vc), cmp_pe[1], cmp_w1[1], cmp_b1[1], cmp_w2[1])
    n_cmp = k_cmp.shape[1]
    cmp_start = jnp.arange(n_cmp) * CMP_STRIDE
    cmp_end = cmp_start + CMP_LEN - 1
    n_sel = S // SEL_BLOCK
    top_n = min(SEL_TOPN, n_sel)
    sel_j = jnp.arange(n_sel)
    sel_start = sel_j * SEL_BLOCK
    overlap = ((cmp_start[:, None] < sel_start[None] + SEL_BLOCK)
               & (cmp_start[:, None] + CMP_LEN > sel_start[None])).astype(f32)
    to_blocks = lambda t: kv(t).reshape(B, n_sel, SEL_BLOCK, G, DH).transpose(0, 3, 1, 2, 4)
    ks_b, vs_b = to_blocks(ks), to_blocks(vs)
    pad = lambda t: jnp.pad(kv(t), ((0, 0), (WINDOW, 0), (0, 0), (0, 0)))
    kw_p, vw_p = pad(kw), pad(vw)
    gather_blocks = jax.vmap(jax.vmap(lambda blocks, ids: blocks[ids]))
    win_off = jnp.arange(WINDOW + Q_BLOCK) - WINDOW

    def query_block(qi):
        q0 = qi * Q_BLOCK
        t = q0 + jnp.arange(Q_BLOCK)
        qb = lax.dynamic_slice_in_dim(q, q0, Q_BLOCK, axis=1)
        gb = lax.dynamic_slice_in_dim(gate, q0, Q_BLOCK, axis=1)
        s = jnp.einsum('bqghd,bcgd->bghqc', qb, k_cmp).astype(f32)
        p_cmp = _masked_softmax(s, cmp_end[None, :] <= t[:, None])
        o_cmp = jnp.einsum('bghqc,bcgd->bqghd', p_cmp.astype(v_cmp.dtype), v_cmp)
        imp = jnp.einsum('bghqc,cn->bgqn', p_cmp, overlap)
        cur = (t // SEL_BLOCK)[:, None]
        causal = sel_j[None] <= cur
        forced = (sel_j[None] == 0) | (sel_j[None] == cur) | (sel_j[None] == cur - 1)
        score = jnp.where(causal, imp + jnp.where(forced, FORCE_BONUS, 0.0), NEG_INF)
        top_val, top_idx = lax.top_k(score, top_n)
        k_sel = gather_blocks(ks_b, top_idx).reshape(B, G, Q_BLOCK, top_n * SEL_BLOCK, DH)
        v_sel = gather_blocks(vs_b, top_idx).reshape(B, G, Q_BLOCK, top_n * SEL_BLOCK, DH)
        tok = top_idx[..., None] * SEL_BLOCK + jnp.arange(SEL_BLOCK)
        m_sel = (top_val[..., None] > 0.5 * NEG_INF) & (tok <= t[:, None, None])
        m_sel = m_sel.reshape(B, G, 1, Q_BLOCK, top_n * SEL_BLOCK)
        s = jnp.einsum('bqghd,bgqkd->bghqk', qb, k_sel).astype(f32)
        p_sel = _masked_softmax(s, m_sel)
        o_sel = jnp.einsum('bghqk,bgqkd->bqghd', p_sel.astype(v_sel.dtype), v_sel)
        kwb = lax.dynamic_slice_in_dim(kw_p, q0, WINDOW + Q_BLOCK, axis=1)
        vwb = lax.dynamic_slice_in_dim(vw_p, q0, WINDOW + Q_BLOCK, axis=1)
        s_pos = q0 + win_off
        dist = t[:, None] - s_pos[None]
        m_win = (dist >= 0) & (dist < WINDOW) & (s_pos[None] >= 0)
        s = jnp.einsum('bqghd,bkgd->bghqk', qb, kwb).astype(f32)
        p_win = _masked_softmax(s, m_win)
        o_win = jnp.einsum('bghqk,bkgd->bqghd', p_win.astype(vwb.dtype), vwb)
        return gb[..., 0:1] * o_cmp + gb[..., 1:2] * o_sel + gb[..., 2:3] * o_win

    out = lax.map(query_block, jnp.arange(S // Q_BLOCK))
    return jnp.moveaxis(out, 0, 1).reshape(B, S, NSA_WIDTH)


def _rotate_half(t, cos, sin):
    t1, t2 = jnp.split(t, 2, axis=-1)
    c, s = cos[:, None, :], sin[:, None, :]
    return jnp.concatenate([t1 * c - t2 * s, t1 * s + t2 * c], axis=-1)


def _retention(p):
    B, S, _ = p.shape
    H, DK = RET_HEADS, RET_HEAD
    f32 = jnp.float32
    q, k, v, g = _split(p, RET_SIZES)
    heads = lambda t: t.astype(f32).reshape(B, S, H, DK)
    inv_freq = 1.0 / (RET_THETA ** jnp.linspace(0.0, 1.0, DK // 2))
    ang = jnp.arange(S, dtype=f32)[:, None] * inv_freq[None, :]
    cos, sin = jnp.cos(ang), jnp.sin(ang)
    q = _rotate_half(heads(q), cos, sin)
    k = _rotate_half(heads(k), cos, sin) * (DK ** -0.5)
    v = heads(v)
    log_g = jnp.log1p(-jnp.exp2(-5.0 - jnp.arange(H, dtype=f32)))
    C = RET_CHUNK
    n_chunks = S // C
    i = jnp.arange(C, dtype=f32)
    diff = i[:, None] - i[None, :]
    inner_decay = jnp.where(diff >= 0, jnp.exp(log_g[:, None, None] * jnp.maximum(diff, 0.0)), 0.0)
    q_decay = jnp.exp(log_g[:, None] * (i + 1.0))[..., None]
    k_decay = jnp.exp(log_g[:, None] * (C - 1.0 - i))[..., None]
    c_decay = jnp.exp(log_g * C)[:, None, None]
    chunks = lambda t: t.reshape(B, n_chunks, C, H, DK).transpose(1, 0, 3, 2, 4)

    def step(R, inp):
        qc, kc, vc = inp
        att = jnp.einsum('bhqd,bhkd->bhqk', qc, kc) * inner_decay
        o = jnp.einsum('bhqk,bhkd->bhqd', att, vc) + jnp.einsum('bhqd,bhde->bhqe', qc, R) * q_decay
        R = R * c_decay + jnp.einsum('bhkd,bhke->bhde', kc * k_decay, vc)
        return R, o

    _, o = lax.scan(step, jnp.zeros((B, H, DK, DK), f32), (chunks(q), chunks(k), chunks(v)))
    o = o.transpose(1, 0, 3, 2, 4).reshape(B, S, H, DK)
    o = _head_norm(o, RET_GN_EPS).reshape(B, S, RET_WIDTH)
    return (jax.nn.silu(g.astype(f32)) * o).astype(p.dtype)


def _hier_moe(x, w_grp, b_grp, w_exp, b_exp, w_gate, w_up, w_down):
    B, S, D = x.shape
    T = B * S
    f32 = jnp.float32
    xt = x.reshape(T, D)
    grp_logits = (xt @ w_grp + b_grp).astype(f32)
    grp = jnp.argmax(grp_logits, axis=-1)
    grp_w = jnp.take_along_axis(jax.nn.softmax(grp_logits, -1), grp[:, None], axis=-1)
    exp_logits = (xt @ w_exp + b_exp).astype(f32).reshape(T, N_GROUPS, EXPERTS_PER_GROUP)
    in_grp = jnp.take_along_axis(exp_logits, grp[:, None, None], axis=1)[:, 0]
    top_val, top_idx = lax.top_k(in_grp, EXPERT_TOPK)
    gate_w = (jax.nn.softmax(top_val, -1) * grp_w).reshape(-1)
    expert = (grp[:, None] * EXPERTS_PER_GROUP + top_idx).reshape(-1)
    token = jnp.repeat(jnp.arange(T, dtype=jnp.int32), EXPERT_TOPK)
    n_assign = T * EXPERT_TOPK
    order = jnp.argsort(expert)
    e_s, t_s, w_s = expert[order], token[order], gate_w[order]
    counts = jnp.zeros((N_EXPERTS,), jnp.int32).at[expert].add(1)
    start = jnp.cumsum(counts) - counts
    padded = (counts + MOE_BLOCK - 1) // MOE_BLOCK * MOE_BLOCK
    pad_end = jnp.cumsum(padded)
    slot = (pad_end - padded)[e_s] + jnp.arange(n_assign, dtype=jnp.int32) - start[e_s]
    n_blocks = -(-n_assign // MOE_BLOCK) + N_EXPERTS
    n_slots = n_blocks * MOE_BLOCK
    slot_tok = jnp.full((n_slots,), T, jnp.int32).at[slot].set(t_s)
    slot_w = jnp.zeros((n_slots,), f32).at[slot].set(w_s)
    blk_expert = jnp.minimum(jnp.searchsorted(pad_end, jnp.arange(n_blocks, dtype=jnp.int32) * MOE_BLOCK, side='right'), N_EXPERTS - 1)
    x_slots = jnp.concatenate([xt, jnp.zeros((1, D), xt.dtype)])[slot_tok].reshape(n_blocks, MOE_BLOCK, D)

    def expert_block(args):
        xb, e = args
        h = jax.nn.silu(xb @ w_gate[e]) * (xb @ w_up[e])
        return h @ w_down[e]

    y = lax.map(expert_block, (x_slots, blk_expert)).reshape(n_slots, D)
    out = jax.ops.segment_sum(y * slot_w[:, None].astype(y.dtype), slot_tok, num_segments=T + 1)[:T]
    return out.reshape(B, S, D)


def setup_inputs(seed: int = 0) -> dict:
    key = jax.random.key(seed)
    keys = iter(jax.random.split(key, 48))
    f32 = jnp.float32
    L, D = DEPTH, D_MODEL
    nrm = lambda shape, scale: jax.random.normal(next(keys), shape, f32) * scale
    around = lambda shape, center, scale: center + nrm(shape, scale)
    ratio = (jnp.arange(RW_WIDTH, dtype=f32) / (RW_WIDTH - 1)) ** 0.85
    return {
        'x': nrm((BATCH, SEQ, D), 1.0),
        'w_in_first': nrm((D, IN_COLS_FIRST), D ** -0.5),
        'w_in_deep': nrm((L - 1, D, IN_COLS_DEEP), D ** -0.5),
        'rw_mu_first': jax.random.uniform(next(keys), (RW_COLS_FIRST,), f32),
        'rw_mu_deep': jax.random.uniform(next(keys), (L - 1, RW_COLS_DEEP), f32),
        'rw_w0': -6.0 + 5.0 * ratio + nrm((L, RW_WIDTH), 0.1),
        'rw_w_up': nrm((L, RW_DECAY_LORA, RW_WIDTH), 0.5 * RW_DECAY_LORA ** -0.5),
        'rw_a0': nrm((L, RW_WIDTH), 0.1),
        'rw_a_up': nrm((L, RW_A_LORA, RW_WIDTH), RW_A_LORA ** -0.5),
        'rw_v0': around((L - 1, RW_WIDTH), 1.0, 0.1),
        'rw_v_up': nrm((L - 1, RW_V_LORA, RW_WIDTH), RW_V_LORA ** -0.5),
        'rw_g_up': nrm((L, RW_G_LORA, RW_WIDTH), RW_G_LORA ** -0.5),
        'rw_k_k': around((L, RW_WIDTH), 0.85, 0.05),
        'rw_k_a': around((L, RW_WIDTH), 1.0, 0.05),
        'rw_r_k': nrm((L, RW_HEADS, RW_HEAD), 0.1),
        'rw_lnx_g': around((L, RW_WIDTH), 1.0, 0.05),
        'rw_lnx_b': nrm((L, RW_WIDTH), 0.02),
        'nsa_cmp_pe': nrm((L, 2, CMP_LEN, NSA_HEAD), 0.2),
        'nsa_cmp_w1': nrm((L, 2, CMP_LEN * NSA_HEAD, CMP_HIDDEN), (CMP_LEN * NSA_HEAD) ** -0.5),
        'nsa_cmp_b1': nrm((L, 2, CMP_HIDDEN), 0.02),
        'nsa_cmp_w2': nrm((L, 2, CMP_HIDDEN, NSA_HEAD), CMP_HIDDEN ** -0.5),
        'w_br_rw': nrm((L, RW_WIDTH, D), RW_WIDTH ** -0.5),
        'w_br_nsa': nrm((L, NSA_WIDTH, D), NSA_WIDTH ** -0.5),
        'w_br_ret': nrm((L, RET_WIDTH, D), RET_WIDTH ** -0.5),
        'w_out': nrm((L, D, D), D ** -0.5 * DN_BETA),
        'ln1_g': around((L, D), 1.0, 0.05),
        'ln1_b': nrm((L, D), 0.02),
        'moe_w_grp': nrm((L, D, N_GROUPS), D ** -0.5),
        'moe_b_grp': nrm((L, N_GROUPS), 0.01),
        'moe_w_exp': nrm((L, D, N_EXPERTS), D ** -0.5),
        'moe_b_exp': nrm((L, N_EXPERTS), 0.01),
        'moe_w_gate': nrm((L, N_EXPERTS, D, EXPERT_FF), D ** -0.5),
        'moe_w_up': nrm((L, N_EXPERTS, D, EXPERT_FF), D ** -0.5),
        'moe_w_down': nrm((L, N_EXPERTS, EXPERT_FF, D), EXPERT_FF ** -0.5 * DN_BETA),
        'ln2_g': around((L, D), 1.0, 0.05),
        'ln2_b': nrm((L, D), 0.02),
    }


def reference(x, w_in_first, w_in_deep, rw_mu_first, rw_mu_deep, rw_w0, rw_w_up, rw_a0, rw_a_up, rw_v0, rw_v_up, rw_g_up, rw_k_k, rw_k_a, rw_r_k, rw_lnx_g, rw_lnx_b, nsa_cmp_pe, nsa_cmp_w1, nsa_cmp_b1, nsa_cmp_w2, w_br_rw, w_br_nsa, w_br_ret, w_out, ln1_g, ln1_b, moe_w_grp, moe_b_grp, moe_w_exp, moe_b_exp, moe_w_gate, moe_w_up, moe_w_down, ln2_g, ln2_b):
    v_first = None
    for l in range(DEPTH):
        first = l == 0
        w_in = w_in_first if first else w_in_deep[l - 1]
        mu = rw_mu_first if first else rw_mu_deep[l - 1]
        rw_cols = RW_COLS_FIRST if first else RW_COLS_DEEP
        proj = x @ w_in
        p_rw, p_nsa, p_ret, g_rw, g_nsa, g_ret = _split(proj, (rw_cols, NSA_COLS, RET_COLS, D_MODEL, D_MODEL, D_MODEL))
        y_rw, v_first = _rwkv7_time_mix(p_rw, mu, rw_w0[l], rw_w_up[l], rw_a0[l], rw_a_up[l], rw_g_up[l],
                                        rw_k_k[l], rw_k_a[l], rw_r_k[l], rw_lnx_g[l], rw_lnx_b[l], v_first,
                                        None if first else rw_v0[l - 1], None if first else rw_v_up[l - 1])
        y_nsa = _nsa(p_nsa, nsa_cmp_pe[l], nsa_cmp_w1[l], nsa_cmp_b1[l], nsa_cmp_w2[l])
        y_ret = _retention(p_ret)
        merged = (jax.nn.sigmoid(g_rw) * (y_rw @ w_br_rw[l])
                  + jax.nn.sigmoid(g_nsa) * (y_nsa @ w_br_nsa[l])
                  + jax.nn.sigmoid(g_ret) * (y_ret @ w_br_ret[l]))
        x = _layer_norm(DN_ALPHA * x + merged @ w_out[l], ln1_g[l], ln1_b[l])
        moe = _hier_moe(x, moe_w_grp[l], moe_b_grp[l], moe_w_exp[l], moe_b_exp[l], moe_w_gate[l], moe_w_up[l], moe_w_down[l])
        x = _layer_norm(DN_ALPHA * x + moe, ln2_g[l], ln2_b[l])
    return x
```

```python
import functools

import jax
import jax.numpy as jnp
from jax import lax
import numpy as np
from jax.experimental import pallas as pl
from jax.experimental.pallas import tpu as pltpu

D_MODEL = 2048
DEPTH = 2

RW_HEADS = 16
RW_HEAD = 64
RW_WIDTH = RW_HEADS * RW_HEAD
RW_DECAY_LORA = 96
RW_A_LORA = 96
RW_V_LORA = 64
RW_G_LORA = 256
RW_GN_EPS = 64e-5
RW_BASE_SIZES = (RW_WIDTH, RW_WIDTH, RW_WIDTH, RW_DECAY_LORA, RW_A_LORA, RW_G_LORA)
RW_COLS_FIRST = sum(RW_BASE_SIZES)
RW_COLS_DEEP = RW_COLS_FIRST + RW_V_LORA

NSA_HEADS = 16
NSA_KV_GROUPS = 4
NSA_HPG = NSA_HEADS // NSA_KV_GROUPS
NSA_HEAD = 64
NSA_WIDTH = NSA_HEADS * NSA_HEAD
NSA_KV = NSA_KV_GROUPS * NSA_HEAD
CMP_LEN = 32
CMP_STRIDE = 16
CMP_HIDDEN = 128
SEL_BLOCK = 64
SEL_TOPN = 16
WINDOW = 512
Q_BLOCK = 128
FORCE_BONUS = 1e4
NEG_INF = -1e30
NSA_SIZES = (NSA_WIDTH,) + (NSA_KV,) * 6 + (3 * NSA_HEADS,)
NSA_COLS = sum(NSA_SIZES)

RET_HEADS = 8
RET_HEAD = 128
RET_WIDTH = RET_HEADS * RET_HEAD
RET_CHUNK = 128
RET_THETA = 10000.0
RET_GN_EPS = 1e-5
RET_SIZES = (RET_WIDTH,) * 4
RET_COLS = sum(RET_SIZES)

N_GROUPS = 4
EXPERTS_PER_GROUP = 8
N_EXPERTS = N_GROUPS * EXPERTS_PER_GROUP
EXPERT_FF = 512
EXPERT_TOPK = 2
MOE_BLOCK = 128

DN_ALPHA = (2 * DEPTH) ** 0.25
LN_EPS = 1e-5

LANES = 128


def _matmul_body(a_ref, b_ref, o_ref):
    a = a_ref[...].astype(jnp.bfloat16)
    b = b_ref[...].astype(jnp.bfloat16)
    o_ref[...] = jnp.dot(a, b, preferred_element_type=jnp.float32).astype(o_ref.dtype)


def _matmul(a, b, tm=512, tn=512):
    m, k = a.shape
    n = b.shape[1]
    n_pad = -(-n // tn) * tn
    if n_pad != n:
        b = jnp.pad(b, ((0, 0), (0, n_pad - n)))
    out = pl.pallas_call(
        _matmul_body,
        grid=(m // tm, n_pad // tn),
        in_specs=[pl.BlockSpec((tm, k), lambda i, j: (i, 0)),
                  pl.BlockSpec((k, tn), lambda i, j: (0, j))],
        out_specs=pl.BlockSpec((tm, tn), lambda i, j: (i, j)),
        out_shape=jax.ShapeDtypeStruct((m, n_pad), jnp.float32),
        compiler_params=pltpu.CompilerParams(
            dimension_semantics=("parallel", "parallel"),
            vmem_limit_bytes=48 * 1024 * 1024),
        name="dense_matmul",
    )(a, b)
    return out[:, :n] if n_pad != n else out


def _split(p, sizes):
    idx = [int(i) for i in np.cumsum(sizes)[:-1]]
    return jnp.split(p, idx, axis=-1)


def _layer_norm(x, g, b):
    xf = x.astype(jnp.float32)
    mu = jnp.mean(xf, -1, keepdims=True)
    var = jnp.mean(jnp.square(xf - mu), -1, keepdims=True)
    return ((xf - mu) * lax.rsqrt(var + LN_EPS)).astype(x.dtype) * g + b


def _head_norm(x, eps):
    x = x.astype(jnp.float32)
    mu = jnp.mean(x, -1, keepdims=True)
    var = jnp.mean(jnp.square(x - mu), -1, keepdims=True)
    return (x - mu) * lax.rsqrt(var + eps)


def _token_shift(z, mu):
    prev = jnp.pad(z, ((0, 0), (1, 0), (0, 0)))[:, :-1]
    return z + (prev - z) * mu


def _masked_softmax(s, mask):
    s = jnp.where(mask, s, NEG_INF)
    p = jax.nn.softmax(s, axis=-1)
    return jnp.where(mask, p, 0.0)


def _rwkv7_time_mix(p, mu, w0, w_up, a0, a_up, g_up, k_k, k_a, r_k, lnx_g, lnx_b, v_first, v0, v_up):
    B, S, _ = p.shape
    H, N = RW_HEADS, RW_HEAD
    f32 = jnp.float32
    p = _token_shift(p, mu)
    if v_first is None:
        r, k, v, xw, xa, xg = _split(p, RW_BASE_SIZES)
        v_first = v
    else:
        r, k, v, xw, xa, xg, xv = _split(p, RW_BASE_SIZES + (RW_V_LORA,))
        v = v + (v_first - v) * jax.nn.sigmoid(v0 + xv @ v_up)
    w_log = -jax.nn.softplus(-(w0 + jnp.tanh(xw) @ w_up)) - 0.5
    decay = jnp.exp(-jnp.exp(w_log.astype(f32)))
    a = jax.nn.sigmoid(a0 + xa @ a_up)
    g = jax.nn.sigmoid(xg) @ g_up
    kk = (k * k_k).astype(f32).reshape(B, S, H, N)
    kk = kk / jnp.maximum(jnp.linalg.norm(kk, axis=-1, keepdims=True), 1e-12)
    k = k * (1.0 + (a - 1.0) * k_a)
    heads = lambda t: t.astype(f32).reshape(B, S, H, N)
    r_h, k_h, v_h, a_h, w_h = heads(r), heads(k), heads(v), heads(a), heads(decay)

    def step(state, inp):
        r_t, w_t, k_t, v_t, kk_t, a_t = inp
        s_kk = jnp.einsum('bhij,bhj->bhi', state, kk_t)
        state = (state * w_t[:, :, None, :] - s_kk[..., :, None] * (kk_t * a_t)[..., None, :]
                 + v_t[..., :, None] * k_t[..., None, :])
        return state, jnp.einsum('bhij,bhj->bhi', state, r_t)

    seq_first = lambda t: jnp.moveaxis(t, 1, 0)
    xs = (seq_first(r_h), seq_first(w_h), seq_first(k_h), seq_first(v_h), seq_first(kk), seq_first(a_h))
    _, y = lax.scan(step, jnp.zeros((B, H, N, N), f32), xs)
    y = jnp.moveaxis(y, 0, 1)
    y = _head_norm(y, RW_GN_EPS).reshape(B, S, RW_WIDTH) * lnx_g + lnx_b
    bonus = jnp.sum(r_h * k_h * r_k, -1, keepdims=True) * v_h
    out = (y + bonus.reshape(B, S, RW_WIDTH)) * g.astype(f32)
    return out.astype(p.dtype), v_first


def _nsa_compress(t, pe, w1, b1, w2):
    B, S, G, DH = t.shape
    n_cmp = (S - CMP_LEN) // CMP_STRIDE + 1
    idx = jnp.arange(n_cmp)[:, None] * CMP_STRIDE + jnp.arange(CMP_LEN)[None, :]
    blocks = t[:, idx] + pe[:, None, :]
    blocks = jnp.swapaxes(blocks, 2, 3).reshape(B, n_cmp, G, CMP_LEN * DH)
    return jax.nn.gelu(blocks @ w1 + b1) @ w2


def _nsa(p, cmp_pe, cmp_w1, cmp_b1, cmp_w2):
    B, S, _ = p.shape
    G, HPG, DH = NSA_KV_GROUPS, NSA_HPG, NSA_HEAD
    f32 = jnp.float32
    q, kc, vc, ks, vs, kw, vw, gate = _split(p, NSA_SIZES)
    kv = lambda t: t.reshape(B, S, G, DH)
    q = q.reshape(B, S, G, HPG, DH) * (DH ** -0.5)
    gate = jax.nn.sigmoid(gate).reshape(B, S, G, HPG, 3)
    k_cmp = _nsa_compress(kv(kc), cmp_pe[0], cmp_w1[0], cmp_b1[0], cmp_w2[0])
    v_cmp = _nsa_compress(kv(vc), cmp_pe[1], cmp_w1[1], cmp_b1[1], cmp_w2[1])
    n_cmp = k_cmp.shape[1]
    cmp_start = jnp.arange(n_cmp) * CMP_STRIDE
    cmp_end = cmp_start + CMP_LEN - 1
    n_sel = S // SEL_BLOCK
    top_n = min(SEL_TOPN, n_sel)
    sel_j = jnp.arange(n_sel)
    sel_start = sel_j * SEL_BLOCK
    overlap = ((cmp_start[:, None] < sel_start[None] + SEL_BLOCK)
               & (cmp_start[:, None] + CMP_LEN > sel_start[None])).astype(f32)
    to_blocks = lambda t: kv(t).reshape(B, n_sel, SEL_BLOCK, G, DH).transpose(0, 3, 1, 2, 4)
    ks_b, vs_b = to_blocks(ks), to_blocks(vs)
    pad = lambda t: jnp.pad(kv(t), ((0, 0), (WINDOW, 0), (0, 0), (0, 0)))
    kw_p, vw_p = pad(kw), pad(vw)
    gather_blocks = jax.vmap(jax.vmap(lambda blocks, ids: blocks[ids]))
    win_off = jnp.arange(WINDOW + Q_BLOCK) - WINDOW

    def query_block(qi):
        q0 = qi * Q_BLOCK
        t = q0 + jnp.arange(Q_BLOCK)
        qb = lax.dynamic_slice_in_dim(q, q0, Q_BLOCK, axis=1)
        gb = lax.dynamic_slice_in_dim(gate, q0, Q_BLOCK, axis=1)
        s = jnp.einsum('bqghd,bcgd->bghqc', qb, k_cmp).astype(f32)
        p_cmp = _masked_softmax(s, cmp_end[None, :] <= t[:, None])
        o_cmp = jnp.einsum('bghqc,bcgd->bqghd', p_cmp.astype(v_cmp.dtype), v_cmp)
        imp = jnp.einsum('bghqc,cn->bgqn', p_cmp, overlap)
        cur = (t // SEL_BLOCK)[:, None]
        causal = sel_j[None] <= cur
        forced = (sel_j[None] == 0) | (sel_j[None] == cur) | (sel_j[None] == cur - 1)
        score = jnp.where(causal, imp + jnp.where(forced, FORCE_BONUS, 0.0), NEG_INF)
        top_val, top_idx = lax.top_k(score, top_n)
        k_sel = gather_blocks(ks_b, top_idx).reshape(B, G, Q_BLOCK, top_n * SEL_BLOCK, DH)
        v_sel = gather_blocks(vs_b, top_idx).reshape(B, G, Q_BLOCK, top_n * SEL_BLOCK, DH)
        tok = top_idx[..., None] * SEL_BLOCK + jnp.arange(SEL_BLOCK)
        m_sel = (top_val[..., None] > 0.5 * NEG_INF) & (tok <= t[:, None, None])
        m_sel = m_sel.reshape(B, G, 1, Q_BLOCK, top_n * SEL_BLOCK)
        s = jnp.einsum('bqghd,bgqkd->bghqk', qb, k_sel).astype(f32)
        p_sel = _masked_softmax(s, m_sel)
        o_sel = jnp.einsum('bghqk,bgqkd->bqghd', p_sel.astype(v_sel.dtype), v_sel)
        kwb = lax.dynamic_slice_in_dim(kw_p, q0, WINDOW + Q_BLOCK, axis=1)
        vwb = lax.dynamic_slice_in_dim(vw_p, q0, WINDOW + Q_BLOCK, axis=1)
        s_pos = q0 + win_off
        dist = t[:, None] - s_pos[None]
        m_win = (dist >= 0) & (dist < WINDOW) & (s_pos[None] >= 0)
        s = jnp.einsum('bqghd,bkgd->bghqk', qb, kwb).astype(f32)
        p_win = _masked_softmax(s, m_win)
        o_win = jnp.einsum('bghqk,bkgd->bqghd', p_win.astype(vwb.dtype), vwb)
        return gb[..., 0:1] * o_cmp + gb[..., 1:2] * o_sel + gb[..., 2:3] * o_win

    out = lax.map(query_block, jnp.arange(S // Q_BLOCK))
    return jnp.moveaxis(out, 0, 1).reshape(B, S, NSA_WIDTH)


def _rotate_half(t, cos, sin):
    t1, t2 = jnp.split(t, 2, axis=-1)
    c, s = cos[:, None, :], sin[:, None, :]
    return jnp.concatenate([t1 * c - t2 * s, t1 * s + t2 * c], axis=-1)


def _retention(p):
    B, S, _ = p.shape
    H, DK = RET_HEADS, RET_HEAD
    f32 = jnp.float32
    q, k, v, g = _split(p, RET_SIZES)
    heads = lambda t: t.astype(f32).reshape(B, S, H, DK)
    inv_freq = 1.0 / (RET_THETA ** jnp.linspace(0.0, 1.0, DK // 2))
    ang = jnp.arange(S, dtype=f32)[:, None] * inv_freq[None, :]
    cos, sin = jnp.cos(ang), jnp.sin(ang)
    q = _rotate_half(heads(q), cos, sin)
    k = _rotate_half(heads(k), cos, sin) * (DK ** -0.5)
    v = heads(v)
    log_g = jnp.log1p(-jnp.exp2(-5.0 - jnp.arange(H, dtype=f32)))
    C = RET_CHUNK
    n_chunks = S // C
    i = jnp.arange(C, dtype=f32)
    diff = i[:, None] - i[None, :]
    inner_decay = jnp.where(diff >= 0, jnp.exp(log_g[:, None, None] * jnp.maximum(diff, 0.0)), 0.0)
    q_decay = jnp.exp(log_g[:, None] * (i + 1.0))[..., None]
    k_decay = jnp.exp(log_g[:, None] * (C - 1.0 - i))[..., None]
    c_decay = jnp.exp(log_g * C)[:, None, None]
    chunks = lambda t: t.reshape(B, n_chunks, C, H, DK).transpose(1, 0, 3, 2, 4)

    def step(R, inp):
        qc, kc, vc = inp
        att = jnp.einsum('bhqd,bhkd->bhqk', qc, kc) * inner_decay
        o = jnp.einsum('bhqk,bhkd->bhqd', att, vc) + jnp.einsum('bhqd,bhde->bhqe', qc, R) * q_decay
        R = R * c_decay + jnp.einsum('bhkd,bhke->bhde', kc * k_decay, vc)
        return R, o

    _, o = lax.scan(step, jnp.zeros((B, H, DK, DK), f32), (chunks(q), chunks(k), chunks(v)))
    o = o.transpose(1, 0, 3, 2, 4).reshape(B, S, H, DK)
    o = _head_norm(o, RET_GN_EPS).reshape(B, S, RET_WIDTH)
    return (jax.nn.silu(g.astype(f32)) * o).astype(p.dtype)


def _hier_moe(x, w_grp, b_grp, w_exp, b_exp, w_gate, w_up, w_down):
    B, S, D = x.shape
    T = B * S
    f32 = jnp.float32
    xt = x.reshape(T, D)
    grp_logits = (xt @ w_grp + b_grp).astype(f32)
    grp = jnp.argmax(grp_logits, axis=-1)
    grp_w = jnp.take_along_axis(jax.nn.softmax(grp_logits, -1), grp[:, None], axis=-1)
    exp_logits = (xt @ w_exp + b_exp).astype(f32).reshape(T, N_GROUPS, EXPERTS_PER_GROUP)
    in_grp = jnp.take_along_axis(exp_logits, grp[:, None, None], axis=1)[:, 0]
    top_val, top_idx = lax.top_k(in_grp, EXPERT_TOPK)
    gate_w = (jax.nn.softmax(top_val, -1) * grp_w).reshape(-1)
    expert = (grp[:, None] * EXPERTS_PER_GROUP + top_idx).reshape(-1)
    token = jnp.repeat(jnp.arange(T, dtype=jnp.int32), EXPERT_TOPK)
    n_assign = T * EXPERT_TOPK
    order = jnp.argsort(expert)
    e_s, t_s, w_s = expert[order], token[order], gate_w[order]
    counts = jnp.zeros((N_EXPERTS,), jnp.int32).at[expert].add(1)
    start = jnp.cumsum(counts) - counts
    padded = (counts + MOE_BLOCK - 1) // MOE_BLOCK * MOE_BLOCK
    pad_end = jnp.cumsum(padded)
    slot = (pad_end - padded)[e_s] + jnp.arange(n_assign, dtype=jnp.int32) - start[e_s]
    n_blocks = -(-n_assign // MOE_BLOCK) + N_EXPERTS
    n_slots = n_blocks * MOE_BLOCK
    slot_tok = jnp.full((n_slots,), T, jnp.int32).at[slot].set(t_s)
    slot_w = jnp.zeros((n_slots,), f32).at[slot].set(w_s)
    blk_expert = jnp.minimum(jnp.searchsorted(pad_end, jnp.arange(n_blocks, dtype=jnp.int32) * MOE_BLOCK, side='right'), N_EXPERTS - 1)
    x_slots = jnp.concatenate([xt, jnp.zeros((1, D), xt.dtype)])[slot_tok].reshape(n_blocks, MOE_BLOCK, D)

    def expert_block(args):
        xb, e = args
        h = jax.nn.silu(xb @ w_gate[e]) * (xb @ w_up[e])
        return h @ w_down[e]

    y = lax.map(expert_block, (x_slots, blk_expert)).reshape(n_slots, D)
    out = jax.ops.segment_sum(y * slot_w[:, None].astype(y.dtype), slot_tok, num_segments=T + 1)[:T]
    return out.reshape(B, S, D)


def kernel(x, w_in_first, w_in_deep, rw_mu_first, rw_mu_deep, rw_w0, rw_w_up, rw_a0, rw_a_up, rw_v0, rw_v_up, rw_g_up, rw_k_k, rw_k_a, rw_r_k, rw_lnx_g, rw_lnx_b, nsa_cmp_pe, nsa_cmp_w1, nsa_cmp_b1, nsa_cmp_w2, w_br_rw, w_br_nsa, w_br_ret, w_out, ln1_g, ln1_b, moe_w_grp, moe_b_grp, moe_w_exp, moe_b_exp, moe_w_gate, moe_w_up, moe_w_down, ln2_g, ln2_b):
    B, S, D = x.shape
    v_first = None
    for l in range(DEPTH):
        first = l == 0
        w_in = w_in_first if first else w_in_deep[l - 1]
        mu = rw_mu_first if first else rw_mu_deep[l - 1]
        rw_cols = RW_COLS_FIRST if first else RW_COLS_DEEP
        proj = _matmul(x.reshape(S, D), w_in).reshape(B, S, -1)
        p_rw, p_nsa, p_ret, g_rw, g_nsa, g_ret = _split(proj, (rw_cols, NSA_COLS, RET_COLS, D_MODEL, D_MODEL, D_MODEL))
        y_rw, v_first = _rwkv7_time_mix(p_rw, mu, rw_w0[l], rw_w_up[l], rw_a0[l], rw_a_up[l], rw_g_up[l],
                                        rw_k_k[l], rw_k_a[l], rw_r_k[l], rw_lnx_g[l], rw_lnx_b[l], v_first,
                                        None if first else rw_v0[l - 1], None if first else rw_v_up[l - 1])
        y_nsa = _nsa(p_nsa, nsa_cmp_pe[l], nsa_cmp_w1[l], nsa_cmp_b1[l], nsa_cmp_w2[l])
        y_ret = _retention(p_ret)
        mm = lambda a, w: _matmul(a.reshape(S, -1), w).reshape(B, S, -1)
        merged = (jax.nn.sigmoid(g_rw) * mm(y_rw, w_br_rw[l])
                  + jax.nn.sigmoid(g_nsa) * mm(y_nsa, w_br_nsa[l])
                  + jax.nn.sigmoid(g_ret) * mm(y_ret, w_br_ret[l]))
        x = _layer_norm(DN_ALPHA * x + mm(merged, w_out[l]), ln1_g[l], ln1_b[l])
        moe = _hier_moe(x, moe_w_grp[l], moe_b_grp[l], moe_w_exp[l], moe_b_exp[l], moe_w_gate[l], moe_w_up[l], moe_w_down[l])
        x = _layer_norm(DN_ALPHA * x + moe, ln2_g[l], ln2_b[l])
    return x
```

```python
import functools

import jax
import jax.numpy as jnp
from jax import lax
import numpy as np
from jax.experimental import pallas as pl
from jax.experimental.pallas import tpu as pltpu

D_MODEL = 2048
DEPTH = 2

RW_HEADS = 16
RW_HEAD = 64
RW_WIDTH = RW_HEADS * RW_HEAD
RW_DECAY_LORA = 96
RW_A_LORA = 96
RW_V_LORA = 64
RW_G_LORA = 256
RW_GN_EPS = 64e-5
RW_BASE_SIZES = (RW_WIDTH, RW_WIDTH, RW_WIDTH, RW_DECAY_LORA, RW_A_LORA, RW_G_LORA)
RW_COLS_FIRST = sum(RW_BASE_SIZES)
RW_COLS_DEEP = RW_COLS_FIRST + RW_V_LORA

NSA_HEADS = 16
NSA_KV_GROUPS = 4
NSA_HPG = NSA_HEADS // NSA_KV_GROUPS
NSA_HEAD = 64
NSA_WIDTH = NSA_HEADS * NSA_HEAD
NSA_KV = NSA_KV_GROUPS * NSA_HEAD
CMP_LEN = 32
CMP_STRIDE = 16
CMP_HIDDEN = 128
SEL_BLOCK = 64
SEL_TOPN = 16
WINDOW = 512
Q_BLOCK = 128
FORCE_BONUS = 1e4
NEG_INF = -1e30
NSA_SIZES = (NSA_WIDTH,) + (NSA_KV,) * 6 + (3 * NSA_HEADS,)
NSA_COLS = sum(NSA_SIZES)

RET_HEADS = 8
RET_HEAD = 128
RET_WIDTH = RET_HEADS * RET_HEAD
RET_CHUNK = 128
RET_THETA = 10000.0
RET_GN_EPS = 1e-5
RET_SIZES = (RET_WIDTH,) * 4
RET_COLS = sum(RET_SIZES)

N_GROUPS = 4
EXPERTS_PER_GROUP = 8
N_EXPERTS = N_GROUPS * EXPERTS_PER_GROUP
EXPERT_FF = 512
EXPERT_TOPK = 2
MOE_BLOCK = 128

DN_ALPHA = (2 * DEPTH) ** 0.25
LN_EPS = 1e-5

LANES = 128


def _matmul_body(a_ref, b_ref, o_ref):
    a = a_ref[...].astype(jnp.bfloat16)
    b = b_ref[...].astype(jnp.bfloat16)
    o_ref[...] = jnp.dot(a, b, preferred_element_type=jnp.float32).astype(o_ref.dtype)


def _matmul(a, b, tm=512, tn=512):
    m, k = a.shape
    n = b.shape[1]
    n_pad = -(-n // tn) * tn
    if n_pad != n:
        b = jnp.pad(b, ((0, 0), (0, n_pad - n)))
    out = pl.pallas_call(
        _matmul_body,
        grid=(m // tm, n_pad // tn),
        in_specs=[pl.BlockSpec((tm, k), lambda i, j: (i, 0)),
                  pl.BlockSpec((k, tn), lambda i, j: (0, j))],
        out_specs=pl.BlockSpec((tm, tn), lambda i, j: (i, j)),
        out_shape=jax.ShapeDtypeStruct((m, n_pad), jnp.float32),
        compiler_params=pltpu.CompilerParams(
            dimension_semantics=("parallel", "parallel"),
            vmem_limit_bytes=48 * 1024 * 1024),
        name="dense_matmul",
    )(a, b)
    return out[:, :n] if n_pad != n else out


def _split(p, sizes):
    idx = [int(i) for i in np.cumsum(sizes)[:-1]]
    return jnp.split(p, idx, axis=-1)


def _layer_norm(x, g, b):
    xf = x.astype(jnp.float32)
    mu = jnp.mean(xf, -1, keepdims=True)
    var = jnp.mean(jnp.square(xf - mu), -1, keepdims=True)
    return ((xf - mu) * lax.rsqrt(var + LN_EPS)).astype(x.dtype) * g + b


def _head_norm(x, eps):
    x = x.astype(jnp.float32)
    mu = jnp.mean(x, -1, keepdims=True)
    var = jnp.mean(jnp.square(x - mu), -1, keepdims=True)
    return (x - mu) * lax.rsqrt(var + eps)


def _token_shift(z, mu):
    prev = jnp.pad(z, ((0, 0), (1, 0), (0, 0)))[:, :-1]
    return z + (prev - z) * mu


def _masked_softmax(s, mask):
    s = jnp.where(mask, s, NEG_INF)
    p = jax.nn.softmax(s, axis=-1)
    return jnp.where(mask, p, 0.0)


RW_CHUNK = 64
LORA_PAD = LANES
RW_OFF_XW = 3 * RW_WIDTH
RW_OFF_XA = RW_OFF_XW + LORA_PAD
RW_OFF_XG = RW_OFF_XA + LORA_PAD
RW_OFF_XV = RW_OFF_XG + RW_G_LORA
RW_PAD_SIZES = (RW_WIDTH, RW_WIDTH, RW_WIDTH, LORA_PAD, LORA_PAD, RW_G_LORA)

_NT = (((1,), (1,)), ((), ()))
_TN = (((0,), (0,)), ((), ()))


def _bf(x):
    return x.astype(jnp.bfloat16)


def _dot(a, b, dims=None):
    if dims is None:
        return jnp.dot(_bf(a), _bf(b), preferred_element_type=jnp.float32)
    return lax.dot_general(_bf(a), _bf(b), dims, preferred_element_type=jnp.float32)


def _split3(x):
    h = _bf(x)
    r = x - h.astype(jnp.float32)
    m = _bf(r)
    l = _bf(r - m.astype(jnp.float32))
    return h, m, l


def _dot_exact_rhs(a, b_bf16, passes):
    acc = None
    for p in _split3(a)[:passes]:
        t = jnp.dot(p, b_bf16, preferred_element_type=jnp.float32)
        acc = t if acc is None else acc + t
    return acc


def _dot_exact_lhs(a_bf16, b, passes):
    acc = None
    for p in _split3(b)[:passes]:
        t = jnp.dot(a_bf16, p, preferred_element_type=jnp.float32)
        acc = t if acc is None else acc + t
    return acc


def _sigmoid(x):
    return 1.0 / (1.0 + jnp.exp(-x))


def _softplus(x):
    return jnp.maximum(x, 0.0) + jnp.log(1.0 + jnp.exp(-jnp.abs(x)))


def _rwkv_body(first, *refs):
    if first:
        (p_ref, mu_ref, w0_ref, wup_ref, a0_ref, aup_ref, gup_ref, kk_ref, ka_ref, rk_ref,
         lg_ref, lb_ref, ones_ref, y_ref, vf_out_ref, st_ref, prev_ref) = refs
    else:
        (p_ref, mu_ref, w0_ref, wup_ref, a0_ref, aup_ref, gup_ref, kk_ref, ka_ref, rk_ref,
         lg_ref, lb_ref, ones_ref, v0_ref, vup_ref, vf_in_ref, y_ref, st_ref, prev_ref) = refs
    C, H, N, W = RW_CHUNK, RW_HEADS, RW_HEAD, RW_WIDTH
    f32 = jnp.float32
    i = pl.program_id(0)

    @pl.when(i == 0)
    def _():
        st_ref[...] = jnp.zeros_like(st_ref)
        prev_ref[...] = jnp.zeros_like(prev_ref)

    z = p_ref[...]
    row = lax.broadcasted_iota(jnp.int32, z.shape, 0)
    prev = jnp.where(row == 0, prev_ref[...], pltpu.roll(z, 1, axis=0))
    prev_ref[...] = z[C - 1:C, :]
    xs = z + (prev - z) * mu_ref[...]

    r = xs[:, 0:W]
    k = xs[:, W:2 * W]
    v = xs[:, 2 * W:3 * W]
    xw = xs[:, RW_OFF_XW:RW_OFF_XW + LORA_PAD]
    xa = xs[:, RW_OFF_XA:RW_OFF_XA + LORA_PAD]
    xg = xs[:, RW_OFF_XG:RW_OFF_XG + RW_G_LORA]
    if first:
        vf_out_ref[...] = v
    else:
        xv = xs[:, RW_OFF_XV:RW_OFF_XV + LORA_PAD]
        v = v + (vf_in_ref[...] - v) * _sigmoid(v0_ref[...] + _dot(xv, vup_ref[...]))

    w_log = -_softplus(-(w0_ref[...] + _dot(jnp.tanh(xw), wup_ref[...]))) - 0.5
    logd = -jnp.exp(w_log)
    a = _sigmoid(a0_ref[...] + _dot(xa, aup_ref[...]))
    g = _dot(_sigmoid(xg), gup_ref[...])
    ones_blk = ones_ref[...]
    kk = k * kk_ref[...]
    kk_n2 = _dot_exact_rhs(kk * kk, ones_blk, 2)
    kk = kk / jnp.maximum(jnp.sqrt(kk_n2), 1e-12)
    k = k * (1.0 + (a - 1.0) * ka_ref[...])
    b = kk * a
    bonus = _dot_exact_rhs(r * k * rk_ref[...], ones_blk, 2) * v

    tr = lax.broadcasted_iota(jnp.int32, (C, C), 0)
    tc = lax.broadcasted_iota(jnp.int32, (C, C), 1)
    low_incl = tr >= tc
    low_strict = tr > tc
    cum = _dot_exact_lhs(low_incl.astype(jnp.bfloat16), logd, 3)
    total = cum[C - 1:C, :]
    e_in = jnp.exp(cum)
    e_ex = jnp.exp(cum - logd)
    e_neg = jnp.exp(-cum)
    e_rem = jnp.exp(total - cum)
    kkd = kk * e_ex
    kh = k * e_neg
    bh = b * e_neg
    rd = r * e_in
    khg = k * e_rem
    bhg = b * e_rem
    gam = jnp.exp(total)

    eye = (tr == tc).astype(f32)
    heads = range(H)
    hs = lambda t: [t[:, h * N:(h + 1) * N] for h in heads]
    kkd_h, kh_h, bh_h, rd_h, v_h, khg_h, bhg_h = (hs(t) for t in (kkd, kh, bh, rd, v, khg, bhg))
    st = [st_ref[h] for h in heads]
    gm = [_dot(jnp.concatenate([kkd_h[h], rd_h[h]], axis=0),
               jnp.concatenate([kh_h[h], bh_h[h]], axis=0), _NT) for h in heads]
    a_k = [jnp.where(low_strict, gm[h][0:C, 0:C], 0.0) for h in heads]
    a_b = [jnp.where(low_strict, gm[h][0:C, C:2 * C], 0.0) for h in heads]
    b_k = [jnp.where(low_incl, gm[h][C:2 * C, 0:C], 0.0) for h in heads]
    b_b = [jnp.where(low_incl, gm[h][C:2 * C, C:2 * C], 0.0) for h in heads]
    rhs = [_dot(kkd_h[h], st[h]) + _dot(a_k[h], v_h[h]) for h in heads]
    y0 = [_dot(rd_h[h], st[h]) + _dot(b_k[h], v_h[h]) for h in heads]
    t_inv = [eye - jnp.where((tr == tc + 1) & (tr % 2 == 1), a_b[h], 0.0) for h in heads]
    m = 2
    while m < C:
        rb, cb = tr // m, tc // m
        sib = (rb == cb + 1) & (rb % 2 == 1)
        tmp = [_dot(jnp.where(sib, a_b[h], 0.0), t_inv[h]) for h in heads]
        t_inv = [t_inv[h] - _dot(t_inv[h], tmp[h]) for h in heads]
        m *= 2
    zz = [_dot(t_inv[h], rhs[h]) for h in heads]
    ys = [y0[h] - _dot(b_b[h], zz[h]) for h in heads]
    upd = [_dot(khg_h[h], v_h[h], _TN) - _dot(bhg_h[h], zz[h], _TN) for h in heads]
    for h in heads:
        st_ref[h] = st[h] * jnp.transpose(gam[:, h * N:(h + 1) * N]) + upd[h]
    y = jnp.concatenate(ys, axis=1)

    inv_n = 1.0 / N
    mean = _dot_exact_rhs(y, ones_blk, 2) * inv_n
    yc = y - mean
    var = _dot_exact_rhs(yc * yc, ones_blk, 2) * inv_n
    yn = yc * lax.rsqrt(var + RW_GN_EPS) * lg_ref[...] + lb_ref[...]
    y_ref[...] = (yn + bonus) * g


def _pad_rows(w, rows):
    return jnp.pad(w, ((0, rows - w.shape[0]), (0, 0)))


def _pad_cols(p, sizes, padded):
    parts = _split(p, sizes)
    return jnp.concatenate([jnp.pad(t, [(0, 0)] * (t.ndim - 1) + [(0, n - t.shape[-1])])
                            for t, n in zip(parts, padded)], -1)


def _rwkv7_time_mix(p, mu, w0, w_up, a0, a_up, g_up, k_k, k_a, r_k, lnx_g, lnx_b, v_first, v0, v_up):
    S, cols = p.shape
    first = v_first is None
    C, W = RW_CHUNK, RW_WIDTH
    row = lambda t: t.reshape(1, -1)
    hid = np.arange(W) // RW_HEAD
    ones_blk = jnp.asarray(hid[:, None] == hid[None, :], jnp.bfloat16)
    full = lambda shape: pl.BlockSpec(shape, lambda i: (0,) * len(shape))
    tok = lambda width: pl.BlockSpec((C, width), lambda i: (i, 0))
    args = [p, row(mu), row(w0), _pad_rows(w_up, LORA_PAD), row(a0), _pad_rows(a_up, LORA_PAD), g_up,
            row(k_k), row(k_a), row(r_k), row(lnx_g), row(lnx_b), ones_blk]
    specs = [tok(cols), full((1, cols)), full((1, W)), full((LORA_PAD, W)), full((1, W)), full((LORA_PAD, W)),
             full((RW_G_LORA, W)), full((1, W)), full((1, W)), full((1, W)), full((1, W)), full((1, W)),
             full((W, W))]
    if first:
        out_shape = (jax.ShapeDtypeStruct((S, W), jnp.float32), jax.ShapeDtypeStruct((S, W), jnp.float32))
        out_specs = (tok(W), tok(W))
    else:
        args += [row(v0), _pad_rows(v_up, LORA_PAD), v_first]
        specs += [full((1, W)), full((LORA_PAD, W)), tok(W)]
        out_shape = jax.ShapeDtypeStruct((S, W), jnp.float32)
        out_specs = tok(W)
    res = pl.pallas_call(
        functools.partial(_rwkv_body, first),
        grid=(S // C,),
        in_specs=specs,
        out_specs=out_specs,
        out_shape=out_shape,
        scratch_shapes=[pltpu.VMEM((RW_HEADS, RW_HEAD, RW_HEAD), jnp.float32),
                        pltpu.VMEM((1, cols), jnp.float32)],
        compiler_params=pltpu.CompilerParams(dimension_semantics=("arbitrary",),
                                             vmem_limit_bytes=48 * 1024 * 1024),
        name="rwkv7_chunked",
    )(*args)
    if first:
        return res[0], res[1]
    return res, v_first


CMP_ROWS = 256


def _compress_body(r_ref, nx_ref, pe_ref, wa_ref, wb_ref, b1_ref, w2_ref, o_ref):
    r = r_ref[...]
    rb, half = r.shape
    row = lax.broadcasted_iota(jnp.int32, r.shape, 0)
    nxt = jnp.where(row == rb - 1, nx_ref[0:1, :], pltpu.roll(r, rb - 1, axis=0))
    pe = pe_ref[...]
    h = (_dot(r + pe[:, :half], wa_ref[...]) + _dot(nxt + pe[:, half:], wb_ref[...]) + b1_ref[...])
    h = 0.5 * h * (1.0 + jnp.tanh(0.7978845608028654 * (h + 0.044715 * (h * h * h))))
    o_ref[...] = _dot(h, w2_ref[...])


def _nsa_compress(t, pe, w1, b1, w2):
    S = t.shape[0]
    G, DH, L, ST = NSA_KV_GROUPS, NSA_HEAD, CMP_LEN, CMP_STRIDE
    nr = S // ST
    rb = min(CMP_ROWS, nr)
    cols = ST * G * DH
    r = t.reshape(nr, cols)
    eye_g = jnp.eye(G, dtype=w1.dtype)
    big = jnp.einsum('ldh,gk->lgdkh', w1.reshape(L, DH, CMP_HIDDEN), eye_g).reshape(L * G * DH, G * CMP_HIDDEN)
    wa, wb = big[:cols], big[cols:]
    pe_big = jnp.broadcast_to(pe[:, None, :], (L, G, DH)).reshape(1, L * G * DH)
    b1_big = jnp.tile(b1, G).reshape(1, G * CMP_HIDDEN)
    w2_big = jnp.einsum('hd,gk->ghkd', w2, eye_g).reshape(G * CMP_HIDDEN, G * DH)
    full = lambda shape: pl.BlockSpec(shape, lambda i: (0, 0))
    return pl.pallas_call(
        _compress_body,
        grid=(nr // rb,),
        in_specs=[pl.BlockSpec((rb, cols), lambda i: (i, 0)),
                  pl.BlockSpec((8, cols), lambda i: (jnp.minimum((i + 1) * (rb // 8), nr // 8 - 1), 0)),
                  full((1, 2 * cols)), full((cols, G * CMP_HIDDEN)), full((cols, G * CMP_HIDDEN)),
                  full((1, G * CMP_HIDDEN)), full((G * CMP_HIDDEN, G * DH))],
        out_specs=pl.BlockSpec((rb, G * DH), lambda i: (i, 0)),
        out_shape=jax.ShapeDtypeStruct((nr, G * DH), jnp.float32),
        compiler_params=pltpu.CompilerParams(dimension_semantics=("parallel",),
                                             vmem_limit_bytes=48 * 1024 * 1024),
        name="nsa_compress",
    )(r, r, pe_big, _bf(wa), _bf(wb), b1_big, _bf(w2_big))


NSA_QB = 128
NSA_KC = 512
NSA_WC = 128
NSA_GATE_PAD = LANES
NSA_PCOLS = NSA_WIDTH + 6 * NSA_KV + NSA_GATE_PAD
NSA_GATE_ROWS = 16


def _softmax_cols(s, mask):
    s = jnp.where(mask, s, NEG_INF)
    m = jnp.max(s, axis=0, keepdims=True)
    p = jnp.where(mask, jnp.exp(s - m), 0.0)
    l = jnp.sum(p, axis=0, keepdims=True)
    return p * (1.0 / jnp.where(l > 0.0, l, 1.0))


def _nsa_body(*refs):
    n_win = (WINDOW + NSA_QB) // NSA_WC
    qt_ref, kvc_ref, kvct_ref, kvs_ref, kvst_ref = refs[:5]
    kvw_refs = refs[5:5 + n_win]
    kvwt_refs = refs[5 + n_win:5 + 2 * n_win]
    ovl_ref, gate_ref, o_ref, bias_ref, acc_ref = refs[5 + 2 * n_win:]
    QB, HPG, DH = NSA_QB, NSA_HPG, NSA_HEAD
    f32 = jnp.float32
    qi = pl.program_id(1)
    q0 = qi * QB
    qt = qt_ref[...]
    ncmp = kvc_ref.shape[0]
    t_row = q0 + lax.broadcasted_iota(jnp.int32, (1, QB), 1)
    head = lambda a, h: a[:, h * QB:(h + 1) * QB]

    s = jnp.dot(kvc_ref[...], qt, preferred_element_type=f32)
    c_end = lax.broadcasted_iota(jnp.int32, (ncmp, QB), 0) * CMP_STRIDE + (CMP_LEN - 1)
    m_cmp = c_end <= t_row
    kvct = kvct_ref[...]
    p_sum = None
    o_cmp = []
    for h in range(HPG):
        p = _softmax_cols(head(s, h), m_cmp)
        o_cmp.append(jnp.dot(kvct, _bf(p), preferred_element_type=f32))
        p_sum = p if p_sum is None else p_sum + p
    imp = _dot_exact_lhs(ovl_ref[...], p_sum, 2)
    nsel = imp.shape[0]

    n_row = lax.broadcasted_iota(jnp.int32, (nsel, QB), 0)
    cur = t_row // SEL_BLOCK
    forced = (n_row == 0) | (n_row == cur) | (n_row == cur - 1)
    score = jnp.where(n_row <= cur, imp + jnp.where(forced, FORCE_BONUS, 0.0), NEG_INF)
    bias = jnp.full((nsel, QB), NEG_INF, f32)
    for _ in range(min(SEL_TOPN, nsel)):
        best = jnp.max(score, axis=0, keepdims=True)
        first = jnp.min(jnp.where(score == best, n_row, nsel), axis=0, keepdims=True)
        hit = n_row == first
        bias = jnp.where(hit & (best > 0.5 * NEG_INF), 0.0, bias)
        score = jnp.where(hit, -jnp.inf, score)
    bias_ref[...] = bias

    KC = NSA_KC
    n_sub = KC // SEL_BLOCK
    key_row = lax.broadcasted_iota(jnp.int32, (KC, QB), 0)
    acc_ref[...] = jnp.zeros_like(acc_ref)

    def sel_step(j, carry):
        ms, ls = carry
        k0 = pl.multiple_of(j * KC, KC)
        kv = kvs_ref[pl.ds(k0, KC), :]
        kvt = kvst_ref[:, pl.ds(k0, KC)]
        rows = [jnp.broadcast_to(bias_ref[pl.ds(j * n_sub + b, 1), :], (SEL_BLOCK, QB)) for b in range(n_sub)]
        madd = jnp.where(k0 + key_row <= t_row, jnp.concatenate(rows, axis=0), NEG_INF)
        sc = jnp.dot(kv, qt, preferred_element_type=f32)
        hh = range(HPG)
        sh = [head(sc, h) + madd for h in hh]
        new_m = [jnp.maximum(ms[h], jnp.max(sh[h], axis=0, keepdims=True)) for h in hh]
        alpha = [jnp.exp(ms[h] - new_m[h]) for h in hh]
        p = [jnp.exp(sh[h] - new_m[h]) for h in hh]
        new_l = [alpha[h] * ls[h] + jnp.sum(p[h], axis=0, keepdims=True) for h in hh]
        pv = [jnp.dot(kvt, _bf(p[h]), preferred_element_type=f32) for h in hh]
        for h in hh:
            acc_ref[h] = alpha[h] * acc_ref[h] + pv[h]
        return tuple(new_m), tuple(new_l)

    init = (tuple(jnp.full((1, QB), NEG_INF, f32) for _ in range(HPG)),
            tuple(jnp.zeros((1, QB), f32) for _ in range(HPG)))
    n_chunks = (q0 + QB + KC - 1) // KC
    _, l_sel = lax.fori_loop(0, n_chunks, sel_step, init)

    WC = NSA_WC
    s_win, m_win = [], []
    for c in range(n_win):
        s_pos = q0 - WINDOW + c * WC + lax.broadcasted_iota(jnp.int32, (WC, QB), 0)
        dist = t_row - s_pos
        m_win.append((dist >= 0) & (dist < WINDOW) & (s_pos >= 0))
        s_win.append(jnp.dot(kvw_refs[c][...], qt, preferred_element_type=f32))
    m_all = jnp.concatenate(m_win, axis=0)
    s_all = jnp.concatenate(s_win, axis=0)

    gate = _sigmoid(gate_ref[...])
    outs = []
    for h in range(HPG):
        p = _bf(_softmax_cols(head(s_all, h), m_all))
        o_win = None
        for c in range(n_win):
            t = jnp.dot(kvwt_refs[c][...], p[c * WC:(c + 1) * WC], preferred_element_type=f32)
            o_win = t if o_win is None else o_win + t
        o_sel = acc_ref[h] * (1.0 / l_sel[h])
        gr = lambda b: gate[h * 3 + b:h * 3 + b + 1, :]
        o = gr(0) * o_cmp[h] + gr(1) * o_sel + gr(2) * o_win
        outs.append(o[DH:, :])
    o_ref[...] = jnp.transpose(jnp.concatenate(outs, axis=0))


def _nsa(p, cmp_pe, cmp_w1, cmp_b1, cmp_w2):
    S = p.shape[0]
    G, HPG, DH, QB, WC = NSA_KV_GROUPS, NSA_HPG, NSA_HEAD, NSA_QB, NSA_WC
    W, KV = NSA_WIDTH, NSA_KV
    nqb = S // QB
    n_win = (WINDOW + QB) // WC
    q = p[:, :W]
    kc, vc, ks, vs, kw, vw = (p[:, W + i * KV:W + (i + 1) * KV] for i in range(6))
    gate = p[:, W + 6 * KV:W + 6 * KV + 3 * NSA_HEADS]
    k_cmp = _nsa_compress(kc, cmp_pe[0], cmp_w1[0], cmp_b1[0], cmp_w2[0])
    v_cmp = _nsa_compress(vc, cmp_pe[1], cmp_w1[1], cmp_b1[1], cmp_w2[1])
    pack = lambda k, v: _bf(jnp.concatenate([k.reshape(-1, G, DH), v.reshape(-1, G, DH)], -1)).transpose(1, 0, 2)
    kvc, kvs, kvw = pack(k_cmp, v_cmp), pack(ks, vs), pack(kw, vw)
    tr = lambda a: a.transpose(0, 2, 1)
    qh = q.reshape(nqb, QB, G, HPG, DH) * (DH ** -0.5)
    qh = _bf(jnp.concatenate([qh, jnp.zeros_like(qh)], -1)).transpose(2, 0, 4, 3, 1).reshape(G, nqb, LANES, HPG * QB)
    gate_t = jnp.pad(gate.reshape(S, G, 3 * HPG).transpose(1, 2, 0), ((0, 0), (0, NSA_GATE_ROWS - 3 * HPG), (0, 0)))
    ncmp, nsel = S // CMP_STRIDE, S // SEL_BLOCK
    c_start = np.arange(ncmp)[None, :] * CMP_STRIDE
    s_start = np.arange(nsel)[:, None] * SEL_BLOCK
    overlap_t = jnp.asarray((c_start < s_start + SEL_BLOCK) & (c_start + CMP_LEN > s_start), jnp.bfloat16)
    win_blk = lambda c: (lambda g, i: (g, jnp.maximum(i - WINDOW // WC + c, 0), 0))
    win_blk_t = lambda c: (lambda g, i: (g, 0, jnp.maximum(i - WINDOW // WC + c, 0)))
    in_specs = ([pl.BlockSpec((None, None, LANES, HPG * QB), lambda g, i: (g, i, 0, 0)),
                 pl.BlockSpec((None, ncmp, LANES), lambda g, i: (g, 0, 0)),
                 pl.BlockSpec((None, LANES, ncmp), lambda g, i: (g, 0, 0)),
                 pl.BlockSpec((None, S, LANES), lambda g, i: (g, 0, 0)),
                 pl.BlockSpec((None, LANES, S), lambda g, i: (g, 0, 0))]
                + [pl.BlockSpec((None, WC, LANES), win_blk(c)) for c in range(n_win)]
                + [pl.BlockSpec((None, LANES, WC), win_blk_t(c)) for c in range(n_win)]
                + [pl.BlockSpec((nsel, ncmp), lambda g, i: (0, 0)),
                   pl.BlockSpec((None, NSA_GATE_ROWS, QB), lambda g, i: (g, 0, i))])
    kvwt = tr(kvw)
    return pl.pallas_call(
        _nsa_body,
        grid=(G, nqb),
        in_specs=in_specs,
        out_specs=pl.BlockSpec((QB, HPG * DH), lambda g, i: (i, g)),
        out_shape=jax.ShapeDtypeStruct((S, W), jnp.float32),
        scratch_shapes=[pltpu.VMEM((nsel, QB), jnp.float32),
                        pltpu.VMEM((HPG, LANES, QB), jnp.float32)],
        compiler_params=pltpu.CompilerParams(dimension_semantics=("arbitrary", "arbitrary"),
                                             vmem_limit_bytes=56 * 1024 * 1024),
        name="nsa_attention",
    )(qh, kvc, tr(kvc), kvs, tr(kvs), *([kvw] * n_win), *([kvwt] * n_win), overlap_t, gate_t)


def _rotate_half(t, cos, sin):
    t1, t2 = jnp.split(t, 2, axis=-1)
    c, s = cos[:, None, :], sin[:, None, :]
    return jnp.concatenate([t1 * c - t2 * s, t1 * s + t2 * c], axis=-1)


def _retention(p):
    B, S, _ = p.shape
    H, DK = RET_HEADS, RET_HEAD
    f32 = jnp.float32
    q, k, v, g = _split(p, RET_SIZES)
    heads = lambda t: t.astype(f32).reshape(B, S, H, DK)
    inv_freq = 1.0 / (RET_THETA ** jnp.linspace(0.0, 1.0, DK // 2))
    ang = jnp.arange(S, dtype=f32)[:, None] * inv_freq[None, :]
    cos, sin = jnp.cos(ang), jnp.sin(ang)
    q = _rotate_half(heads(q), cos, sin)
    k = _rotate_half(heads(k), cos, sin) * (DK ** -0.5)
    v = heads(v)
    log_g = jnp.log1p(-jnp.exp2(-5.0 - jnp.arange(H, dtype=f32)))
    C = RET_CHUNK
    n_chunks = S // C
    i = jnp.arange(C, dtype=f32)
    diff = i[:, None] - i[None, :]
    inner_decay = jnp.where(diff >= 0, jnp.exp(log_g[:, None, None] * jnp.maximum(diff, 0.0)), 0.0)
    q_decay = jnp.exp(log_g[:, None] * (i + 1.0))[..., None]
    k_decay = jnp.exp(log_g[:, None] * (C - 1.0 - i))[..., None]
    c_decay = jnp.exp(log_g * C)[:, None, None]
    chunks = lambda t: t.reshape(B, n_chunks, C, H, DK).transpose(1, 0, 3, 2, 4)

    def step(R, inp):
        qc, kc, vc = inp
        att = jnp.einsum('bhqd,bhkd->bhqk', qc, kc) * inner_decay
        o = jnp.einsum('bhqk,bhkd->bhqd', att, vc) + jnp.einsum('bhqd,bhde->bhqe', qc, R) * q_decay
        R = R * c_decay + jnp.einsum('bhkd,bhke->bhde', kc * k_decay, vc)
        return R, o

    _, o = lax.scan(step, jnp.zeros((B, H, DK, DK), f32), (chunks(q), chunks(k), chunks(v)))
    o = o.transpose(1, 0, 3, 2, 4).reshape(B, S, H, DK)
    o = _head_norm(o, RET_GN_EPS).reshape(B, S, RET_WIDTH)
    return (jax.nn.silu(g.astype(f32)) * o).astype(p.dtype)


def _hier_moe(x, w_grp, b_grp, w_exp, b_exp, w_gate, w_up, w_down):
    B, S, D = x.shape
    T = B * S
    f32 = jnp.float32
    xt = x.reshape(T, D)
    grp_logits = (xt @ w_grp + b_grp).astype(f32)
    grp = jnp.argmax(grp_logits, axis=-1)
    grp_w = jnp.take_along_axis(jax.nn.softmax(grp_logits, -1), grp[:, None], axis=-1)
    exp_logits = (xt @ w_exp + b_exp).astype(f32).reshape(T, N_GROUPS, EXPERTS_PER_GROUP)
    in_grp = jnp.take_along_axis(exp_logits, grp[:, None, None], axis=1)[:, 0]
    top_val, top_idx = lax.top_k(in_grp, EXPERT_TOPK)
    gate_w = (jax.nn.softmax(top_val, -1) * grp_w).reshape(-1)
    expert = (grp[:, None] * EXPERTS_PER_GROUP + top_idx).reshape(-1)
    token = jnp.repeat(jnp.arange(T, dtype=jnp.int32), EXPERT_TOPK)
    n_assign = T * EXPERT_TOPK
    order = jnp.argsort(expert)
    e_s, t_s, w_s = expert[order], token[order], gate_w[order]
    counts = jnp.zeros((N_EXPERTS,), jnp.int32).at[expert].add(1)
    start = jnp.cumsum(counts) - counts
    padded = (counts + MOE_BLOCK - 1) // MOE_BLOCK * MOE_BLOCK
    pad_end = jnp.cumsum(padded)
    slot = (pad_end - padded)[e_s] + jnp.arange(n_assign, dtype=jnp.int32) - start[e_s]
    n_blocks = -(-n_assign // MOE_BLOCK) + N_EXPERTS
    n_slots = n_blocks * MOE_BLOCK
    slot_tok = jnp.full((n_slots,), T, jnp.int32).at[slot].set(t_s)
    slot_w = jnp.zeros((n_slots,), f32).at[slot].set(w_s)
    blk_expert = jnp.minimum(jnp.searchsorted(pad_end, jnp.arange(n_blocks, dtype=jnp.int32) * MOE_BLOCK, side='right'), N_EXPERTS - 1)
    x_slots = jnp.concatenate([xt, jnp.zeros((1, D), xt.dtype)])[slot_tok].reshape(n_blocks, MOE_BLOCK, D)

    def expert_block(args):
        xb, e = args
        h = jax.nn.silu(xb @ w_gate[e]) * (xb @ w_up[e])
        return h @ w_down[e]

    y = lax.map(expert_block, (x_slots, blk_expert)).reshape(n_slots, D)
    out = jax.ops.segment_sum(y * slot_w[:, None].astype(y.dtype), slot_tok, num_segments=T + 1)[:T]
    return out.reshape(B, S, D)


def kernel(x, w_in_first, w_in_deep, rw_mu_first, rw_mu_deep, rw_w0, rw_w_up, rw_a0, rw_a_up, rw_v0, rw_v_up, rw_g_up, rw_k_k, rw_k_a, rw_r_k, rw_lnx_g, rw_lnx_b, nsa_cmp_pe, nsa_cmp_w1, nsa_cmp_b1, nsa_cmp_w2, w_br_rw, w_br_nsa, w_br_ret, w_out, ln1_g, ln1_b, moe_w_grp, moe_b_grp, moe_w_exp, moe_b_exp, moe_w_gate, moe_w_up, moe_w_down, ln2_g, ln2_b):
    B, S, D = x.shape
    v_first = None
    for l in range(DEPTH):
        first = l == 0
        w_in = w_in_first if first else w_in_deep[l - 1]
        mu = rw_mu_first if first else rw_mu_deep[l - 1]
        rw_cols = RW_COLS_FIRST if first else RW_COLS_DEEP
        rw_sizes = RW_BASE_SIZES if first else RW_BASE_SIZES + (RW_V_LORA,)
        rw_padded = RW_PAD_SIZES if first else RW_PAD_SIZES + (LORA_PAD,)
        x2 = x.reshape(S, D)
        p_rw = _matmul(x2, _pad_cols(w_in[:, :rw_cols], rw_sizes, rw_padded))
        w_nsa = jnp.pad(w_in[:, rw_cols:rw_cols + NSA_COLS], ((0, 0), (0, NSA_PCOLS - NSA_COLS)))
        p_nsa = _matmul(x2, w_nsa, tn=NSA_PCOLS // 3)
        proj = _matmul(x2, w_in[:, rw_cols + NSA_COLS:]).reshape(B, S, -1)
        p_ret, g_rw, g_nsa, g_ret = _split(proj, (RET_COLS, D_MODEL, D_MODEL, D_MODEL))
        y_rw, v_first = _rwkv7_time_mix(p_rw, _pad_cols(mu, rw_sizes, rw_padded), rw_w0[l], rw_w_up[l], rw_a0[l],
                                        rw_a_up[l], rw_g_up[l], rw_k_k[l], rw_k_a[l], rw_r_k[l], rw_lnx_g[l],
                                        rw_lnx_b[l], v_first,
                                        None if first else rw_v0[l - 1], None if first else rw_v_up[l - 1])
        y_rw = y_rw.reshape(B, S, RW_WIDTH)
        y_nsa = _nsa(p_nsa, nsa_cmp_pe[l], nsa_cmp_w1[l], nsa_cmp_b1[l], nsa_cmp_w2[l]).reshape(B, S, NSA_WIDTH)
        y_ret = _retention(p_ret)
        mm = lambda a, w: _matmul(a.reshape(S, -1), w).reshape(B, S, -1)
        merged = (jax.nn.sigmoid(g_rw) * mm(y_rw, w_br_rw[l])
                  + jax.nn.sigmoid(g_nsa) * mm(y_nsa, w_br_nsa[l])
                  + jax.nn.sigmoid(g_ret) * mm(y_ret, w_br_ret[l]))
        x = _layer_norm(DN_ALPHA * x + mm(merged, w_out[l]), ln1_g[l], ln1_b[l])
        moe = _hier_moe(x, moe_w_grp[l], moe_b_grp[l], moe_w_exp[l], moe_b_exp[l], moe_w_gate[l], moe_w_up[l], moe_w_down[l])
        x = _layer_norm(DN_ALPHA * x + moe, ln2_g[l], ln2_b[l])
    return x
```

```python
import functools

import jax
import jax.numpy as jnp
from jax import lax
import numpy as np
from jax.experimental import pallas as pl
from jax.experimental.pallas import tpu as pltpu

D_MODEL = 2048
DEPTH = 2

RW_HEADS = 16
RW_HEAD = 64
RW_WIDTH = RW_HEADS * RW_HEAD
RW_DECAY_LORA = 96
RW_A_LORA = 96
RW_V_LORA = 64
RW_G_LORA = 256
RW_GN_EPS = 64e-5
RW_BASE_SIZES = (RW_WIDTH, RW_WIDTH, RW_WIDTH, RW_DECAY_LORA, RW_A_LORA, RW_G_LORA)
RW_COLS_FIRST = sum(RW_BASE_SIZES)
RW_COLS_DEEP = RW_COLS_FIRST + RW_V_LORA

NSA_HEADS = 16
NSA_KV_GROUPS = 4
NSA_HPG = NSA_HEADS // NSA_KV_GROUPS
NSA_HEAD = 64
NSA_WIDTH = NSA_HEADS * NSA_HEAD
NSA_KV = NSA_KV_GROUPS * NSA_HEAD
CMP_LEN = 32
CMP_STRIDE = 16
CMP_HIDDEN = 128
SEL_BLOCK = 64
SEL_TOPN = 16
WINDOW = 512
Q_BLOCK = 128
FORCE_BONUS = 1e4
NEG_INF = -1e30
NSA_SIZES = (NSA_WIDTH,) + (NSA_KV,) * 6 + (3 * NSA_HEADS,)
NSA_COLS = sum(NSA_SIZES)

RET_HEADS = 8
RET_HEAD = 128
RET_WIDTH = RET_HEADS * RET_HEAD
RET_CHUNK = 128
RET_THETA = 10000.0
RET_GN_EPS = 1e-5
RET_SIZES = (RET_WIDTH,) * 4
RET_COLS = sum(RET_SIZES)

N_GROUPS = 4
EXPERTS_PER_GROUP = 8
N_EXPERTS = N_GROUPS * EXPERTS_PER_GROUP
EXPERT_FF = 512
EXPERT_TOPK = 2
MOE_BLOCK = 128

DN_ALPHA = (2 * DEPTH) ** 0.25
LN_EPS = 1e-5

LANES = 128


MM_ROWS = 1024


def _matmul_body(a_ref, b_ref, o_ref):
    o_ref[...] = jnp.dot(a_ref[...], b_ref[...], preferred_element_type=jnp.float32).astype(o_ref.dtype)


def _matmul(a, b, tn, out_dtype=jnp.float32):
    m, k = a.shape
    n = b.shape[1]
    tm = min(MM_ROWS, m)
    return pl.pallas_call(
        _matmul_body,
        grid=(m // tm, n // tn),
        in_specs=[pl.BlockSpec((tm, k), lambda i, j: (i, 0)),
                  pl.BlockSpec((k, tn), lambda i, j: (0, j))],
        out_specs=pl.BlockSpec((tm, tn), lambda i, j: (i, j)),
        out_shape=jax.ShapeDtypeStruct((m, n), out_dtype),
        compiler_params=pltpu.CompilerParams(
            dimension_semantics=("parallel", "parallel"),
            vmem_limit_bytes=48 * 1024 * 1024),
        name="dense_matmul",
    )(a, b)


MERGE_ROWS = 512
MERGE_COLS = 512


def _merge_body(y0_ref, y1_ref, y2_ref, g0_ref, g1_ref, g2_ref, w_ref, o_ref):
    acc = None
    for b, (y_ref, g_ref) in enumerate(((y0_ref, g0_ref), (y1_ref, g1_ref), (y2_ref, g2_ref))):
        t = _sigmoid(g_ref[...].astype(jnp.float32)) * _dot(y_ref[...], w_ref[b])
        acc = t if acc is None else acc + t
    o_ref[...] = acc.astype(o_ref.dtype)


def _merge_branches(ys, gates, w_br):
    S, wb = ys[0].shape
    D = w_br.shape[2]
    tm, tn = min(MERGE_ROWS, S), MERGE_COLS
    nj = D // tn
    y_spec = pl.BlockSpec((tm, wb), lambda i, j: (i, 0))
    g_spec = lambda b: pl.BlockSpec((tm, tn), lambda i, j: (i, b * nj + j))
    return pl.pallas_call(
        _merge_body,
        grid=(S // tm, nj),
        in_specs=[y_spec, y_spec, y_spec, g_spec(0), g_spec(1), g_spec(2),
                  pl.BlockSpec((3, wb, tn), lambda i, j: (0, 0, j))],
        out_specs=pl.BlockSpec((tm, tn), lambda i, j: (i, j)),
        out_shape=jax.ShapeDtypeStruct((S, D), jnp.bfloat16),
        compiler_params=pltpu.CompilerParams(dimension_semantics=("parallel", "parallel"),
                                             vmem_limit_bytes=48 * 1024 * 1024),
        name="merge_branches",
    )(*ys, gates, gates, gates, w_br)


LN_ROWS = 256
ROUTER_PAD = LANES


def _ln_rows(z, g, b):
    mu = jnp.mean(z, axis=-1, keepdims=True)
    zc = z - mu
    var = jnp.mean(zc * zc, axis=-1, keepdims=True)
    return zc * lax.rsqrt(var + LN_EPS) * g + b


def _out_ln_body(m_ref, w_ref, x_ref, g_ref, b_ref, wr_ref, br_ref, x1_ref, x1b_ref, lg_ref):
    z = DN_ALPHA * x_ref[...] + jnp.dot(m_ref[...], w_ref[...], preferred_element_type=jnp.float32)
    x1 = _ln_rows(z, g_ref[...], b_ref[...])
    x1_ref[...] = x1
    x1b = _bf(x1)
    x1b_ref[...] = x1b
    lg_ref[...] = jnp.dot(x1b, wr_ref[...], preferred_element_type=jnp.float32) + br_ref[...]


def _out_ln_router(merged, w_out, x, ln_g, ln_b, w_router, b_router):
    S, D = x.shape
    tm = min(LN_ROWS, S)
    row = pl.BlockSpec((tm, D), lambda i: (i, 0))
    full = lambda shape: pl.BlockSpec(shape, lambda i: (0, 0))
    return pl.pallas_call(
        _out_ln_body,
        grid=(S // tm,),
        in_specs=[row, full((D, D)), row, full((1, D)), full((1, D)), full((D, ROUTER_PAD)), full((1, ROUTER_PAD))],
        out_specs=(row, row, pl.BlockSpec((tm, ROUTER_PAD), lambda i: (i, 0))),
        out_shape=(jax.ShapeDtypeStruct((S, D), jnp.float32), jax.ShapeDtypeStruct((S, D), jnp.bfloat16),
                   jax.ShapeDtypeStruct((S, ROUTER_PAD), jnp.float32)),
        compiler_params=pltpu.CompilerParams(dimension_semantics=("parallel",),
                                             vmem_limit_bytes=48 * 1024 * 1024),
        name="out_proj_ln_router",
    )(merged, w_out, x, ln_g.reshape(1, D), ln_b.reshape(1, D), w_router, b_router)


def _residual_ln_body(x_ref, m_ref, g_ref, b_ref, o_ref, ob_ref):
    y = _ln_rows(DN_ALPHA * x_ref[...] + m_ref[...], g_ref[...], b_ref[...])
    o_ref[...] = y
    ob_ref[...] = _bf(y)


def _residual_ln(x, m, ln_g, ln_b):
    S, D = x.shape
    tm = min(LN_ROWS, S)
    row = pl.BlockSpec((tm, D), lambda i: (i, 0))
    full = pl.BlockSpec((1, D), lambda i: (0, 0))
    return pl.pallas_call(
        _residual_ln_body,
        grid=(S // tm,),
        in_specs=[row, row, full, full],
        out_specs=(row, row),
        out_shape=(jax.ShapeDtypeStruct((S, D), jnp.float32), jax.ShapeDtypeStruct((S, D), jnp.bfloat16)),
        compiler_params=pltpu.CompilerParams(dimension_semantics=("parallel",)),
        name="residual_ln",
    )(x, m, ln_g.reshape(1, D), ln_b.reshape(1, D))


def _split(p, sizes):
    idx = [int(i) for i in np.cumsum(sizes)[:-1]]
    return jnp.split(p, idx, axis=-1)


def _layer_norm(x, g, b):
    xf = x.astype(jnp.float32)
    mu = jnp.mean(xf, -1, keepdims=True)
    var = jnp.mean(jnp.square(xf - mu), -1, keepdims=True)
    return ((xf - mu) * lax.rsqrt(var + LN_EPS)).astype(x.dtype) * g + b


def _head_norm(x, eps):
    x = x.astype(jnp.float32)
    mu = jnp.mean(x, -1, keepdims=True)
    var = jnp.mean(jnp.square(x - mu), -1, keepdims=True)
    return (x - mu) * lax.rsqrt(var + eps)


def _token_shift(z, mu):
    prev = jnp.pad(z, ((0, 0), (1, 0), (0, 0)))[:, :-1]
    return z + (prev - z) * mu


def _masked_softmax(s, mask):
    s = jnp.where(mask, s, NEG_INF)
    p = jax.nn.softmax(s, axis=-1)
    return jnp.where(mask, p, 0.0)


RW_CHUNK = 64
LORA_PAD = LANES
RW_OFF_XW = 3 * RW_WIDTH
RW_OFF_XA = RW_OFF_XW + LORA_PAD
RW_OFF_XG = RW_OFF_XA + LORA_PAD
RW_OFF_XV = RW_OFF_XG + RW_G_LORA
RW_PAD_SIZES = (RW_WIDTH, RW_WIDTH, RW_WIDTH, LORA_PAD, LORA_PAD, RW_G_LORA)

_NT = (((1,), (1,)), ((), ()))
_TN = (((0,), (0,)), ((), ()))


def _bf(x):
    return x.astype(jnp.bfloat16)


def _dot(a, b, dims=None):
    if dims is None:
        return jnp.dot(_bf(a), _bf(b), preferred_element_type=jnp.float32)
    return lax.dot_general(_bf(a), _bf(b), dims, preferred_element_type=jnp.float32)


def _split3(x):
    h = _bf(x)
    r = x - h.astype(jnp.float32)
    m = _bf(r)
    l = _bf(r - m.astype(jnp.float32))
    return h, m, l


def _dot_exact_rhs(a, b_bf16, passes):
    acc = None
    for p in _split3(a)[:passes]:
        t = jnp.dot(p, b_bf16, preferred_element_type=jnp.float32)
        acc = t if acc is None else acc + t
    return acc


def _dot_exact_lhs(a_bf16, b, passes):
    acc = None
    for p in _split3(b)[:passes]:
        t = jnp.dot(a_bf16, p, preferred_element_type=jnp.float32)
        acc = t if acc is None else acc + t
    return acc


def _sigmoid(x):
    return 1.0 / (1.0 + jnp.exp(-x))


def _softplus(x):
    return jnp.maximum(x, 0.0) + jnp.log(1.0 + jnp.exp(-jnp.abs(x)))


def _rwkv_body(first, *refs):
    if first:
        (p_ref, mu_ref, w0_ref, wup_ref, a0_ref, aup_ref, gup_ref, kk_ref, ka_ref, rk_ref,
         lg_ref, lb_ref, ones_ref, y_ref, vf_out_ref, st_ref, prev_ref) = refs
    else:
        (p_ref, mu_ref, w0_ref, wup_ref, a0_ref, aup_ref, gup_ref, kk_ref, ka_ref, rk_ref,
         lg_ref, lb_ref, ones_ref, v0_ref, vup_ref, vf_in_ref, y_ref, st_ref, prev_ref) = refs
    C, H, N, W = RW_CHUNK, RW_HEADS, RW_HEAD, RW_WIDTH
    f32 = jnp.float32
    i = pl.program_id(0)

    @pl.when(i == 0)
    def _():
        st_ref[...] = jnp.zeros_like(st_ref)
        prev_ref[...] = jnp.zeros_like(prev_ref)

    z = p_ref[...]
    row = lax.broadcasted_iota(jnp.int32, z.shape, 0)
    prev = jnp.where(row == 0, prev_ref[...], pltpu.roll(z, 1, axis=0))
    prev_ref[...] = z[C - 1:C, :]
    xs = z + (prev - z) * mu_ref[...]

    r = xs[:, 0:W]
    k = xs[:, W:2 * W]
    v = xs[:, 2 * W:3 * W]
    xw = xs[:, RW_OFF_XW:RW_OFF_XW + LORA_PAD]
    xa = xs[:, RW_OFF_XA:RW_OFF_XA + LORA_PAD]
    xg = xs[:, RW_OFF_XG:RW_OFF_XG + RW_G_LORA]
    if first:
        vf_out_ref[...] = v
    else:
        xv = xs[:, RW_OFF_XV:RW_OFF_XV + LORA_PAD]
        v = v + (vf_in_ref[...] - v) * _sigmoid(v0_ref[...] + _dot(xv, vup_ref[...]))

    w_log = -_softplus(-(w0_ref[...] + _dot(jnp.tanh(xw), wup_ref[...]))) - 0.5
    logd = -jnp.exp(w_log)
    a = _sigmoid(a0_ref[...] + _dot(xa, aup_ref[...]))
    g = _dot(_sigmoid(xg), gup_ref[...])
    ones_blk = ones_ref[...]
    kk = k * kk_ref[...]
    kk_n2 = _dot_exact_rhs(kk * kk, ones_blk, 2)
    kk = kk / jnp.maximum(jnp.sqrt(kk_n2), 1e-12)
    k = k * (1.0 + (a - 1.0) * ka_ref[...])
    b = kk * a
    bonus = _dot_exact_rhs(r * k * rk_ref[...], ones_blk, 2) * v

    tr = lax.broadcasted_iota(jnp.int32, (C, C), 0)
    tc = lax.broadcasted_iota(jnp.int32, (C, C), 1)
    low_incl = tr >= tc
    low_strict = tr > tc
    cum = _dot_exact_lhs(low_incl.astype(jnp.bfloat16), logd, 3)
    total = cum[C - 1:C, :]
    e_in = jnp.exp(cum)
    e_ex = jnp.exp(cum - logd)
    e_neg = jnp.exp(-cum)
    e_rem = jnp.exp(total - cum)
    kkd = kk * e_ex
    kh = k * e_neg
    bh = b * e_neg
    rd = r * e_in
    khg = k * e_rem
    bhg = b * e_rem
    gam = jnp.exp(total)

    eye = (tr == tc).astype(f32)
    heads = range(H)
    hs = lambda t: [t[:, h * N:(h + 1) * N] for h in heads]
    kkd_h, kh_h, bh_h, rd_h, v_h, khg_h, bhg_h = (hs(t) for t in (kkd, kh, bh, rd, v, khg, bhg))
    st = [st_ref[h] for h in heads]
    gm = [_dot(jnp.concatenate([kkd_h[h], rd_h[h]], axis=0),
               jnp.concatenate([kh_h[h], bh_h[h]], axis=0), _NT) for h in heads]
    a_k = [jnp.where(low_strict, gm[h][0:C, 0:C], 0.0) for h in heads]
    a_b = [jnp.where(low_strict, gm[h][0:C, C:2 * C], 0.0) for h in heads]
    b_k = [jnp.where(low_incl, gm[h][C:2 * C, 0:C], 0.0) for h in heads]
    b_b = [jnp.where(low_incl, gm[h][C:2 * C, C:2 * C], 0.0) for h in heads]
    rhs = [_dot(kkd_h[h], st[h]) + _dot(a_k[h], v_h[h]) for h in heads]
    y0 = [_dot(rd_h[h], st[h]) + _dot(b_k[h], v_h[h]) for h in heads]
    t_inv = [eye - jnp.where((tr == tc + 1) & (tr % 2 == 1), a_b[h], 0.0) for h in heads]
    m = 2
    while m < C:
        rb, cb = tr // m, tc // m
        sib = (rb == cb + 1) & (rb % 2 == 1)
        tmp = [_dot(jnp.where(sib, a_b[h], 0.0), t_inv[h]) for h in heads]
        t_inv = [t_inv[h] - _dot(t_inv[h], tmp[h]) for h in heads]
        m *= 2
    zz = [_dot(t_inv[h], rhs[h]) for h in heads]
    ys = [y0[h] - _dot(b_b[h], zz[h]) for h in heads]
    upd = [_dot(khg_h[h], v_h[h], _TN) - _dot(bhg_h[h], zz[h], _TN) for h in heads]
    for h in heads:
        st_ref[h] = st[h] * jnp.transpose(gam[:, h * N:(h + 1) * N]) + upd[h]
    y = jnp.concatenate(ys, axis=1)

    inv_n = 1.0 / N
    mean = _dot_exact_rhs(y, ones_blk, 2) * inv_n
    yc = y - mean
    var = _dot_exact_rhs(yc * yc, ones_blk, 2) * inv_n
    yn = yc * lax.rsqrt(var + RW_GN_EPS) * lg_ref[...] + lb_ref[...]
    y_ref[...] = (yn + bonus) * g


def _pad_rows(w, rows):
    return jnp.pad(w, ((0, rows - w.shape[0]), (0, 0)))


def _pad_cols(p, sizes, padded):
    parts = _split(p, sizes)
    return jnp.concatenate([jnp.pad(t, [(0, 0)] * (t.ndim - 1) + [(0, n - t.shape[-1])])
                            for t, n in zip(parts, padded)], -1)


def _rwkv7_time_mix(p, mu, w0, w_up, a0, a_up, g_up, k_k, k_a, r_k, lnx_g, lnx_b, v_first, v0, v_up):
    S, cols = p.shape
    first = v_first is None
    C, W = RW_CHUNK, RW_WIDTH
    row = lambda t: t.reshape(1, -1)
    hid = np.arange(W) // RW_HEAD
    ones_blk = jnp.asarray(hid[:, None] == hid[None, :], jnp.bfloat16)
    full = lambda shape: pl.BlockSpec(shape, lambda i: (0,) * len(shape))
    tok = lambda width: pl.BlockSpec((C, width), lambda i: (i, 0))
    args = [p, row(mu), row(w0), _pad_rows(w_up, LORA_PAD), row(a0), _pad_rows(a_up, LORA_PAD), g_up,
            row(k_k), row(k_a), row(r_k), row(lnx_g), row(lnx_b), ones_blk]
    specs = [tok(cols), full((1, cols)), full((1, W)), full((LORA_PAD, W)), full((1, W)), full((LORA_PAD, W)),
             full((RW_G_LORA, W)), full((1, W)), full((1, W)), full((1, W)), full((1, W)), full((1, W)),
             full((W, W))]
    if first:
        out_shape = (jax.ShapeDtypeStruct((S, W), jnp.float32), jax.ShapeDtypeStruct((S, W), jnp.float32))
        out_specs = (tok(W), tok(W))
    else:
        args += [row(v0), _pad_rows(v_up, LORA_PAD), v_first]
        specs += [full((1, W)), full((LORA_PAD, W)), tok(W)]
        out_shape = jax.ShapeDtypeStruct((S, W), jnp.float32)
        out_specs = tok(W)
    res = pl.pallas_call(
        functools.partial(_rwkv_body, first),
        grid=(S // C,),
        in_specs=specs,
        out_specs=out_specs,
        out_shape=out_shape,
        scratch_shapes=[pltpu.VMEM((RW_HEADS, RW_HEAD, RW_HEAD), jnp.float32),
                        pltpu.VMEM((1, cols), jnp.float32)],
        compiler_params=pltpu.CompilerParams(dimension_semantics=("arbitrary",),
                                             vmem_limit_bytes=48 * 1024 * 1024),
        name="rwkv7_chunked",
    )(*args)
    if first:
        return res[0], res[1]
    return res, v_first


CMP_ROWS = 256


def _compress_body(r_ref, nx_ref, pe_ref, wa_ref, wb_ref, b1_ref, w2_ref, o_ref):
    r = r_ref[...]
    rb, half = r.shape
    row = lax.broadcasted_iota(jnp.int32, r.shape, 0)
    nxt = jnp.where(row == rb - 1, nx_ref[0:1, :], pltpu.roll(r, rb - 1, axis=0))
    pe = pe_ref[...]
    h = (_dot(r + pe[:, :half], wa_ref[...]) + _dot(nxt + pe[:, half:], wb_ref[...]) + b1_ref[...])
    h = 0.5 * h * (1.0 + jnp.tanh(0.7978845608028654 * (h + 0.044715 * (h * h * h))))
    o_ref[...] = _dot(h, w2_ref[...])


def _nsa_compress(t, pe, w1, b1, w2):
    S = t.shape[0]
    G, DH, L, ST = NSA_KV_GROUPS, NSA_HEAD, CMP_LEN, CMP_STRIDE
    nr = S // ST
    rb = min(CMP_ROWS, nr)
    cols = ST * G * DH
    r = t.reshape(nr, cols)
    eye_g = jnp.eye(G, dtype=w1.dtype)
    big = jnp.einsum('ldh,gk->lgdkh', w1.reshape(L, DH, CMP_HIDDEN), eye_g).reshape(L * G * DH, G * CMP_HIDDEN)
    wa, wb = big[:cols], big[cols:]
    pe_big = jnp.broadcast_to(pe[:, None, :], (L, G, DH)).reshape(1, L * G * DH)
    b1_big = jnp.tile(b1, G).reshape(1, G * CMP_HIDDEN)
    w2_big = jnp.einsum('hd,gk->ghkd', w2, eye_g).reshape(G * CMP_HIDDEN, G * DH)
    full = lambda shape: pl.BlockSpec(shape, lambda i: (0, 0))
    return pl.pallas_call(
        _compress_body,
        grid=(nr // rb,),
        in_specs=[pl.BlockSpec((rb, cols), lambda i: (i, 0)),
                  pl.BlockSpec((8, cols), lambda i: (jnp.minimum((i + 1) * (rb // 8), nr // 8 - 1), 0)),
                  full((1, 2 * cols)), full((cols, G * CMP_HIDDEN)), full((cols, G * CMP_HIDDEN)),
                  full((1, G * CMP_HIDDEN)), full((G * CMP_HIDDEN, G * DH))],
        out_specs=pl.BlockSpec((rb, G * DH), lambda i: (i, 0)),
        out_shape=jax.ShapeDtypeStruct((nr, G * DH), jnp.float32),
        compiler_params=pltpu.CompilerParams(dimension_semantics=("parallel",),
                                             vmem_limit_bytes=48 * 1024 * 1024),
        name="nsa_compress",
    )(r, r, pe_big, _bf(wa), _bf(wb), b1_big, _bf(w2_big))


NSA_QB = 128
NSA_KC = 512
NSA_WC = 128
NSA_GATE_PAD = LANES
NSA_PCOLS = NSA_WIDTH + 6 * NSA_KV + NSA_GATE_PAD
NSA_GATE_ROWS = 16


def _softmax_cols(s, mask):
    s = jnp.where(mask, s, NEG_INF)
    m = jnp.max(s, axis=0, keepdims=True)
    p = jnp.where(mask, jnp.exp(s - m), 0.0)
    l = jnp.sum(p, axis=0, keepdims=True)
    return p * (1.0 / jnp.where(l > 0.0, l, 1.0))


def _nsa_body(*refs):
    n_win = (WINDOW + NSA_QB) // NSA_WC
    qt_ref, kvc_ref, kvct_ref, kvs_ref, kvst_ref = refs[:5]
    kvw_refs = refs[5:5 + n_win]
    kvwt_refs = refs[5 + n_win:5 + 2 * n_win]
    ovl_ref, gate_ref, o_ref, bias_ref, acc_ref = refs[5 + 2 * n_win:]
    QB, HPG, DH = NSA_QB, NSA_HPG, NSA_HEAD
    f32 = jnp.float32
    qi = pl.program_id(1)
    q0 = qi * QB
    qt = qt_ref[...]
    ncmp = kvc_ref.shape[0]
    t_row = q0 + lax.broadcasted_iota(jnp.int32, (1, QB), 1)
    head = lambda a, h: a[:, h * QB:(h + 1) * QB]

    s = jnp.dot(kvc_ref[...], qt, preferred_element_type=f32)
    c_end = lax.broadcasted_iota(jnp.int32, (ncmp, QB), 0) * CMP_STRIDE + (CMP_LEN - 1)
    m_cmp = c_end <= t_row
    kvct = kvct_ref[...]
    p_sum = None
    o_cmp = []
    for h in range(HPG):
        p = _softmax_cols(head(s, h), m_cmp)
        o_cmp.append(jnp.dot(kvct, _bf(p), preferred_element_type=f32))
        p_sum = p if p_sum is None else p_sum + p
    imp = _dot_exact_lhs(ovl_ref[...], p_sum, 2)
    nsel = imp.shape[0]

    n_row = lax.broadcasted_iota(jnp.int32, (nsel, QB), 0)
    cur = t_row // SEL_BLOCK
    forced = (n_row == 0) | (n_row == cur) | (n_row == cur - 1)
    score = jnp.where(n_row <= cur, imp + jnp.where(forced, FORCE_BONUS, 0.0), NEG_INF)
    bias = jnp.full((nsel, QB), NEG_INF, f32)
    for _ in range(min(SEL_TOPN, nsel)):
        best = jnp.max(score, axis=0, keepdims=True)
        first = jnp.min(jnp.where(score == best, n_row, nsel), axis=0, keepdims=True)
        hit = n_row == first
        bias = jnp.where(hit & (best > 0.5 * NEG_INF), 0.0, bias)
        score = jnp.where(hit, -jnp.inf, score)
    bias_ref[...] = bias

    KC = NSA_KC
    n_sub = KC // SEL_BLOCK
    key_row = lax.broadcasted_iota(jnp.int32, (KC, QB), 0)
    acc_ref[...] = jnp.zeros_like(acc_ref)

    def sel_step(j, carry):
        ms, ls = carry
        k0 = pl.multiple_of(j * KC, KC)
        kv = kvs_ref[pl.ds(k0, KC), :]
        kvt = kvst_ref[:, pl.ds(k0, KC)]
        rows = [jnp.broadcast_to(bias_ref[pl.ds(j * n_sub + b, 1), :], (SEL_BLOCK, QB)) for b in range(n_sub)]
        madd = jnp.where(k0 + key_row <= t_row, jnp.concatenate(rows, axis=0), NEG_INF)
        sc = jnp.dot(kv, qt, preferred_element_type=f32)
        hh = range(HPG)
        sh = [head(sc, h) + madd for h in hh]
        new_m = [jnp.maximum(ms[h], jnp.max(sh[h], axis=0, keepdims=True)) for h in hh]
        alpha = [jnp.exp(ms[h] - new_m[h]) for h in hh]
        p = [jnp.exp(sh[h] - new_m[h]) for h in hh]
        new_l = [alpha[h] * ls[h] + jnp.sum(p[h], axis=0, keepdims=True) for h in hh]
        pv = [jnp.dot(kvt, _bf(p[h]), preferred_element_type=f32) for h in hh]
        for h in hh:
            acc_ref[h] = alpha[h] * acc_ref[h] + pv[h]
        return tuple(new_m), tuple(new_l)

    init = (tuple(jnp.full((1, QB), NEG_INF, f32) for _ in range(HPG)),
            tuple(jnp.zeros((1, QB), f32) for _ in range(HPG)))
    n_chunks = (q0 + QB + KC - 1) // KC
    _, l_sel = lax.fori_loop(0, n_chunks, sel_step, init)

    WC = NSA_WC
    s_win, m_win = [], []
    for c in range(n_win):
        s_pos = q0 - WINDOW + c * WC + lax.broadcasted_iota(jnp.int32, (WC, QB), 0)
        dist = t_row - s_pos
        m_win.append((dist >= 0) & (dist < WINDOW) & (s_pos >= 0))
        s_win.append(jnp.dot(kvw_refs[c][...], qt, preferred_element_type=f32))
    m_all = jnp.concatenate(m_win, axis=0)
    s_all = jnp.concatenate(s_win, axis=0)

    gate = _sigmoid(gate_ref[...])
    outs = []
    for h in range(HPG):
        p = _bf(_softmax_cols(head(s_all, h), m_all))
        o_win = None
        for c in range(n_win):
            t = jnp.dot(kvwt_refs[c][...], p[c * WC:(c + 1) * WC], preferred_element_type=f32)
            o_win = t if o_win is None else o_win + t
        o_sel = acc_ref[h] * (1.0 / l_sel[h])
        gr = lambda b: gate[h * 3 + b:h * 3 + b + 1, :]
        o = gr(0) * o_cmp[h] + gr(1) * o_sel + gr(2) * o_win
        outs.append(o[DH:, :])
    o_ref[...] = jnp.transpose(jnp.concatenate(outs, axis=0))


def _nsa(p, cmp_pe, cmp_w1, cmp_b1, cmp_w2):
    S = p.shape[0]
    G, HPG, DH, QB, WC = NSA_KV_GROUPS, NSA_HPG, NSA_HEAD, NSA_QB, NSA_WC
    W, KV = NSA_WIDTH, NSA_KV
    nqb = S // QB
    n_win = (WINDOW + QB) // WC
    q = p[:, :W]
    kc, vc, ks, vs, kw, vw = (p[:, W + i * KV:W + (i + 1) * KV] for i in range(6))
    gate = p[:, W + 6 * KV:W + 6 * KV + 3 * NSA_HEADS]
    k_cmp = _nsa_compress(kc, cmp_pe[0], cmp_w1[0], cmp_b1[0], cmp_w2[0])
    v_cmp = _nsa_compress(vc, cmp_pe[1], cmp_w1[1], cmp_b1[1], cmp_w2[1])
    pack = lambda k, v: _bf(jnp.concatenate([k.reshape(-1, G, DH), v.reshape(-1, G, DH)], -1)).transpose(1, 0, 2)
    kvc, kvs, kvw = pack(k_cmp, v_cmp), pack(ks, vs), pack(kw, vw)
    tr = lambda a: a.transpose(0, 2, 1)
    qh = q.reshape(nqb, QB, G, HPG, DH) * (DH ** -0.5)
    qh = _bf(jnp.concatenate([qh, jnp.zeros_like(qh)], -1)).transpose(2, 0, 4, 3, 1).reshape(G, nqb, LANES, HPG * QB)
    gate_t = jnp.pad(gate.reshape(S, G, 3 * HPG).transpose(1, 2, 0), ((0, 0), (0, NSA_GATE_ROWS - 3 * HPG), (0, 0)))
    ncmp, nsel = S // CMP_STRIDE, S // SEL_BLOCK
    c_start = np.arange(ncmp)[None, :] * CMP_STRIDE
    s_start = np.arange(nsel)[:, None] * SEL_BLOCK
    overlap_t = jnp.asarray((c_start < s_start + SEL_BLOCK) & (c_start + CMP_LEN > s_start), jnp.bfloat16)
    win_blk = lambda c: (lambda g, i: (g, jnp.maximum(i - WINDOW // WC + c, 0), 0))
    win_blk_t = lambda c: (lambda g, i: (g, 0, jnp.maximum(i - WINDOW // WC + c, 0)))
    in_specs = ([pl.BlockSpec((None, None, LANES, HPG * QB), lambda g, i: (g, i, 0, 0)),
                 pl.BlockSpec((None, ncmp, LANES), lambda g, i: (g, 0, 0)),
                 pl.BlockSpec((None, LANES, ncmp), lambda g, i: (g, 0, 0)),
                 pl.BlockSpec((None, S, LANES), lambda g, i: (g, 0, 0)),
                 pl.BlockSpec((None, LANES, S), lambda g, i: (g, 0, 0))]
                + [pl.BlockSpec((None, WC, LANES), win_blk(c)) for c in range(n_win)]
                + [pl.BlockSpec((None, LANES, WC), win_blk_t(c)) for c in range(n_win)]
                + [pl.BlockSpec((nsel, ncmp), lambda g, i: (0, 0)),
                   pl.BlockSpec((None, NSA_GATE_ROWS, QB), lambda g, i: (g, 0, i))])
    kvwt = tr(kvw)
    return pl.pallas_call(
        _nsa_body,
        grid=(G, nqb),
        in_specs=in_specs,
        out_specs=pl.BlockSpec((QB, HPG * DH), lambda g, i: (i, g)),
        out_shape=jax.ShapeDtypeStruct((S, W), jnp.float32),
        scratch_shapes=[pltpu.VMEM((nsel, QB), jnp.float32),
                        pltpu.VMEM((HPG, LANES, QB), jnp.float32)],
        compiler_params=pltpu.CompilerParams(dimension_semantics=("arbitrary", "arbitrary"),
                                             vmem_limit_bytes=56 * 1024 * 1024),
        name="nsa_attention",
    )(qh, kvc, tr(kvc), kvs, tr(kvs), *([kvw] * n_win), *([kvwt] * n_win), overlap_t, gate_t)


def _retention_body(p_ref, cc_ref, ss_ref, inner_ref, qd_ref, kd_ref, cd_ref, o_ref, r_ref):
    H, DK, W = RET_HEADS, RET_HEAD, RET_WIDTH
    f32 = jnp.float32

    @pl.when(pl.program_id(0) == 0)
    def _():
        r_ref[...] = jnp.zeros_like(r_ref)

    cc, ss = cc_ref[...], ss_ref[...]
    hh = range(H)
    col = lambda part, h: p_ref[:, part * W + h * DK:part * W + (h + 1) * DK].astype(f32)
    rot = lambda t: t * cc + pltpu.roll(t, DK // 2, axis=1) * ss
    q = [rot(col(0, h)) for h in hh]
    k = [rot(col(1, h)) * (DK ** -0.5) for h in hh]
    v = [col(2, h) for h in hh]
    r = [r_ref[h] for h in hh]
    att = [_dot(q[h], k[h], _NT) * inner_ref[h] for h in hh]
    o = [_dot(att[h], v[h]) + _dot(q[h], r[h]) * qd_ref[h] for h in hh]
    upd = [_dot(k[h] * kd_ref[h], v[h], _TN) for h in hh]
    for h in hh:
        r_ref[h] = r[h] * cd_ref[h] + upd[h]
        mu = jnp.mean(o[h], axis=-1, keepdims=True)
        oc = o[h] - mu
        var = jnp.mean(oc * oc, axis=-1, keepdims=True)
        g = col(3, h)
        o_ref[:, h * DK:(h + 1) * DK] = (g * _sigmoid(g)) * (oc * lax.rsqrt(var + RET_GN_EPS))


def _retention(p):
    S = p.shape[0]
    H, DK, C = RET_HEADS, RET_HEAD, RET_CHUNK
    f32 = jnp.float32
    inv_freq = 1.0 / (RET_THETA ** jnp.linspace(0.0, 1.0, DK // 2))
    ang = jnp.arange(S, dtype=f32)[:, None] * inv_freq[None, :]
    cos, sin = jnp.cos(ang), jnp.sin(ang)
    cc = jnp.concatenate([cos, cos], -1)
    ss = jnp.concatenate([-sin, sin], -1)
    log_g = jnp.log1p(-jnp.exp2(-5.0 - jnp.arange(H, dtype=f32)))
    i = jnp.arange(C, dtype=f32)
    diff = i[:, None] - i[None, :]
    inner_decay = jnp.where(diff >= 0, jnp.exp(log_g[:, None, None] * jnp.maximum(diff, 0.0)), 0.0)
    lanes = lambda t: jnp.broadcast_to(t, (H, C, DK))
    q_decay = lanes(jnp.exp(log_g[:, None] * (i + 1.0))[..., None])
    k_decay = lanes(jnp.exp(log_g[:, None] * (C - 1.0 - i))[..., None])
    c_decay = lanes(jnp.exp(log_g * C)[:, None, None])
    tok = lambda width: pl.BlockSpec((C, width), lambda n: (n, 0))
    const = pl.BlockSpec((H, C, DK), lambda n: (0, 0, 0))
    return pl.pallas_call(
        _retention_body,
        grid=(S // C,),
        in_specs=[tok(p.shape[1]), tok(DK), tok(DK), const, const, const, const],
        out_specs=tok(RET_WIDTH),
        out_shape=jax.ShapeDtypeStruct((S, RET_WIDTH), f32),
        scratch_shapes=[pltpu.VMEM((H, DK, DK), f32)],
        compiler_params=pltpu.CompilerParams(dimension_semantics=("arbitrary",),
                                             vmem_limit_bytes=48 * 1024 * 1024),
        name="retention_chunked",
    )(p, cc, ss, inner_decay, q_decay, k_decay, c_decay)


MOE_ROWS = 256


def _expert_body(blk_e_ref, n_used_ref, x_ref, wg_ref, wu_ref, wd_ref, o_ref):
    i = pl.program_id(0)

    @pl.when(i < n_used_ref[0])
    def _():
        x = x_ref[...]
        hg = _dot(x, wg_ref[...])
        h = (hg * _sigmoid(hg)) * _dot(x, wu_ref[...])
        o_ref[...] = _dot(h, wd_ref[...])

    @pl.when(i >= n_used_ref[0])
    def _():
        o_ref[...] = jnp.zeros_like(o_ref)


def _moe_experts(x_slots, blk_expert, n_used, w_gate, w_up, w_down):
    n_slots, D = x_slots.shape
    F = w_gate.shape[2]
    mb = MOE_ROWS
    grid_spec = pltpu.PrefetchScalarGridSpec(
        num_scalar_prefetch=2,
        grid=(n_slots // mb,),
        in_specs=[pl.BlockSpec((mb, D), lambda i, be, nu: (i, 0)),
                  pl.BlockSpec((None, D, F), lambda i, be, nu: (be[i], 0, 0)),
                  pl.BlockSpec((None, D, F), lambda i, be, nu: (be[i], 0, 0)),
                  pl.BlockSpec((None, F, D), lambda i, be, nu: (be[i], 0, 0))],
        out_specs=pl.BlockSpec((mb, D), lambda i, be, nu: (i, 0)))
    return pl.pallas_call(
        _expert_body,
        grid_spec=grid_spec,
        out_shape=jax.ShapeDtypeStruct((n_slots, D), jnp.float32),
        compiler_params=pltpu.CompilerParams(dimension_semantics=("arbitrary",),
                                             vmem_limit_bytes=56 * 1024 * 1024),
        name="moe_experts",
    )(blk_expert, n_used, x_slots, w_gate, w_up, w_down)


def _hier_moe(x1b, logits, w_gate, w_up, w_down):
    T, D = x1b.shape
    f32 = jnp.float32
    grp_logits = logits[:, :N_GROUPS]
    grp = jnp.argmax(grp_logits, axis=-1)
    grp_w = jnp.take_along_axis(jax.nn.softmax(grp_logits, -1), grp[:, None], axis=-1)
    exp_logits = logits[:, N_GROUPS:N_GROUPS + N_EXPERTS].reshape(T, N_GROUPS, EXPERTS_PER_GROUP)
    in_grp = jnp.take_along_axis(exp_logits, grp[:, None, None], axis=1)[:, 0]
    top_val, top_idx = lax.top_k(in_grp, EXPERT_TOPK)
    gate_w = (jax.nn.softmax(top_val, -1) * grp_w).reshape(-1)
    expert = (grp[:, None] * EXPERTS_PER_GROUP + top_idx).reshape(-1).astype(jnp.int32)
    token = jnp.repeat(jnp.arange(T, dtype=jnp.int32), EXPERT_TOPK)
    n_assign = T * EXPERT_TOPK
    mb = MOE_ROWS
    onehot = (expert[:, None] == jnp.arange(N_EXPERTS, dtype=jnp.int32)[None, :]).astype(jnp.int32)
    csum = jnp.cumsum(onehot, axis=0)
    pos = jnp.take_along_axis(csum, expert[:, None], axis=1)[:, 0] - 1
    counts = csum[-1]
    padded = (counts + mb - 1) // mb * mb
    pad_end = jnp.cumsum(padded)
    slot = (pad_end - padded)[expert] + pos
    n_blocks = -(-n_assign // mb) + N_EXPERTS
    n_slots = n_blocks * mb
    slot_tok = jnp.full((n_slots,), T, jnp.int32).at[slot].set(token)
    blk_expert = jnp.minimum(jnp.searchsorted(pad_end, jnp.arange(n_blocks, dtype=jnp.int32) * mb, side='right'),
                             N_EXPERTS - 1).astype(jnp.int32)
    n_used = (pad_end[-1:] // mb).astype(jnp.int32)
    x_slots = jnp.concatenate([x1b, jnp.zeros((1, D), x1b.dtype)])[slot_tok]
    y = _moe_experts(x_slots, blk_expert, n_used, w_gate, w_up, w_down)
    ys = y[slot].reshape(T, EXPERT_TOPK, D) * gate_w.reshape(T, EXPERT_TOPK, 1)
    return ys[:, 0] + ys[:, 1]


def kernel(x, w_in_first, w_in_deep, rw_mu_first, rw_mu_deep, rw_w0, rw_w_up, rw_a0, rw_a_up, rw_v0, rw_v_up, rw_g_up, rw_k_k, rw_k_a, rw_r_k, rw_lnx_g, rw_lnx_b, nsa_cmp_pe, nsa_cmp_w1, nsa_cmp_b1, nsa_cmp_w2, w_br_rw, w_br_nsa, w_br_ret, w_out, ln1_g, ln1_b, moe_w_grp, moe_b_grp, moe_w_exp, moe_b_exp, moe_w_gate, moe_w_up, moe_w_down, ln2_g, ln2_b):
    B, S, D = x.shape
    assert B == 1
    x = x.reshape(S, D)
    xb = _bf(x)
    v_first = None
    for l in range(DEPTH):
        first = l == 0
        w_in = w_in_first if first else w_in_deep[l - 1]
        mu = rw_mu_first if first else rw_mu_deep[l - 1]
        rw_cols = RW_COLS_FIRST if first else RW_COLS_DEEP
        rw_sizes = RW_BASE_SIZES if first else RW_BASE_SIZES + (RW_V_LORA,)
        rw_padded = RW_PAD_SIZES if first else RW_PAD_SIZES + (LORA_PAD,)
        rw_tn = 512 if first else 768
        tail = -sum(rw_padded) % rw_tn
        w_rw = _bf(jnp.pad(_pad_cols(w_in[:, :rw_cols], rw_sizes, rw_padded), ((0, 0), (0, tail))))
        mu_p = jnp.pad(_pad_cols(mu, rw_sizes, rw_padded), (0, tail))
        c0 = rw_cols
        w_nsa = _bf(jnp.pad(w_in[:, c0:c0 + NSA_COLS], ((0, 0), (0, NSA_PCOLS - NSA_COLS))))
        c0 += NSA_COLS
        w_ret = _bf(w_in[:, c0:c0 + RET_COLS])
        w_gates = _bf(w_in[:, c0 + RET_COLS:])
        p_rw = _matmul(xb, w_rw, tn=rw_tn)
        p_nsa = _matmul(xb, w_nsa, tn=NSA_PCOLS // 3)
        p_ret = _matmul(xb, w_ret, tn=512)
        gates = _matmul(xb, w_gates, tn=512, out_dtype=jnp.bfloat16)
        y_rw, v_first = _rwkv7_time_mix(p_rw, mu_p, rw_w0[l], rw_w_up[l], rw_a0[l],
                                        rw_a_up[l], rw_g_up[l], rw_k_k[l], rw_k_a[l], rw_r_k[l], rw_lnx_g[l],
                                        rw_lnx_b[l], v_first,
                                        None if first else rw_v0[l - 1], None if first else rw_v_up[l - 1])
        y_nsa = _nsa(p_nsa, nsa_cmp_pe[l], nsa_cmp_w1[l], nsa_cmp_b1[l], nsa_cmp_w2[l])
        y_ret = _retention(p_ret)
        w_br = _bf(jnp.stack([w_br_rw[l], w_br_nsa[l], w_br_ret[l]]))
        merged = _merge_branches((y_rw, y_nsa, y_ret), gates, w_br)
        w_router = jnp.pad(jnp.concatenate([moe_w_grp[l], moe_w_exp[l]], 1),
                           ((0, 0), (0, ROUTER_PAD - N_GROUPS - N_EXPERTS)))
        b_router = jnp.pad(jnp.concatenate([moe_b_grp[l], moe_b_exp[l]]),
                           (0, ROUTER_PAD - N_GROUPS - N_EXPERTS)).reshape(1, ROUTER_PAD)
        x1, x1b, logits = _out_ln_router(merged, _bf(w_out[l]), x, ln1_g[l], ln1_b[l], _bf(w_router), b_router)
        moe = _hier_moe(x1b, logits, moe_w_gate[l], moe_w_up[l], moe_w_down[l])
        x, xb = _residual_ln(x1, moe, ln2_g[l], ln2_b[l])
    return x.reshape(B, S, D)
```

```python
import functools

import jax
import jax.numpy as jnp
from jax import lax
import numpy as np
from jax.experimental import pallas as pl
from jax.experimental.pallas import tpu as pltpu

D_MODEL = 2048
DEPTH = 2

RW_HEADS = 16
RW_HEAD = 64
RW_WIDTH = RW_HEADS * RW_HEAD
RW_DECAY_LORA = 96
RW_A_LORA = 96
RW_V_LORA = 64
RW_G_LORA = 256
RW_GN_EPS = 64e-5
RW_BASE_SIZES = (RW_WIDTH, RW_WIDTH, RW_WIDTH, RW_DECAY_LORA, RW_A_LORA, RW_G_LORA)
RW_COLS_FIRST = sum(RW_BASE_SIZES)
RW_COLS_DEEP = RW_COLS_FIRST + RW_V_LORA

NSA_HEADS = 16
NSA_KV_GROUPS = 4
NSA_HPG = NSA_HEADS // NSA_KV_GROUPS
NSA_HEAD = 64
NSA_WIDTH = NSA_HEADS * NSA_HEAD
NSA_KV = NSA_KV_GROUPS * NSA_HEAD
CMP_LEN = 32
CMP_STRIDE = 16
CMP_HIDDEN = 128
SEL_BLOCK = 64
SEL_TOPN = 16
WINDOW = 512
Q_BLOCK = 128
FORCE_BONUS = 1e4
NEG_INF = -1e30
NSA_SIZES = (NSA_WIDTH,) + (NSA_KV,) * 6 + (3 * NSA_HEADS,)
NSA_COLS = sum(NSA_SIZES)

RET_HEADS = 8
RET_HEAD = 128
RET_WIDTH = RET_HEADS * RET_HEAD
RET_CHUNK = 128
RET_THETA = 10000.0
RET_GN_EPS = 1e-5
RET_SIZES = (RET_WIDTH,) * 4
RET_COLS = sum(RET_SIZES)

N_GROUPS = 4
EXPERTS_PER_GROUP = 8
N_EXPERTS = N_GROUPS * EXPERTS_PER_GROUP
EXPERT_FF = 512
EXPERT_TOPK = 2
MOE_BLOCK = 128

DN_ALPHA = (2 * DEPTH) ** 0.25
LN_EPS = 1e-5

LANES = 128


MM_ROWS = 1024


def _matmul_body(a_ref, b_ref, o_ref):
    o_ref[...] = jnp.dot(a_ref[...], b_ref[...], preferred_element_type=jnp.float32).astype(o_ref.dtype)


def _matmul(a, b, tn, out_dtype=jnp.float32):
    m, k = a.shape
    n = b.shape[1]
    tm = min(MM_ROWS, m)
    return pl.pallas_call(
        _matmul_body,
        grid=(m // tm, n // tn),
        in_specs=[pl.BlockSpec((tm, k), lambda i, j: (i, 0)),
                  pl.BlockSpec((k, tn), lambda i, j: (0, j))],
        out_specs=pl.BlockSpec((tm, tn), lambda i, j: (i, j)),
        out_shape=jax.ShapeDtypeStruct((m, n), out_dtype),
        compiler_params=pltpu.CompilerParams(
            dimension_semantics=("parallel", "parallel"),
            vmem_limit_bytes=48 * 1024 * 1024),
        name="dense_matmul",
    )(a, b)


MERGE_ROWS = 512
MERGE_COLS = 512


def _merge_body(y0_ref, y1_ref, y2_ref, g0_ref, g1_ref, g2_ref, w_ref, o_ref):
    acc = None
    for b, (y_ref, g_ref) in enumerate(((y0_ref, g0_ref), (y1_ref, g1_ref), (y2_ref, g2_ref))):
        t = _sigmoid(g_ref[...].astype(jnp.float32)) * _dot(y_ref[...], w_ref[b])
        acc = t if acc is None else acc + t
    o_ref[...] = acc.astype(o_ref.dtype)


def _merge_branches(ys, gates, w_br):
    S, wb = ys[0].shape
    D = w_br.shape[2]
    tm, tn = min(MERGE_ROWS, S), MERGE_COLS
    nj = D // tn
    y_spec = pl.BlockSpec((tm, wb), lambda i, j: (i, 0))
    g_spec = lambda b: pl.BlockSpec((tm, tn), lambda i, j: (i, b * nj + j))
    return pl.pallas_call(
        _merge_body,
        grid=(S // tm, nj),
        in_specs=[y_spec, y_spec, y_spec, g_spec(0), g_spec(1), g_spec(2),
                  pl.BlockSpec((3, wb, tn), lambda i, j: (0, 0, j))],
        out_specs=pl.BlockSpec((tm, tn), lambda i, j: (i, j)),
        out_shape=jax.ShapeDtypeStruct((S, D), jnp.bfloat16),
        compiler_params=pltpu.CompilerParams(dimension_semantics=("parallel", "parallel"),
                                             vmem_limit_bytes=48 * 1024 * 1024),
        name="merge_branches",
    )(*ys, gates, gates, gates, w_br)


LN_ROWS = 256
ROUTER_PAD = LANES


def _ln_rows(z, g, b):
    mu = jnp.mean(z, axis=-1, keepdims=True)
    zc = z - mu
    var = jnp.mean(zc * zc, axis=-1, keepdims=True)
    return zc * lax.rsqrt(var + LN_EPS) * g + b


def _out_ln_body(m_ref, w_ref, x_ref, g_ref, b_ref, wr_ref, br_ref, x1_ref, x1b_ref, lg_ref):
    z = DN_ALPHA * x_ref[...] + jnp.dot(m_ref[...], w_ref[...], preferred_element_type=jnp.float32)
    x1 = _ln_rows(z, g_ref[...], b_ref[...])
    x1_ref[...] = x1
    x1b = _bf(x1)
    x1b_ref[...] = x1b
    lg_ref[...] = jnp.dot(x1b, wr_ref[...], preferred_element_type=jnp.float32) + br_ref[...]


def _out_ln_router(merged, w_out, x, ln_g, ln_b, w_router, b_router):
    S, D = x.shape
    tm = min(LN_ROWS, S)
    row = pl.BlockSpec((tm, D), lambda i: (i, 0))
    full = lambda shape: pl.BlockSpec(shape, lambda i: (0, 0))
    return pl.pallas_call(
        _out_ln_body,
        grid=(S // tm,),
        in_specs=[row, full((D, D)), row, full((1, D)), full((1, D)), full((D, ROUTER_PAD)), full((1, ROUTER_PAD))],
        out_specs=(row, row, pl.BlockSpec((tm, ROUTER_PAD), lambda i: (i, 0))),
        out_shape=(jax.ShapeDtypeStruct((S, D), jnp.float32), jax.ShapeDtypeStruct((S, D), jnp.bfloat16),
                   jax.ShapeDtypeStruct((S, ROUTER_PAD), jnp.float32)),
        compiler_params=pltpu.CompilerParams(dimension_semantics=("parallel",),
                                             vmem_limit_bytes=48 * 1024 * 1024),
        name="out_proj_ln_router",
    )(merged, w_out, x, ln_g.reshape(1, D), ln_b.reshape(1, D), w_router, b_router)


def _residual_ln_body(x_ref, m_ref, g_ref, b_ref, o_ref, ob_ref):
    y = _ln_rows(DN_ALPHA * x_ref[...] + m_ref[...], g_ref[...], b_ref[...])
    o_ref[...] = y
    ob_ref[...] = _bf(y)


def _residual_ln(x, m, ln_g, ln_b):
    S, D = x.shape
    tm = min(LN_ROWS, S)
    row = pl.BlockSpec((tm, D), lambda i: (i, 0))
    full = pl.BlockSpec((1, D), lambda i: (0, 0))
    return pl.pallas_call(
        _residual_ln_body,
        grid=(S // tm,),
        in_specs=[row, row, full, full],
        out_specs=(row, row),
        out_shape=(jax.ShapeDtypeStruct((S, D), jnp.float32), jax.ShapeDtypeStruct((S, D), jnp.bfloat16)),
        compiler_params=pltpu.CompilerParams(dimension_semantics=("parallel",)),
        name="residual_ln",
    )(x, m, ln_g.reshape(1, D), ln_b.reshape(1, D))


def _split(p, sizes):
    idx = [int(i) for i in np.cumsum(sizes)[:-1]]
    return jnp.split(p, idx, axis=-1)


def _layer_norm(x, g, b):
    xf = x.astype(jnp.float32)
    mu = jnp.mean(xf, -1, keepdims=True)
    var = jnp.mean(jnp.square(xf - mu), -1, keepdims=True)
    return ((xf - mu) * lax.rsqrt(var + LN_EPS)).astype(x.dtype) * g + b


def _head_norm(x, eps):
    x = x.astype(jnp.float32)
    mu = jnp.mean(x, -1, keepdims=True)
    var = jnp.mean(jnp.square(x - mu), -1, keepdims=True)
    return (x - mu) * lax.rsqrt(var + eps)


def _token_shift(z, mu):
    prev = jnp.pad(z, ((0, 0), (1, 0), (0, 0)))[:, :-1]
    return z + (prev - z) * mu


def _masked_softmax(s, mask):
    s = jnp.where(mask, s, NEG_INF)
    p = jax.nn.softmax(s, axis=-1)
    return jnp.where(mask, p, 0.0)


RW_CHUNK = 64
LORA_PAD = LANES
RW_OFF_XW = 3 * RW_WIDTH
RW_OFF_XA = RW_OFF_XW + LORA_PAD
RW_OFF_XG = RW_OFF_XA + LORA_PAD
RW_OFF_XV = RW_OFF_XG + RW_G_LORA
RW_PAD_SIZES = (RW_WIDTH, RW_WIDTH, RW_WIDTH, LORA_PAD, LORA_PAD, RW_G_LORA)

_NT = (((1,), (1,)), ((), ()))
_TN = (((0,), (0,)), ((), ()))


def _bf(x):
    return x.astype(jnp.bfloat16)


def _dot(a, b, dims=None):
    if dims is None:
        return jnp.dot(_bf(a), _bf(b), preferred_element_type=jnp.float32)
    return lax.dot_general(_bf(a), _bf(b), dims, preferred_element_type=jnp.float32)


def _split3(x):
    h = _bf(x)
    r = x - h.astype(jnp.float32)
    m = _bf(r)
    l = _bf(r - m.astype(jnp.float32))
    return h, m, l


def _dot_exact_rhs(a, b_bf16, passes):
    acc = None
    for p in _split3(a)[:passes]:
        t = jnp.dot(p, b_bf16, preferred_element_type=jnp.float32)
        acc = t if acc is None else acc + t
    return acc


def _dot_exact_lhs(a_bf16, b, passes):
    acc = None
    for p in _split3(b)[:passes]:
        t = jnp.dot(a_bf16, p, preferred_element_type=jnp.float32)
        acc = t if acc is None else acc + t
    return acc


def _sigmoid(x):
    return 1.0 / (1.0 + jnp.exp(-x))


def _softplus(x):
    return jnp.maximum(x, 0.0) + jnp.log(1.0 + jnp.exp(-jnp.abs(x)))


def _rwkv_body(first, *refs):
    if first:
        (p_ref, mu_ref, w0_ref, wup_ref, a0_ref, aup_ref, gup_ref, kk_ref, ka_ref, rk_ref,
         lg_ref, lb_ref, ones_ref, y_ref, vf_out_ref, st_ref, prev_ref) = refs
    else:
        (p_ref, mu_ref, w0_ref, wup_ref, a0_ref, aup_ref, gup_ref, kk_ref, ka_ref, rk_ref,
         lg_ref, lb_ref, ones_ref, v0_ref, vup_ref, vf_in_ref, y_ref, st_ref, prev_ref) = refs
    C, H, N, W = RW_CHUNK, RW_HEADS, RW_HEAD, RW_WIDTH
    f32 = jnp.float32
    i = pl.program_id(0)

    @pl.when(i == 0)
    def _():
        st_ref[...] = jnp.zeros_like(st_ref)
        prev_ref[...] = jnp.zeros_like(prev_ref)

    z = p_ref[...]
    row = lax.broadcasted_iota(jnp.int32, z.shape, 0)
    prev = jnp.where(row == 0, prev_ref[...], pltpu.roll(z, 1, axis=0))
    prev_ref[...] = z[C - 1:C, :]
    xs = z + (prev - z) * mu_ref[...]

    r = xs[:, 0:W]
    k = xs[:, W:2 * W]
    v = xs[:, 2 * W:3 * W]
    xw = xs[:, RW_OFF_XW:RW_OFF_XW + LORA_PAD]
    xa = xs[:, RW_OFF_XA:RW_OFF_XA + LORA_PAD]
    xg = xs[:, RW_OFF_XG:RW_OFF_XG + RW_G_LORA]
    if first:
        vf_out_ref[...] = v
    else:
        xv = xs[:, RW_OFF_XV:RW_OFF_XV + LORA_PAD]
        v = v + (vf_in_ref[...] - v) * _sigmoid(v0_ref[...] + _dot(xv, vup_ref[...]))

    w_log = -_softplus(-(w0_ref[...] + _dot(jnp.tanh(xw), wup_ref[...]))) - 0.5
    logd = -jnp.exp(w_log)
    a = _sigmoid(a0_ref[...] + _dot(xa, aup_ref[...]))
    g = _dot(_sigmoid(xg), gup_ref[...])
    ones_blk = ones_ref[...]
    kk = k * kk_ref[...]
    kk_n2 = _dot_exact_rhs(kk * kk, ones_blk, 2)
    kk = kk / jnp.maximum(jnp.sqrt(kk_n2), 1e-12)
    k = k * (1.0 + (a - 1.0) * ka_ref[...])
    b = kk * a
    bonus = _dot_exact_rhs(r * k * rk_ref[...], ones_blk, 2) * v

    tr = lax.broadcasted_iota(jnp.int32, (C, C), 0)
    tc = lax.broadcasted_iota(jnp.int32, (C, C), 1)
    low_incl = tr >= tc
    low_strict = tr > tc
    cum = _dot_exact_lhs(low_incl.astype(jnp.bfloat16), logd, 3)
    total = cum[C - 1:C, :]
    e_in = jnp.exp(cum)
    e_ex = jnp.exp(cum - logd)
    e_neg = jnp.exp(-cum)
    e_rem = jnp.exp(total - cum)
    kkd = kk * e_ex
    kh = k * e_neg
    bh = b * e_neg
    rd = r * e_in
    khg = k * e_rem
    bhg = b * e_rem
    gam = jnp.exp(total)

    eye = (tr == tc).astype(f32)
    heads = range(H)
    hs = lambda t: [t[:, h * N:(h + 1) * N] for h in heads]
    kkd_h, kh_h, bh_h, rd_h, v_h, khg_h, bhg_h = (hs(t) for t in (kkd, kh, bh, rd, v, khg, bhg))
    st = [st_ref[h] for h in heads]
    gm = [_dot(jnp.concatenate([kkd_h[h], rd_h[h]], axis=0),
               jnp.concatenate([kh_h[h], bh_h[h]], axis=0), _NT) for h in heads]
    a_k = [jnp.where(low_strict, gm[h][0:C, 0:C], 0.0) for h in heads]
    a_b = [jnp.where(low_strict, gm[h][0:C, C:2 * C], 0.0) for h in heads]
    b_k = [jnp.where(low_incl, gm[h][C:2 * C, 0:C], 0.0) for h in heads]
    b_b = [jnp.where(low_incl, gm[h][C:2 * C, C:2 * C], 0.0) for h in heads]
    rhs = [_dot(kkd_h[h], st[h]) + _dot(a_k[h], v_h[h]) for h in heads]
    y0 = [_dot(rd_h[h], st[h]) + _dot(b_k[h], v_h[h]) for h in heads]
    t_inv = [eye - jnp.where((tr == tc + 1) & (tr % 2 == 1), a_b[h], 0.0) for h in heads]
    m = 2
    while m < C:
        rb, cb = tr // m, tc // m
        sib = (rb == cb + 1) & (rb % 2 == 1)
        tmp = [_dot(jnp.where(sib, a_b[h], 0.0), t_inv[h]) for h in heads]
        t_inv = [t_inv[h] - _dot(t_inv[h], tmp[h]) for h in heads]
        m *= 2
    zz = [_dot(t_inv[h], rhs[h]) for h in heads]
    ys = [y0[h] - _dot(b_b[h], zz[h]) for h in heads]
    upd = [_dot(khg_h[h], v_h[h], _TN) - _dot(bhg_h[h], zz[h], _TN) for h in heads]
    for h in heads:
        st_ref[h] = st[h] * jnp.transpose(gam[:, h * N:(h + 1) * N]) + upd[h]
    y = jnp.concatenate(ys, axis=1)

    inv_n = 1.0 / N
    mean = _dot_exact_rhs(y, ones_blk, 2) * inv_n
    yc = y - mean
    var = _dot_exact_rhs(yc * yc, ones_blk, 2) * inv_n
    yn = yc * lax.rsqrt(var + RW_GN_EPS) * lg_ref[...] + lb_ref[...]
    y_ref[...] = (yn + bonus) * g


def _pad_rows(w, rows):
    return jnp.pad(w, ((0, rows - w.shape[0]), (0, 0)))


def _pad_cols(p, sizes, padded):
    parts = _split(p, sizes)
    return jnp.concatenate([jnp.pad(t, [(0, 0)] * (t.ndim - 1) + [(0, n - t.shape[-1])])
                            for t, n in zip(parts, padded)], -1)


def _rwkv7_time_mix(p, mu, w0, w_up, a0, a_up, g_up, k_k, k_a, r_k, lnx_g, lnx_b, v_first, v0, v_up):
    S, cols = p.shape
    first = v_first is None
    C, W = RW_CHUNK, RW_WIDTH
    row = lambda t: t.reshape(1, -1)
    hid = np.arange(W) // RW_HEAD
    ones_blk = jnp.asarray(hid[:, None] == hid[None, :], jnp.bfloat16)
    full = lambda shape: pl.BlockSpec(shape, lambda i: (0,) * len(shape))
    tok = lambda width: pl.BlockSpec((C, width), lambda i: (i, 0))
    args = [p, row(mu), row(w0), _pad_rows(w_up, LORA_PAD), row(a0), _pad_rows(a_up, LORA_PAD), g_up,
            row(k_k), row(k_a), row(r_k), row(lnx_g), row(lnx_b), ones_blk]
    specs = [tok(cols), full((1, cols)), full((1, W)), full((LORA_PAD, W)), full((1, W)), full((LORA_PAD, W)),
             full((RW_G_LORA, W)), full((1, W)), full((1, W)), full((1, W)), full((1, W)), full((1, W)),
             full((W, W))]
    if first:
        out_shape = (jax.ShapeDtypeStruct((S, W), jnp.float32), jax.ShapeDtypeStruct((S, W), jnp.float32))
        out_specs = (tok(W), tok(W))
    else:
        args += [row(v0), _pad_rows(v_up, LORA_PAD), v_first]
        specs += [full((1, W)), full((LORA_PAD, W)), tok(W)]
        out_shape = jax.ShapeDtypeStruct((S, W), jnp.float32)
        out_specs = tok(W)
    res = pl.pallas_call(
        functools.partial(_rwkv_body, first),
        grid=(S // C,),
        in_specs=specs,
        out_specs=out_specs,
        out_shape=out_shape,
        scratch_shapes=[pltpu.VMEM((RW_HEADS, RW_HEAD, RW_HEAD), jnp.float32),
                        pltpu.VMEM((1, cols), jnp.float32)],
        compiler_params=pltpu.CompilerParams(dimension_semantics=("arbitrary",),
                                             vmem_limit_bytes=48 * 1024 * 1024),
        name="rwkv7_chunked",
    )(*args)
    if first:
        return res[0], res[1]
    return res, v_first


CMP_ROWS = 256


def _compress_body(r_ref, nx_ref, pe_ref, wa_ref, wb_ref, b1_ref, w2_ref, o_ref):
    r = r_ref[...]
    rb, half = r.shape
    row = lax.broadcasted_iota(jnp.int32, r.shape, 0)
    nxt = jnp.where(row == rb - 1, nx_ref[0:1, :], pltpu.roll(r, rb - 1, axis=0))
    pe = pe_ref[...]
    h = (_dot(r + pe[:, :half], wa_ref[...]) + _dot(nxt + pe[:, half:], wb_ref[...]) + b1_ref[...])
    h = 0.5 * h * (1.0 + jnp.tanh(0.7978845608028654 * (h + 0.044715 * (h * h * h))))
    o_ref[...] = _dot(h, w2_ref[...])


def _nsa_compress(t, pe, w1, b1, w2):
    S = t.shape[0]
    G, DH, L, ST = NSA_KV_GROUPS, NSA_HEAD, CMP_LEN, CMP_STRIDE
    nr = S // ST
    rb = min(CMP_ROWS, nr)
    cols = ST * G * DH
    r = t.reshape(nr, cols)
    eye_g = jnp.eye(G, dtype=w1.dtype)
    big = jnp.einsum('ldh,gk->lgdkh', w1.reshape(L, DH, CMP_HIDDEN), eye_g).reshape(L * G * DH, G * CMP_HIDDEN)
    wa, wb = big[:cols], big[cols:]
    pe_big = jnp.broadcast_to(pe[:, None, :], (L, G, DH)).reshape(1, L * G * DH)
    b1_big = jnp.tile(b1, G).reshape(1, G * CMP_HIDDEN)
    w2_big = jnp.einsum('hd,gk->ghkd', w2, eye_g).reshape(G * CMP_HIDDEN, G * DH)
    full = lambda shape: pl.BlockSpec(shape, lambda i: (0, 0))
    return pl.pallas_call(
        _compress_body,
        grid=(nr // rb,),
        in_specs=[pl.BlockSpec((rb, cols), lambda i: (i, 0)),
                  pl.BlockSpec((8, cols), lambda i: (jnp.minimum((i + 1) * (rb // 8), nr // 8 - 1), 0)),
                  full((1, 2 * cols)), full((cols, G * CMP_HIDDEN)), full((cols, G * CMP_HIDDEN)),
                  full((1, G * CMP_HIDDEN)), full((G * CMP_HIDDEN, G * DH))],
        out_specs=pl.BlockSpec((rb, G * DH), lambda i: (i, 0)),
        out_shape=jax.ShapeDtypeStruct((nr, G * DH), jnp.float32),
        compiler_params=pltpu.CompilerParams(dimension_semantics=("parallel",),
                                             vmem_limit_bytes=48 * 1024 * 1024),
        name="nsa_compress",
    )(r, r, pe_big, _bf(wa), _bf(wb), b1_big, _bf(w2_big))


NSA_QB = 128
NSA_KC = 512
NSA_WC = 128
NSA_GATE_PAD = LANES
NSA_PCOLS = NSA_WIDTH + 6 * NSA_KV + NSA_GATE_PAD
NSA_GATE_ROWS = 16
LOG2E = 1.4426950408889634


def _softmax_cols(s, mask):
    s = jnp.where(mask, s, NEG_INF)
    m = jnp.max(s, axis=0, keepdims=True)
    p = jnp.where(mask, jnp.exp2(s - m), 0.0)
    l = jnp.sum(p, axis=0, keepdims=True)
    return p * (1.0 / jnp.where(l > 0.0, l, 1.0))


def _nsa_body(*refs):
    n_win = (WINDOW + NSA_QB) // NSA_WC
    qt_ref, kvc_ref, kvct_ref, kvs_ref, kvst_ref = refs[:5]
    kvw_refs = refs[5:5 + n_win]
    kvwt_refs = refs[5 + n_win:5 + 2 * n_win]
    ovl_ref, gate_ref, o_ref, bias_ref, acc_ref, sca_ref, scb_ref = refs[5 + 2 * n_win:]
    QB, HPG, DH = NSA_QB, NSA_HPG, NSA_HEAD
    f32 = jnp.float32
    qi = pl.program_id(1)
    q0 = qi * QB
    qt = qt_ref[...]
    ncmp = kvc_ref.shape[0]
    t_row = q0 + lax.broadcasted_iota(jnp.int32, (1, QB), 1)
    head = lambda a, h: a[:, h * QB:(h + 1) * QB]

    s = jnp.dot(kvc_ref[...], qt, preferred_element_type=f32)
    c_end = lax.broadcasted_iota(jnp.int32, (ncmp, QB), 0) * CMP_STRIDE + (CMP_LEN - 1)
    m_cmp = c_end <= t_row
    kvct = kvct_ref[...]
    p_sum = None
    o_cmp = []
    for h in range(HPG):
        p = _softmax_cols(head(s, h), m_cmp)
        o_cmp.append(jnp.dot(kvct, _bf(p), preferred_element_type=f32))
        p_sum = p if p_sum is None else p_sum + p
    imp = _dot_exact_lhs(ovl_ref[...], p_sum, 2)
    nsel = imp.shape[0]

    n_row = lax.broadcasted_iota(jnp.int32, (nsel, QB), 0)
    cur = t_row // SEL_BLOCK
    forced = (n_row == 0) | (n_row == cur) | (n_row == cur - 1)
    score = jnp.where(n_row <= cur, imp + jnp.where(forced, FORCE_BONUS, 0.0), NEG_INF)
    bias = jnp.full((nsel, QB), NEG_INF, f32)
    for _ in range(min(SEL_TOPN, nsel)):
        best = jnp.max(score, axis=0, keepdims=True)
        first = jnp.min(jnp.where(score == best, n_row, nsel), axis=0, keepdims=True)
        hit = n_row == first
        bias = jnp.where(hit & (best > 0.5 * NEG_INF), 0.0, bias)
        score = jnp.where(hit, -jnp.inf, score)
    bias_ref[...] = bias

    KC = NSA_KC
    n_sub = KC // SEL_BLOCK
    key_row = lax.broadcasted_iota(jnp.int32, (KC, QB), 0)
    acc_ref[...] = jnp.zeros_like(acc_ref)
    last_chunk = kvs_ref.shape[0] // KC - 1

    def chunk_scores(idx):
        k0 = pl.multiple_of(jnp.minimum(idx, last_chunk) * KC, KC)
        return jnp.dot(kvs_ref[pl.ds(k0, KC), :], qt, preferred_element_type=f32)

    def consume(sc_ref, idx, ms, ls):
        c = jnp.minimum(idx, last_chunk)
        kvt = kvst_ref[:, pl.ds(pl.multiple_of(c * KC, KC), KC)]
        rows = [jnp.broadcast_to(bias_ref[pl.ds(c * n_sub + b, 1), :], (SEL_BLOCK, QB)) for b in range(n_sub)]
        madd = jnp.where(idx * KC + key_row <= t_row, jnp.concatenate(rows, axis=0), NEG_INF)
        sh = sc_ref[...] + jnp.concatenate([madd] * HPG, axis=1)
        new_m = jnp.maximum(ms, jnp.max(sh, axis=0, keepdims=True))
        alpha = jnp.exp2(ms - new_m)
        p = jnp.exp2(sh - new_m)
        new_l = alpha * ls + jnp.sum(p, axis=0, keepdims=True)
        acc_ref[...] = alpha * acc_ref[...] + jnp.dot(kvt, _bf(p), preferred_element_type=f32)
        return new_m, new_l

    sca_ref[...] = chunk_scores(0)

    def pair_step(jj, carry):
        ms, ls = carry
        scb_ref[...] = chunk_scores(2 * jj + 1)
        ms, ls = consume(sca_ref, 2 * jj, ms, ls)
        sca_ref[...] = chunk_scores(2 * jj + 2)
        return consume(scb_ref, 2 * jj + 1, ms, ls)

    init = (jnp.full((1, HPG * QB), NEG_INF, f32), jnp.zeros((1, HPG * QB), f32))
    n_chunks = (q0 + QB + KC - 1) // KC
    _, l_sel = lax.fori_loop(0, (n_chunks + 1) // 2, pair_step, init)

    WC = NSA_WC
    s_win, m_win = [], []
    for c in range(n_win):
        s_pos = q0 - WINDOW + c * WC + lax.broadcasted_iota(jnp.int32, (WC, QB), 0)
        dist = t_row - s_pos
        m_win.append((dist >= 0) & (dist < WINDOW) & (s_pos >= 0))
        s_win.append(jnp.dot(kvw_refs[c][...], qt, preferred_element_type=f32))
    m_all = jnp.concatenate(m_win, axis=0)
    s_all = jnp.concatenate(s_win, axis=0)

    gate = _sigmoid(gate_ref[...])
    outs = []
    for h in range(HPG):
        p = _bf(_softmax_cols(head(s_all, h), m_all))
        o_win = None
        for c in range(n_win):
            t = jnp.dot(kvwt_refs[c][...], p[c * WC:(c + 1) * WC], preferred_element_type=f32)
            o_win = t if o_win is None else o_win + t
        o_sel = head(acc_ref[...], h) * (1.0 / head(l_sel, h))
        gr = lambda b: gate[h * 3 + b:h * 3 + b + 1, :]
        o = gr(0) * o_cmp[h] + gr(1) * o_sel + gr(2) * o_win
        outs.append(o[DH:, :])
    o_ref[...] = jnp.transpose(jnp.concatenate(outs, axis=0))


def _nsa(p, cmp_pe, cmp_w1, cmp_b1, cmp_w2):
    S = p.shape[0]
    G, HPG, DH, QB, WC = NSA_KV_GROUPS, NSA_HPG, NSA_HEAD, NSA_QB, NSA_WC
    W, KV = NSA_WIDTH, NSA_KV
    nqb = S // QB
    n_win = (WINDOW + QB) // WC
    q = p[:, :W]
    kc, vc, ks, vs, kw, vw = (p[:, W + i * KV:W + (i + 1) * KV] for i in range(6))
    gate = p[:, W + 6 * KV:W + 6 * KV + 3 * NSA_HEADS]
    k_cmp = _nsa_compress(kc, cmp_pe[0], cmp_w1[0], cmp_b1[0], cmp_w2[0])
    v_cmp = _nsa_compress(vc, cmp_pe[1], cmp_w1[1], cmp_b1[1], cmp_w2[1])
    pack = lambda k, v: _bf(jnp.concatenate([k.reshape(-1, G, DH), v.reshape(-1, G, DH)], -1)).transpose(1, 0, 2)
    kvc, kvs, kvw = pack(k_cmp, v_cmp), pack(ks, vs), pack(kw, vw)
    tr = lambda a: a.transpose(0, 2, 1)
    qh = q.reshape(nqb, QB, G, HPG, DH) * (DH ** -0.5 * LOG2E)
    qh = _bf(jnp.concatenate([qh, jnp.zeros_like(qh)], -1)).transpose(2, 0, 4, 3, 1).reshape(G, nqb, LANES, HPG * QB)
    gate_t = jnp.pad(gate.reshape(S, G, 3 * HPG).transpose(1, 2, 0), ((0, 0), (0, NSA_GATE_ROWS - 3 * HPG), (0, 0)))
    ncmp, nsel = S // CMP_STRIDE, S // SEL_BLOCK
    c_start = np.arange(ncmp)[None, :] * CMP_STRIDE
    s_start = np.arange(nsel)[:, None] * SEL_BLOCK
    overlap_t = jnp.asarray((c_start < s_start + SEL_BLOCK) & (c_start + CMP_LEN > s_start), jnp.bfloat16)
    win_blk = lambda c: (lambda g, i: (g, jnp.maximum(i - WINDOW // WC + c, 0), 0))
    win_blk_t = lambda c: (lambda g, i: (g, 0, jnp.maximum(i - WINDOW // WC + c, 0)))
    in_specs = ([pl.BlockSpec((None, None, LANES, HPG * QB), lambda g, i: (g, i, 0, 0)),
                 pl.BlockSpec((None, ncmp, LANES), lambda g, i: (g, 0, 0)),
                 pl.BlockSpec((None, LANES, ncmp), lambda g, i: (g, 0, 0)),
                 pl.BlockSpec((None, S, LANES), lambda g, i: (g, 0, 0)),
                 pl.BlockSpec((None, LANES, S), lambda g, i: (g, 0, 0))]
                + [pl.BlockSpec((None, WC, LANES), win_blk(c)) for c in range(n_win)]
                + [pl.BlockSpec((None, LANES, WC), win_blk_t(c)) for c in range(n_win)]
                + [pl.BlockSpec((nsel, ncmp), lambda g, i: (0, 0)),
                   pl.BlockSpec((None, NSA_GATE_ROWS, QB), lambda g, i: (g, 0, i))])
    kvwt = tr(kvw)
    return pl.pallas_call(
        _nsa_body,
        grid=(G, nqb),
        in_specs=in_specs,
        out_specs=pl.BlockSpec((QB, HPG * DH), lambda g, i: (i, g)),
        out_shape=jax.ShapeDtypeStruct((S, W), jnp.float32),
        scratch_shapes=[pltpu.VMEM((nsel, QB), jnp.float32),
                        pltpu.VMEM((LANES, HPG * QB), jnp.float32),
                        pltpu.VMEM((NSA_KC, HPG * QB), jnp.float32),
                        pltpu.VMEM((NSA_KC, HPG * QB), jnp.float32)],
        compiler_params=pltpu.CompilerParams(dimension_semantics=("arbitrary", "arbitrary"),
                                             vmem_limit_bytes=56 * 1024 * 1024),
        name="nsa_attention",
    )(qh, kvc, tr(kvc), kvs, tr(kvs), *([kvw] * n_win), *([kvwt] * n_win), overlap_t, gate_t)


def _retention_body(p_ref, cc_ref, ss_ref, inner_ref, qd_ref, kd_ref, cd_ref, o_ref, r_ref):
    H, DK, W = RET_HEADS, RET_HEAD, RET_WIDTH
    f32 = jnp.float32

    @pl.when(pl.program_id(0) == 0)
    def _():
        r_ref[...] = jnp.zeros_like(r_ref)

    cc, ss = cc_ref[...], ss_ref[...]
    hh = range(H)
    col = lambda part, h: p_ref[:, part * W + h * DK:part * W + (h + 1) * DK].astype(f32)
    rot = lambda t: t * cc + pltpu.roll(t, DK // 2, axis=1) * ss
    q = [rot(col(0, h)) for h in hh]
    k = [rot(col(1, h)) * (DK ** -0.5) for h in hh]
    v = [col(2, h) for h in hh]
    r = [r_ref[h] for h in hh]
    att = [_dot(q[h], k[h], _NT) * inner_ref[h] for h in hh]
    o = [_dot(att[h], v[h]) + _dot(q[h], r[h]) * qd_ref[h] for h in hh]
    upd = [_dot(k[h] * kd_ref[h], v[h], _TN) for h in hh]
    for h in hh:
        r_ref[h] = r[h] * cd_ref[h] + upd[h]
        mu = jnp.mean(o[h], axis=-1, keepdims=True)
        oc = o[h] - mu
        var = jnp.mean(oc * oc, axis=-1, keepdims=True)
        g = col(3, h)
        o_ref[:, h * DK:(h + 1) * DK] = (g * _sigmoid(g)) * (oc * lax.rsqrt(var + RET_GN_EPS))


def _retention(p):
    S = p.shape[0]
    H, DK, C = RET_HEADS, RET_HEAD, RET_CHUNK
    f32 = jnp.float32
    inv_freq = 1.0 / (RET_THETA ** jnp.linspace(0.0, 1.0, DK // 2))
    ang = jnp.arange(S, dtype=f32)[:, None] * inv_freq[None, :]
    cos, sin = jnp.cos(ang), jnp.sin(ang)
    cc = jnp.concatenate([cos, cos], -1)
    ss = jnp.concatenate([-sin, sin], -1)
    log_g = jnp.log1p(-jnp.exp2(-5.0 - jnp.arange(H, dtype=f32)))
    i = jnp.arange(C, dtype=f32)
    diff = i[:, None] - i[None, :]
    inner_decay = jnp.where(diff >= 0, jnp.exp(log_g[:, None, None] * jnp.maximum(diff, 0.0)), 0.0)
    lanes = lambda t: jnp.broadcast_to(t, (H, C, DK))
    q_decay = lanes(jnp.exp(log_g[:, None] * (i + 1.0))[..., None])
    k_decay = lanes(jnp.exp(log_g[:, None] * (C - 1.0 - i))[..., None])
    c_decay = lanes(jnp.exp(log_g * C)[:, None, None])
    tok = lambda width: pl.BlockSpec((C, width), lambda n: (n, 0))
    const = pl.BlockSpec((H, C, DK), lambda n: (0, 0, 0))
    return pl.pallas_call(
        _retention_body,
        grid=(S // C,),
        in_specs=[tok(p.shape[1]), tok(DK), tok(DK), const, const, const, const],
        out_specs=tok(RET_WIDTH),
        out_shape=jax.ShapeDtypeStruct((S, RET_WIDTH), f32),
        scratch_shapes=[pltpu.VMEM((H, DK, DK), f32)],
        compiler_params=pltpu.CompilerParams(dimension_semantics=("arbitrary",),
                                             vmem_limit_bytes=48 * 1024 * 1024),
        name="retention_chunked",
    )(p, cc, ss, inner_decay, q_decay, k_decay, c_decay)


MOE_ROWS = 256


def _expert_body(blk_e_ref, n_used_ref, x_ref, wg_ref, wu_ref, wd_ref, o_ref):
    i = pl.program_id(0)

    @pl.when(i < n_used_ref[0])
    def _():
        x = x_ref[...]
        hg = _dot(x, wg_ref[...])
        h = (hg * _sigmoid(hg)) * _dot(x, wu_ref[...])
        o_ref[...] = _dot(h, wd_ref[...])

    @pl.when(i >= n_used_ref[0])
    def _():
        o_ref[...] = jnp.zeros_like(o_ref)


def _moe_experts(x_slots, blk_expert, n_used, w_gate, w_up, w_down):
    n_slots, D = x_slots.shape
    F = w_gate.shape[2]
    mb = MOE_ROWS
    grid_spec = pltpu.PrefetchScalarGridSpec(
        num_scalar_prefetch=2,
        grid=(n_slots // mb,),
        in_specs=[pl.BlockSpec((mb, D), lambda i, be, nu: (i, 0)),
                  pl.BlockSpec((None, D, F), lambda i, be, nu: (be[i], 0, 0)),
                  pl.BlockSpec((None, D, F), lambda i, be, nu: (be[i], 0, 0)),
                  pl.BlockSpec((None, F, D), lambda i, be, nu: (be[i], 0, 0))],
        out_specs=pl.BlockSpec((mb, D), lambda i, be, nu: (i, 0)))
    return pl.pallas_call(
        _expert_body,
        grid_spec=grid_spec,
        out_shape=jax.ShapeDtypeStruct((n_slots, D), jnp.float32),
        compiler_params=pltpu.CompilerParams(dimension_semantics=("arbitrary",),
                                             vmem_limit_bytes=56 * 1024 * 1024),
        name="moe_experts",
    )(blk_expert, n_used, x_slots, w_gate, w_up, w_down)


def _hier_moe(x1b, logits, w_gate, w_up, w_down):
    T, D = x1b.shape
    f32 = jnp.float32
    grp_logits = logits[:, :N_GROUPS]
    grp = jnp.argmax(grp_logits, axis=-1)
    grp_w = jnp.take_along_axis(jax.nn.softmax(grp_logits, -1), grp[:, None], axis=-1)
    exp_logits = logits[:, N_GROUPS:N_GROUPS + N_EXPERTS].reshape(T, N_GROUPS, EXPERTS_PER_GROUP)
    in_grp = jnp.take_along_axis(exp_logits, grp[:, None, None], axis=1)[:, 0]
    e_iota = jnp.arange(EXPERTS_PER_GROUP, dtype=jnp.int32)[None, :]
    i1 = jnp.argmax(in_grp, axis=-1).astype(jnp.int32)
    rest = jnp.where(e_iota == i1[:, None], -jnp.inf, in_grp)
    i2 = jnp.argmax(rest, axis=-1).astype(jnp.int32)
    top_idx = jnp.stack([i1, i2], -1)
    top_val = jnp.stack([jnp.max(in_grp, -1), jnp.max(rest, -1)], -1)
    gate_w = (jax.nn.softmax(top_val, -1) * grp_w).reshape(-1)
    expert = (grp[:, None] * EXPERTS_PER_GROUP + top_idx).reshape(-1).astype(jnp.int32)
    token = jnp.repeat(jnp.arange(T, dtype=jnp.int32), EXPERT_TOPK)
    n_assign = T * EXPERT_TOPK
    mb = MOE_ROWS
    onehot = (expert[:, None] == jnp.arange(N_EXPERTS, dtype=jnp.int32)[None, :]).astype(jnp.int32)
    csum = jnp.cumsum(onehot, axis=0)
    pos = jnp.take_along_axis(csum, expert[:, None], axis=1)[:, 0] - 1
    counts = csum[-1]
    padded = (counts + mb - 1) // mb * mb
    pad_end = jnp.cumsum(padded)
    slot = (pad_end - padded)[expert] + pos
    n_blocks = -(-n_assign // mb) + N_EXPERTS
    n_slots = n_blocks * mb
    slot_tok = jnp.full((n_slots,), T, jnp.int32).at[slot].set(token)
    blk_start = jnp.arange(n_blocks, dtype=jnp.int32) * mb
    blk_expert = jnp.minimum(jnp.sum((pad_end[None, :] <= blk_start[:, None]).astype(jnp.int32), axis=1),
                             N_EXPERTS - 1)
    n_used = (pad_end[-1:] // mb).astype(jnp.int32)
    x_slots = jnp.concatenate([x1b, jnp.zeros((1, D), x1b.dtype)])[slot_tok]
    y = _moe_experts(x_slots, blk_expert, n_used, w_gate, w_up, w_down)
    ys = y[slot].reshape(T, EXPERT_TOPK, D) * gate_w.reshape(T, EXPERT_TOPK, 1)
    return ys[:, 0] + ys[:, 1]


def kernel(x, w_in_first, w_in_deep, rw_mu_first, rw_mu_deep, rw_w0, rw_w_up, rw_a0, rw_a_up, rw_v0, rw_v_up, rw_g_up, rw_k_k, rw_k_a, rw_r_k, rw_lnx_g, rw_lnx_b, nsa_cmp_pe, nsa_cmp_w1, nsa_cmp_b1, nsa_cmp_w2, w_br_rw, w_br_nsa, w_br_ret, w_out, ln1_g, ln1_b, moe_w_grp, moe_b_grp, moe_w_exp, moe_b_exp, moe_w_gate, moe_w_up, moe_w_down, ln2_g, ln2_b):
    B, S, D = x.shape
    assert B == 1
    x = x.reshape(S, D)
    xb = _bf(x)
    v_first = None
    for l in range(DEPTH):
        first = l == 0
        w_in = w_in_first if first else w_in_deep[l - 1]
        mu = rw_mu_first if first else rw_mu_deep[l - 1]
        rw_cols = RW_COLS_FIRST if first else RW_COLS_DEEP
        rw_sizes = RW_BASE_SIZES if first else RW_BASE_SIZES + (RW_V_LORA,)
        rw_padded = RW_PAD_SIZES if first else RW_PAD_SIZES + (LORA_PAD,)
        rw_tn = 512 if first else 768
        tail = -sum(rw_padded) % rw_tn
        w_rw = _bf(jnp.pad(_pad_cols(w_in[:, :rw_cols], rw_sizes, rw_padded), ((0, 0), (0, tail))))
        mu_p = jnp.pad(_pad_cols(mu, rw_sizes, rw_padded), (0, tail))
        c0 = rw_cols
        w_nsa = _bf(jnp.pad(w_in[:, c0:c0 + NSA_COLS], ((0, 0), (0, NSA_PCOLS - NSA_COLS))))
        c0 += NSA_COLS
        w_ret = _bf(w_in[:, c0:c0 + RET_COLS])
        w_gates = _bf(w_in[:, c0 + RET_COLS:])
        p_rw = _matmul(xb, w_rw, tn=rw_tn)
        p_nsa = _matmul(xb, w_nsa, tn=NSA_PCOLS // 3)
        p_ret = _matmul(xb, w_ret, tn=512)
        gates = _matmul(xb, w_gates, tn=512, out_dtype=jnp.bfloat16)
        y_rw, v_first = _rwkv7_time_mix(p_rw, mu_p, rw_w0[l], rw_w_up[l], rw_a0[l],
                                        rw_a_up[l], rw_g_up[l], rw_k_k[l], rw_k_a[l], rw_r_k[l], rw_lnx_g[l],
                                        rw_lnx_b[l], v_first,
                                        None if first else rw_v0[l - 1], None if first else rw_v_up[l - 1])
        y_nsa = _nsa(p_nsa, nsa_cmp_pe[l], nsa_cmp_w1[l], nsa_cmp_b1[l], nsa_cmp_w2[l])
        y_ret = _retention(p_ret)
        w_br = _bf(jnp.stack([w_br_rw[l], w_br_nsa[l], w_br_ret[l]]))
        merged = _merge_branches((y_rw, y_nsa, y_ret), gates, w_br)
        w_router = jnp.pad(jnp.concatenate([moe_w_grp[l], moe_w_exp[l]], 1),
                           ((0, 0), (0, ROUTER_PAD - N_GROUPS - N_EXPERTS)))
        b_router = jnp.pad(jnp.concatenate([moe_b_grp[l], moe_b_exp[l]]),
                           (0, ROUTER_PAD - N_GROUPS - N_EXPERTS)).reshape(1, ROUTER_PAD)
        x1, x1b, logits = _out_ln_router(merged, _bf(w_out[l]), x, ln1_g[l], ln1_b[l], _bf(w_router), b_router)
        moe = _hier_moe(x1b, logits, moe_w_gate[l], moe_w_up[l], moe_w_down[l])
        x, xb = _residual_ln(x1, moe, ln2_g[l], ln2_b[l])
    return x.reshape(B, S, D)
```

```python
import functools

import jax
import jax.numpy as jnp
from jax import lax
import numpy as np
from jax.experimental import pallas as pl
from jax.experimental.pallas import tpu as pltpu

D_MODEL = 2048
DEPTH = 2

RW_HEADS = 16
RW_HEAD = 64
RW_WIDTH = RW_HEADS * RW_HEAD
RW_DECAY_LORA = 96
RW_A_LORA = 96
RW_V_LORA = 64
RW_G_LORA = 256
RW_GN_EPS = 64e-5
RW_BASE_SIZES = (RW_WIDTH, RW_WIDTH, RW_WIDTH, RW_DECAY_LORA, RW_A_LORA, RW_G_LORA)
RW_COLS_FIRST = sum(RW_BASE_SIZES)
RW_COLS_DEEP = RW_COLS_FIRST + RW_V_LORA

NSA_HEADS = 16
NSA_KV_GROUPS = 4
NSA_HPG = NSA_HEADS // NSA_KV_GROUPS
NSA_HEAD = 64
NSA_WIDTH = NSA_HEADS * NSA_HEAD
NSA_KV = NSA_KV_GROUPS * NSA_HEAD
CMP_LEN = 32
CMP_STRIDE = 16
CMP_HIDDEN = 128
SEL_BLOCK = 64
SEL_TOPN = 16
WINDOW = 512
Q_BLOCK = 128
FORCE_BONUS = 1e4
NEG_INF = -1e30
NSA_SIZES = (NSA_WIDTH,) + (NSA_KV,) * 6 + (3 * NSA_HEADS,)
NSA_COLS = sum(NSA_SIZES)

RET_HEADS = 8
RET_HEAD = 128
RET_WIDTH = RET_HEADS * RET_HEAD
RET_CHUNK = 128
RET_THETA = 10000.0
RET_GN_EPS = 1e-5
RET_SIZES = (RET_WIDTH,) * 4
RET_COLS = sum(RET_SIZES)

N_GROUPS = 4
EXPERTS_PER_GROUP = 8
N_EXPERTS = N_GROUPS * EXPERTS_PER_GROUP
EXPERT_FF = 512
EXPERT_TOPK = 2
MOE_BLOCK = 128

DN_ALPHA = (2 * DEPTH) ** 0.25
LN_EPS = 1e-5

LANES = 128


MM_ROWS = 1024


def _matmul_body(a_ref, b_ref, o_ref):
    o_ref[...] = jnp.dot(a_ref[...], b_ref[...], preferred_element_type=jnp.float32).astype(o_ref.dtype)


def _matmul(a, b, tn, out_dtype=jnp.float32):
    m, k = a.shape
    n = b.shape[1]
    tm = min(MM_ROWS, m)
    return pl.pallas_call(
        _matmul_body,
        grid=(m // tm, n // tn),
        in_specs=[pl.BlockSpec((tm, k), lambda i, j: (i, 0)),
                  pl.BlockSpec((k, tn), lambda i, j: (0, j))],
        out_specs=pl.BlockSpec((tm, tn), lambda i, j: (i, j)),
        out_shape=jax.ShapeDtypeStruct((m, n), out_dtype),
        compiler_params=pltpu.CompilerParams(
            dimension_semantics=("parallel", "parallel"),
            vmem_limit_bytes=48 * 1024 * 1024),
        name="dense_matmul",
    )(a, b)


MERGE_ROWS = 512
MERGE_COLS = 512


def _merge_body(y0_ref, y1_ref, y2_ref, g0_ref, g1_ref, g2_ref, w_ref, o_ref):
    acc = None
    for b, (y_ref, g_ref) in enumerate(((y0_ref, g0_ref), (y1_ref, g1_ref), (y2_ref, g2_ref))):
        t = _sigmoid(g_ref[...].astype(jnp.float32)) * _dot(y_ref[...], w_ref[b])
        acc = t if acc is None else acc + t
    o_ref[...] = acc.astype(o_ref.dtype)


def _merge_branches(ys, gates, w_br):
    S, wb = ys[0].shape
    D = w_br.shape[2]
    tm, tn = min(MERGE_ROWS, S), MERGE_COLS
    nj = D // tn
    y_spec = pl.BlockSpec((tm, wb), lambda i, j: (i, 0))
    g_spec = lambda b: pl.BlockSpec((tm, tn), lambda i, j: (i, b * nj + j))
    return pl.pallas_call(
        _merge_body,
        grid=(S // tm, nj),
        in_specs=[y_spec, y_spec, y_spec, g_spec(0), g_spec(1), g_spec(2),
                  pl.BlockSpec((3, wb, tn), lambda i, j: (0, 0, j))],
        out_specs=pl.BlockSpec((tm, tn), lambda i, j: (i, j)),
        out_shape=jax.ShapeDtypeStruct((S, D), jnp.bfloat16),
        compiler_params=pltpu.CompilerParams(dimension_semantics=("parallel", "parallel"),
                                             vmem_limit_bytes=48 * 1024 * 1024),
        name="merge_branches",
    )(*ys, gates, gates, gates, w_br)


LN_ROWS = 256
ROUTER_PAD = LANES


def _ln_rows(z, g, b):
    mu = jnp.mean(z, axis=-1, keepdims=True)
    zc = z - mu
    var = jnp.mean(zc * zc, axis=-1, keepdims=True)
    return zc * lax.rsqrt(var + LN_EPS) * g + b


def _out_ln_body(m_ref, w_ref, x_ref, g_ref, b_ref, wr_ref, br_ref, x1_ref, x1b_ref, lg_ref):
    z = DN_ALPHA * x_ref[...] + jnp.dot(m_ref[...], w_ref[...], preferred_element_type=jnp.float32)
    x1 = _ln_rows(z, g_ref[...], b_ref[...])
    x1_ref[...] = x1
    x1b = _bf(x1)
    x1b_ref[...] = x1b
    lg_ref[...] = jnp.dot(x1b, wr_ref[...], preferred_element_type=jnp.float32) + br_ref[...]


def _out_ln_router(merged, w_out, x, ln_g, ln_b, w_router, b_router):
    S, D = x.shape
    tm = min(LN_ROWS, S)
    row = pl.BlockSpec((tm, D), lambda i: (i, 0))
    full = lambda shape: pl.BlockSpec(shape, lambda i: (0, 0))
    return pl.pallas_call(
        _out_ln_body,
        grid=(S // tm,),
        in_specs=[row, full((D, D)), row, full((1, D)), full((1, D)), full((D, ROUTER_PAD)), full((1, ROUTER_PAD))],
        out_specs=(row, row, pl.BlockSpec((tm, ROUTER_PAD), lambda i: (i, 0))),
        out_shape=(jax.ShapeDtypeStruct((S, D), jnp.float32), jax.ShapeDtypeStruct((S, D), jnp.bfloat16),
                   jax.ShapeDtypeStruct((S, ROUTER_PAD), jnp.float32)),
        compiler_params=pltpu.CompilerParams(dimension_semantics=("parallel",),
                                             vmem_limit_bytes=48 * 1024 * 1024),
        name="out_proj_ln_router",
    )(merged, w_out, x, ln_g.reshape(1, D), ln_b.reshape(1, D), w_router, b_router)


def _residual_ln_body(x_ref, m_ref, g_ref, b_ref, o_ref, ob_ref):
    y = _ln_rows(DN_ALPHA * x_ref[...] + m_ref[...], g_ref[...], b_ref[...])
    o_ref[...] = y
    ob_ref[...] = _bf(y)


def _residual_ln(x, m, ln_g, ln_b):
    S, D = x.shape
    tm = min(LN_ROWS, S)
    row = pl.BlockSpec((tm, D), lambda i: (i, 0))
    full = pl.BlockSpec((1, D), lambda i: (0, 0))
    return pl.pallas_call(
        _residual_ln_body,
        grid=(S // tm,),
        in_specs=[row, row, full, full],
        out_specs=(row, row),
        out_shape=(jax.ShapeDtypeStruct((S, D), jnp.float32), jax.ShapeDtypeStruct((S, D), jnp.bfloat16)),
        compiler_params=pltpu.CompilerParams(dimension_semantics=("parallel",)),
        name="residual_ln",
    )(x, m, ln_g.reshape(1, D), ln_b.reshape(1, D))


def _split(p, sizes):
    idx = [int(i) for i in np.cumsum(sizes)[:-1]]
    return jnp.split(p, idx, axis=-1)


def _layer_norm(x, g, b):
    xf = x.astype(jnp.float32)
    mu = jnp.mean(xf, -1, keepdims=True)
    var = jnp.mean(jnp.square(xf - mu), -1, keepdims=True)
    return ((xf - mu) * lax.rsqrt(var + LN_EPS)).astype(x.dtype) * g + b


def _head_norm(x, eps):
    x = x.astype(jnp.float32)
    mu = jnp.mean(x, -1, keepdims=True)
    var = jnp.mean(jnp.square(x - mu), -1, keepdims=True)
    return (x - mu) * lax.rsqrt(var + eps)


def _token_shift(z, mu):
    prev = jnp.pad(z, ((0, 0), (1, 0), (0, 0)))[:, :-1]
    return z + (prev - z) * mu


def _masked_softmax(s, mask):
    s = jnp.where(mask, s, NEG_INF)
    p = jax.nn.softmax(s, axis=-1)
    return jnp.where(mask, p, 0.0)


RW_CHUNK = 64
LORA_PAD = LANES
RW_OFF_XW = 3 * RW_WIDTH
RW_OFF_XA = RW_OFF_XW + LORA_PAD
RW_OFF_XG = RW_OFF_XA + LORA_PAD
RW_OFF_XV = RW_OFF_XG + RW_G_LORA
RW_PAD_SIZES = (RW_WIDTH, RW_WIDTH, RW_WIDTH, LORA_PAD, LORA_PAD, RW_G_LORA)

_NT = (((1,), (1,)), ((), ()))
_TN = (((0,), (0,)), ((), ()))


def _bf(x):
    return x.astype(jnp.bfloat16)


def _dot(a, b, dims=None):
    if dims is None:
        return jnp.dot(_bf(a), _bf(b), preferred_element_type=jnp.float32)
    return lax.dot_general(_bf(a), _bf(b), dims, preferred_element_type=jnp.float32)


def _split3(x):
    h = _bf(x)
    r = x - h.astype(jnp.float32)
    m = _bf(r)
    l = _bf(r - m.astype(jnp.float32))
    return h, m, l


def _dot_exact_rhs(a, b_bf16, passes):
    acc = None
    for p in _split3(a)[:passes]:
        t = jnp.dot(p, b_bf16, preferred_element_type=jnp.float32)
        acc = t if acc is None else acc + t
    return acc


def _dot_exact_lhs(a_bf16, b, passes):
    acc = None
    for p in _split3(b)[:passes]:
        t = jnp.dot(a_bf16, p, preferred_element_type=jnp.float32)
        acc = t if acc is None else acc + t
    return acc


def _sigmoid(x):
    return 1.0 / (1.0 + jnp.exp(-x))


def _softplus(x):
    return jnp.maximum(x, 0.0) + jnp.log(1.0 + jnp.exp(-jnp.abs(x)))


def _rwkv_body(first, *refs):
    if first:
        (p_ref, mu_ref, w0_ref, wup_ref, a0_ref, aup_ref, gup_ref, kk_ref, ka_ref, rk_ref,
         lg_ref, lb_ref, ones_ref, y_ref, vf_out_ref, st_ref, prev_ref) = refs
    else:
        (p_ref, mu_ref, w0_ref, wup_ref, a0_ref, aup_ref, gup_ref, kk_ref, ka_ref, rk_ref,
         lg_ref, lb_ref, ones_ref, v0_ref, vup_ref, vf_in_ref, y_ref, st_ref, prev_ref) = refs
    C, H, N, W = RW_CHUNK, RW_HEADS, RW_HEAD, RW_WIDTH
    f32 = jnp.float32
    i = pl.program_id(0)

    @pl.when(i == 0)
    def _():
        st_ref[...] = jnp.zeros_like(st_ref)
        prev_ref[...] = jnp.zeros_like(prev_ref)

    z = p_ref[...]
    row = lax.broadcasted_iota(jnp.int32, z.shape, 0)
    prev = jnp.where(row == 0, prev_ref[...], pltpu.roll(z, 1, axis=0))
    prev_ref[...] = z[C - 1:C, :]
    xs = z + (prev - z) * mu_ref[...]

    r = xs[:, 0:W]
    k = xs[:, W:2 * W]
    v = xs[:, 2 * W:3 * W]
    xw = xs[:, RW_OFF_XW:RW_OFF_XW + LORA_PAD]
    xa = xs[:, RW_OFF_XA:RW_OFF_XA + LORA_PAD]
    xg = xs[:, RW_OFF_XG:RW_OFF_XG + RW_G_LORA]
    if first:
        vf_out_ref[...] = v
    else:
        xv = xs[:, RW_OFF_XV:RW_OFF_XV + LORA_PAD]
        v = v + (vf_in_ref[...] - v) * _sigmoid(v0_ref[...] + _dot(xv, vup_ref[...]))

    w_log = -_softplus(-(w0_ref[...] + _dot(jnp.tanh(xw), wup_ref[...]))) - 0.5
    logd = -jnp.exp(w_log)
    a = _sigmoid(a0_ref[...] + _dot(xa, aup_ref[...]))
    g = _dot(_sigmoid(xg), gup_ref[...])
    ones_blk = ones_ref[...]
    kk = k * kk_ref[...]
    kk_n2 = _dot_exact_rhs(kk * kk, ones_blk, 2)
    kk = kk / jnp.maximum(jnp.sqrt(kk_n2), 1e-12)
    k = k * (1.0 + (a - 1.0) * ka_ref[...])
    b = kk * a
    bonus = _dot_exact_rhs(r * k * rk_ref[...], ones_blk, 2) * v

    tr = lax.broadcasted_iota(jnp.int32, (C, C), 0)
    tc = lax.broadcasted_iota(jnp.int32, (C, C), 1)
    low_incl = tr >= tc
    low_strict = tr > tc
    cum = _dot_exact_lhs(low_incl.astype(jnp.bfloat16), logd, 3)
    total = cum[C - 1:C, :]
    e_in = jnp.exp(cum)
    e_ex = jnp.exp(cum - logd)
    e_neg = jnp.exp(-cum)
    e_rem = jnp.exp(total - cum)
    kkd = kk * e_ex
    kh = k * e_neg
    bh = b * e_neg
    rd = r * e_in
    khg = k * e_rem
    bhg = b * e_rem
    gam = jnp.exp(total)

    eye = (tr == tc).astype(f32)
    heads = range(H)
    hs = lambda t: [t[:, h * N:(h + 1) * N] for h in heads]
    kkd_h, kh_h, bh_h, rd_h, v_h, khg_h, bhg_h = (hs(t) for t in (kkd, kh, bh, rd, v, khg, bhg))
    st = [st_ref[h] for h in heads]
    gm = [_dot(jnp.concatenate([kkd_h[h], rd_h[h]], axis=0),
               jnp.concatenate([kh_h[h], bh_h[h]], axis=0), _NT) for h in heads]
    a_k = [jnp.where(low_strict, gm[h][0:C, 0:C], 0.0) for h in heads]
    a_b = [jnp.where(low_strict, gm[h][0:C, C:2 * C], 0.0) for h in heads]
    b_k = [jnp.where(low_incl, gm[h][C:2 * C, 0:C], 0.0) for h in heads]
    b_b = [jnp.where(low_incl, gm[h][C:2 * C, C:2 * C], 0.0) for h in heads]
    rhs = [_dot(kkd_h[h], st[h]) + _dot(a_k[h], v_h[h]) for h in heads]
    y0 = [_dot(rd_h[h], st[h]) + _dot(b_k[h], v_h[h]) for h in heads]
    t_inv = [eye - jnp.where((tr == tc + 1) & (tr % 2 == 1), a_b[h], 0.0) for h in heads]
    m = 2
    while m < C:
        rb, cb = tr // m, tc // m
        sib = (rb == cb + 1) & (rb % 2 == 1)
        tmp = [_dot(jnp.where(sib, a_b[h], 0.0), t_inv[h]) for h in heads]
        t_inv = [t_inv[h] - _dot(t_inv[h], tmp[h]) for h in heads]
        m *= 2
    zz = [_dot(t_inv[h], rhs[h]) for h in heads]
    ys = [y0[h] - _dot(b_b[h], zz[h]) for h in heads]
    upd = [_dot(khg_h[h], v_h[h], _TN) - _dot(bhg_h[h], zz[h], _TN) for h in heads]
    for h in heads:
        st_ref[h] = st[h] * jnp.transpose(gam[:, h * N:(h + 1) * N]) + upd[h]
    y = jnp.concatenate(ys, axis=1)

    inv_n = 1.0 / N
    mean = _dot_exact_rhs(y, ones_blk, 2) * inv_n
    yc = y - mean
    var = _dot_exact_rhs(yc * yc, ones_blk, 2) * inv_n
    yn = yc * lax.rsqrt(var + RW_GN_EPS) * lg_ref[...] + lb_ref[...]
    y_ref[...] = (yn + bonus) * g


def _pad_rows(w, rows):
    return jnp.pad(w, ((0, rows - w.shape[0]), (0, 0)))


def _pad_cols(p, sizes, padded):
    parts = _split(p, sizes)
    return jnp.concatenate([jnp.pad(t, [(0, 0)] * (t.ndim - 1) + [(0, n - t.shape[-1])])
                            for t, n in zip(parts, padded)], -1)


def _rwkv7_time_mix(p, mu, w0, w_up, a0, a_up, g_up, k_k, k_a, r_k, lnx_g, lnx_b, v_first, v0, v_up):
    S, cols = p.shape
    first = v_first is None
    C, W = RW_CHUNK, RW_WIDTH
    row = lambda t: t.reshape(1, -1)
    hid = np.arange(W) // RW_HEAD
    ones_blk = jnp.asarray(hid[:, None] == hid[None, :], jnp.bfloat16)
    full = lambda shape: pl.BlockSpec(shape, lambda i: (0,) * len(shape))
    tok = lambda width: pl.BlockSpec((C, width), lambda i: (i, 0))
    args = [p, row(mu), row(w0), _pad_rows(w_up, LORA_PAD), row(a0), _pad_rows(a_up, LORA_PAD), g_up,
            row(k_k), row(k_a), row(r_k), row(lnx_g), row(lnx_b), ones_blk]
    specs = [tok(cols), full((1, cols)), full((1, W)), full((LORA_PAD, W)), full((1, W)), full((LORA_PAD, W)),
             full((RW_G_LORA, W)), full((1, W)), full((1, W)), full((1, W)), full((1, W)), full((1, W)),
             full((W, W))]
    if first:
        out_shape = (jax.ShapeDtypeStruct((S, W), jnp.float32), jax.ShapeDtypeStruct((S, W), jnp.float32))
        out_specs = (tok(W), tok(W))
    else:
        args += [row(v0), _pad_rows(v_up, LORA_PAD), v_first]
        specs += [full((1, W)), full((LORA_PAD, W)), tok(W)]
        out_shape = jax.ShapeDtypeStruct((S, W), jnp.float32)
        out_specs = tok(W)
    res = pl.pallas_call(
        functools.partial(_rwkv_body, first),
        grid=(S // C,),
        in_specs=specs,
        out_specs=out_specs,
        out_shape=out_shape,
        scratch_shapes=[pltpu.VMEM((RW_HEADS, RW_HEAD, RW_HEAD), jnp.float32),
                        pltpu.VMEM((1, cols), jnp.float32)],
        compiler_params=pltpu.CompilerParams(dimension_semantics=("arbitrary",),
                                             vmem_limit_bytes=48 * 1024 * 1024),
        name="rwkv7_chunked",
    )(*args)
    if first:
        return res[0], res[1]
    return res, v_first


CMP_ROWS = 256


def _compress_body(r_ref, nx_ref, pe_ref, wa_ref, wb_ref, b1_ref, w2_ref, o_ref):
    r = r_ref[...]
    rb, half = r.shape
    row = lax.broadcasted_iota(jnp.int32, r.shape, 0)
    nxt = jnp.where(row == rb - 1, nx_ref[0:1, :], pltpu.roll(r, rb - 1, axis=0))
    pe = pe_ref[...]
    h = (_dot(r + pe[:, :half], wa_ref[...]) + _dot(nxt + pe[:, half:], wb_ref[...]) + b1_ref[...])
    h = 0.5 * h * (1.0 + jnp.tanh(0.7978845608028654 * (h + 0.044715 * (h * h * h))))
    o_ref[...] = _dot(h, w2_ref[...])


def _nsa_compress(t, pe, w1, b1, w2):
    S = t.shape[0]
    G, DH, L, ST = NSA_KV_GROUPS, NSA_HEAD, CMP_LEN, CMP_STRIDE
    nr = S // ST
    rb = min(CMP_ROWS, nr)
    cols = ST * G * DH
    r = t.reshape(nr, cols)
    eye_g = jnp.eye(G, dtype=w1.dtype)
    big = jnp.einsum('ldh,gk->lgdkh', w1.reshape(L, DH, CMP_HIDDEN), eye_g).reshape(L * G * DH, G * CMP_HIDDEN)
    wa, wb = big[:cols], big[cols:]
    pe_big = jnp.broadcast_to(pe[:, None, :], (L, G, DH)).reshape(1, L * G * DH)
    b1_big = jnp.tile(b1, G).reshape(1, G * CMP_HIDDEN)
    w2_big = jnp.einsum('hd,gk->ghkd', w2, eye_g).reshape(G * CMP_HIDDEN, G * DH)
    full = lambda shape: pl.BlockSpec(shape, lambda i: (0, 0))
    return pl.pallas_call(
        _compress_body,
        grid=(nr // rb,),
        in_specs=[pl.BlockSpec((rb, cols), lambda i: (i, 0)),
                  pl.BlockSpec((8, cols), lambda i: (jnp.minimum((i + 1) * (rb // 8), nr // 8 - 1), 0)),
                  full((1, 2 * cols)), full((cols, G * CMP_HIDDEN)), full((cols, G * CMP_HIDDEN)),
                  full((1, G * CMP_HIDDEN)), full((G * CMP_HIDDEN, G * DH))],
        out_specs=pl.BlockSpec((rb, G * DH), lambda i: (i, 0)),
        out_shape=jax.ShapeDtypeStruct((nr, G * DH), jnp.float32),
        compiler_params=pltpu.CompilerParams(dimension_semantics=("parallel",),
                                             vmem_limit_bytes=48 * 1024 * 1024),
        name="nsa_compress",
    )(r, r, pe_big, _bf(wa), _bf(wb), b1_big, _bf(w2_big))


NSA_QB = 128
NSA_KC = 512
NSA_WC = 128
NSA_GATE_PAD = LANES
NSA_PCOLS = NSA_WIDTH + 6 * NSA_KV + NSA_GATE_PAD
NSA_GATE_ROWS = 16
LOG2E = 1.4426950408889634
NSA_HOT = 16


def _softmax_cols(s, mask):
    s = jnp.where(mask, s, NEG_INF)
    m = jnp.max(s, axis=0, keepdims=True)
    p = jnp.where(mask, jnp.exp2(s - m), 0.0)
    l = jnp.sum(p, axis=0, keepdims=True)
    return p * (1.0 / jnp.where(l > 0.0, l, 1.0))


def _nsa_body(*refs):
    n_win = (WINDOW + NSA_QB) // NSA_WC
    qt_ref, kvc_ref, kvct_ref, kq_ref, vst_ref = refs[:5]
    kvw_refs = refs[5:5 + n_win]
    kvwt_refs = refs[5 + n_win:5 + 2 * n_win]
    ovl_ref, gate_ref, o_ref, bias_ref, acc_ref, sca_ref, scb_ref = refs[5 + 2 * n_win:]
    QB, HPG, DH = NSA_QB, NSA_HPG, NSA_HEAD
    f32 = jnp.float32
    qi = pl.program_id(1)
    q0 = qi * QB
    qt = qt_ref[...]
    ncmp = kvc_ref.shape[0]
    t_row = q0 + lax.broadcasted_iota(jnp.int32, (1, QB), 1)
    head = lambda a, h: a[:, h * QB:(h + 1) * QB]

    s = jnp.dot(kvc_ref[...], qt, preferred_element_type=f32)
    c_end = lax.broadcasted_iota(jnp.int32, (ncmp, QB), 0) * CMP_STRIDE + (CMP_LEN - 1)
    m_cmp = c_end <= t_row
    kvct = kvct_ref[...]
    p_sum = None
    o_cmp = []
    for h in range(HPG):
        p = _softmax_cols(head(s, h), m_cmp)
        o_cmp.append(jnp.dot(kvct, _bf(p), preferred_element_type=f32))
        p_sum = p if p_sum is None else p_sum + p
    imp = _dot_exact_lhs(ovl_ref[...], p_sum, 2)
    nsel = imp.shape[0]

    n_row = lax.broadcasted_iota(jnp.int32, (nsel, QB), 0)
    cur = t_row // SEL_BLOCK
    forced = (n_row == 0) | (n_row == cur) | (n_row == cur - 1)
    score = jnp.where(n_row <= cur, imp + jnp.where(forced, FORCE_BONUS, 0.0), NEG_INF)
    bias = jnp.full((nsel, QB), NEG_INF, f32)
    for _ in range(min(SEL_TOPN, nsel)):
        best = jnp.max(score, axis=0, keepdims=True)
        first = jnp.min(jnp.where(score == best, n_row, nsel), axis=0, keepdims=True)
        hit = n_row == first
        bias = jnp.where(hit & (best > 0.5 * NEG_INF), 0.0, bias)
        score = jnp.where(hit, -jnp.inf, score)
    bias_ref[...] = bias

    KC = NSA_KC
    n_sub = KC // SEL_BLOCK
    key_row = lax.broadcasted_iota(jnp.int32, (KC, QB), 0)
    acc_ref[...] = jnp.zeros_like(acc_ref)
    last_chunk = kq_ref.shape[0] // KC - 1
    q_rows = qt[0:DH]
    pad_rows = jnp.zeros((LANES - DH - NSA_HOT, HPG * QB), jnp.bfloat16)

    def chunk_scores(idx):
        c = jnp.minimum(idx, last_chunk)
        b8 = bias_ref[pl.ds(pl.multiple_of(c * n_sub, n_sub), n_sub), :]
        b_rows = jnp.concatenate([b8, jnp.zeros((NSA_HOT - n_sub, QB), f32)], axis=0)
        q_aug = jnp.concatenate([q_rows, _bf(jnp.concatenate([b_rows] * HPG, axis=1)), pad_rows], axis=0)
        k0 = pl.multiple_of(c * KC, KC)
        return jnp.dot(kq_ref[pl.ds(k0, KC), :], q_aug, preferred_element_type=f32)

    def consume(sc_ref, idx, ms, ls, causal):
        c = jnp.minimum(idx, last_chunk)
        vt = vst_ref[:, pl.ds(pl.multiple_of(c * KC, KC), KC)]
        sh = sc_ref[...]
        if causal:
            madd = jnp.where(idx * KC + key_row <= t_row, 0.0, NEG_INF)
            sh = sh + jnp.concatenate([madd] * HPG, axis=1)
        new_m = jnp.maximum(ms, jnp.max(sh, axis=0, keepdims=True))
        alpha = jnp.exp2(ms - new_m)
        p = jnp.exp2(sh - new_m)
        new_l = alpha * ls + jnp.sum(p, axis=0, keepdims=True)
        acc_ref[...] = alpha * acc_ref[...] + jnp.dot(vt, _bf(p), preferred_element_type=f32)
        return new_m, new_l

    sca_ref[...] = chunk_scores(0)

    def pair_step(jj, carry):
        ms, ls = carry
        scb_ref[...] = chunk_scores(2 * jj + 1)
        ms, ls = consume(sca_ref, 2 * jj, ms, ls, False)
        sca_ref[...] = chunk_scores(2 * jj + 2)
        return consume(scb_ref, 2 * jj + 1, ms, ls, False)

    init = (jnp.full((1, HPG * QB), NEG_INF, f32), jnp.zeros((1, HPG * QB), f32))
    n_pairs = (q0 // KC) // 2
    ms, ls = lax.fori_loop(0, n_pairs, pair_step, init)
    scb_ref[...] = chunk_scores(2 * n_pairs + 1)
    ms, ls = consume(sca_ref, 2 * n_pairs, ms, ls, True)
    _, l_sel = consume(scb_ref, 2 * n_pairs + 1, ms, ls, True)

    WC = NSA_WC
    s_win, m_win = [], []
    for c in range(n_win):
        s_pos = q0 - WINDOW + c * WC + lax.broadcasted_iota(jnp.int32, (WC, QB), 0)
        dist = t_row - s_pos
        m_win.append((dist >= 0) & (dist < WINDOW) & (s_pos >= 0))
        s_win.append(jnp.dot(kvw_refs[c][...], qt, preferred_element_type=f32))
    m_all = jnp.concatenate(m_win, axis=0)
    s_all = jnp.concatenate(s_win, axis=0)

    gate = _sigmoid(gate_ref[...])
    outs = []
    for h in range(HPG):
        p = _bf(_softmax_cols(head(s_all, h), m_all))
        o_win = None
        for c in range(n_win):
            t = jnp.dot(kvwt_refs[c][...], p[c * WC:(c + 1) * WC], preferred_element_type=f32)
            o_win = t if o_win is None else o_win + t
        o_sel = head(acc_ref[...], h) * (1.0 / head(l_sel, h))
        gr = lambda b: gate[h * 3 + b:h * 3 + b + 1, :]
        outs.append(gr(0) * o_cmp[h][DH:, :] + gr(1) * o_sel + gr(2) * o_win[DH:, :])
    o_ref[...] = jnp.transpose(jnp.concatenate(outs, axis=0))


def _nsa(p, cmp_pe, cmp_w1, cmp_b1, cmp_w2):
    S = p.shape[0]
    G, HPG, DH, QB, WC = NSA_KV_GROUPS, NSA_HPG, NSA_HEAD, NSA_QB, NSA_WC
    W, KV = NSA_WIDTH, NSA_KV
    nqb = S // QB
    n_win = (WINDOW + QB) // WC
    q = p[:, :W]
    kc, vc, ks, vs, kw, vw = (p[:, W + i * KV:W + (i + 1) * KV] for i in range(6))
    gate = p[:, W + 6 * KV:W + 6 * KV + 3 * NSA_HEADS]
    k_cmp = _nsa_compress(kc, cmp_pe[0], cmp_w1[0], cmp_b1[0], cmp_w2[0])
    v_cmp = _nsa_compress(vc, cmp_pe[1], cmp_w1[1], cmp_b1[1], cmp_w2[1])
    pack = lambda k, v: _bf(jnp.concatenate([k.reshape(-1, G, DH), v.reshape(-1, G, DH)], -1)).transpose(1, 0, 2)
    kvc, kvw = pack(k_cmp, v_cmp), pack(kw, vw)
    hot = jax.nn.one_hot((jnp.arange(S) // SEL_BLOCK) % (NSA_KC // SEL_BLOCK), LANES - DH, dtype=jnp.bfloat16)
    kq = jnp.concatenate([_bf(ks.reshape(S, G, DH)), jnp.broadcast_to(hot[:, None, :], (S, G, LANES - DH))], -1)
    kq = kq.transpose(1, 0, 2)
    vst = _bf(vs.reshape(S, G, DH)).transpose(1, 2, 0)
    tr = lambda a: a.transpose(0, 2, 1)
    qh = q.reshape(nqb, QB, G, HPG, DH) * (DH ** -0.5 * LOG2E)
    qh = _bf(jnp.concatenate([qh, jnp.zeros_like(qh)], -1)).transpose(2, 0, 4, 3, 1).reshape(G, nqb, LANES, HPG * QB)
    gate_t = jnp.pad(gate.reshape(S, G, 3 * HPG).transpose(1, 2, 0), ((0, 0), (0, NSA_GATE_ROWS - 3 * HPG), (0, 0)))
    ncmp, nsel = S // CMP_STRIDE, S // SEL_BLOCK
    c_start = np.arange(ncmp)[None, :] * CMP_STRIDE
    s_start = np.arange(nsel)[:, None] * SEL_BLOCK
    overlap_t = jnp.asarray((c_start < s_start + SEL_BLOCK) & (c_start + CMP_LEN > s_start), jnp.bfloat16)
    win_blk = lambda c: (lambda g, i: (g, jnp.maximum(i - WINDOW // WC + c, 0), 0))
    win_blk_t = lambda c: (lambda g, i: (g, 0, jnp.maximum(i - WINDOW // WC + c, 0)))
    in_specs = ([pl.BlockSpec((None, None, LANES, HPG * QB), lambda g, i: (g, i, 0, 0)),
                 pl.BlockSpec((None, ncmp, LANES), lambda g, i: (g, 0, 0)),
                 pl.BlockSpec((None, LANES, ncmp), lambda g, i: (g, 0, 0)),
                 pl.BlockSpec((None, S, LANES), lambda g, i: (g, 0, 0)),
                 pl.BlockSpec((None, DH, S), lambda g, i: (g, 0, 0))]
                + [pl.BlockSpec((None, WC, LANES), win_blk(c)) for c in range(n_win)]
                + [pl.BlockSpec((None, LANES, WC), win_blk_t(c)) for c in range(n_win)]
                + [pl.BlockSpec((nsel, ncmp), lambda g, i: (0, 0)),
                   pl.BlockSpec((None, NSA_GATE_ROWS, QB), lambda g, i: (g, 0, i))])
    kvwt = tr(kvw)
    return pl.pallas_call(
        _nsa_body,
        grid=(G, nqb),
        in_specs=in_specs,
        out_specs=pl.BlockSpec((QB, HPG * DH), lambda g, i: (i, g)),
        out_shape=jax.ShapeDtypeStruct((S, W), jnp.float32),
        scratch_shapes=[pltpu.VMEM((nsel, QB), jnp.float32),
                        pltpu.VMEM((DH, HPG * QB), jnp.float32),
                        pltpu.VMEM((NSA_KC, HPG * QB), jnp.float32),
                        pltpu.VMEM((NSA_KC, HPG * QB), jnp.float32)],
        compiler_params=pltpu.CompilerParams(dimension_semantics=("arbitrary", "arbitrary"),
                                             vmem_limit_bytes=56 * 1024 * 1024),
        name="nsa_attention",
    )(qh, kvc, tr(kvc), kq, vst, *([kvw] * n_win), *([kvwt] * n_win), overlap_t, gate_t)


def _retention_body(p_ref, cc_ref, ss_ref, inner_ref, qd_ref, kd_ref, cd_ref, o_ref, r_ref):
    H, DK, W = RET_HEADS, RET_HEAD, RET_WIDTH
    f32 = jnp.float32

    @pl.when(pl.program_id(0) == 0)
    def _():
        r_ref[...] = jnp.zeros_like(r_ref)

    cc, ss = cc_ref[...], ss_ref[...]
    hh = range(H)
    col = lambda part, h: p_ref[:, part * W + h * DK:part * W + (h + 1) * DK].astype(f32)
    rot = lambda t: t * cc + pltpu.roll(t, DK // 2, axis=1) * ss
    q = [rot(col(0, h)) for h in hh]
    k = [rot(col(1, h)) * (DK ** -0.5) for h in hh]
    v = [col(2, h) for h in hh]
    r = [r_ref[h] for h in hh]
    att = [_dot(q[h], k[h], _NT) * inner_ref[h] for h in hh]
    o = [_dot(att[h], v[h]) + _dot(q[h], r[h]) * qd_ref[h] for h in hh]
    upd = [_dot(k[h] * kd_ref[h], v[h], _TN) for h in hh]
    for h in hh:
        r_ref[h] = r[h] * cd_ref[h] + upd[h]
        mu = jnp.mean(o[h], axis=-1, keepdims=True)
        oc = o[h] - mu
        var = jnp.mean(oc * oc, axis=-1, keepdims=True)
        g = col(3, h)
        o_ref[:, h * DK:(h + 1) * DK] = (g * _sigmoid(g)) * (oc * lax.rsqrt(var + RET_GN_EPS))


def _retention(p):
    S = p.shape[0]
    H, DK, C = RET_HEADS, RET_HEAD, RET_CHUNK
    f32 = jnp.float32
    inv_freq = 1.0 / (RET_THETA ** jnp.linspace(0.0, 1.0, DK // 2))
    ang = jnp.arange(S, dtype=f32)[:, None] * inv_freq[None, :]
    cos, sin = jnp.cos(ang), jnp.sin(ang)
    cc = jnp.concatenate([cos, cos], -1)
    ss = jnp.concatenate([-sin, sin], -1)
    log_g = jnp.log1p(-jnp.exp2(-5.0 - jnp.arange(H, dtype=f32)))
    i = jnp.arange(C, dtype=f32)
    diff = i[:, None] - i[None, :]
    inner_decay = jnp.where(diff >= 0, jnp.exp(log_g[:, None, None] * jnp.maximum(diff, 0.0)), 0.0)
    lanes = lambda t: jnp.broadcast_to(t, (H, C, DK))
    q_decay = lanes(jnp.exp(log_g[:, None] * (i + 1.0))[..., None])
    k_decay = lanes(jnp.exp(log_g[:, None] * (C - 1.0 - i))[..., None])
    c_decay = lanes(jnp.exp(log_g * C)[:, None, None])
    tok = lambda width: pl.BlockSpec((C, width), lambda n: (n, 0))
    const = pl.BlockSpec((H, C, DK), lambda n: (0, 0, 0))
    return pl.pallas_call(
        _retention_body,
        grid=(S // C,),
        in_specs=[tok(p.shape[1]), tok(DK), tok(DK), const, const, const, const],
        out_specs=tok(RET_WIDTH),
        out_shape=jax.ShapeDtypeStruct((S, RET_WIDTH), f32),
        scratch_shapes=[pltpu.VMEM((H, DK, DK), f32)],
        compiler_params=pltpu.CompilerParams(dimension_semantics=("arbitrary",),
                                             vmem_limit_bytes=48 * 1024 * 1024),
        name="retention_chunked",
    )(p, cc, ss, inner_decay, q_decay, k_decay, c_decay)


MOE_ROWS = 256


def _expert_body(blk_e_ref, n_used_ref, x_ref, wg_ref, wu_ref, wd_ref, o_ref):
    i = pl.program_id(0)

    @pl.when(i < n_used_ref[0])
    def _():
        x = x_ref[...]
        hg = _dot(x, wg_ref[...])
        h = (hg * _sigmoid(hg)) * _dot(x, wu_ref[...])
        o_ref[...] = _dot(h, wd_ref[...])

    @pl.when(i >= n_used_ref[0])
    def _():
        o_ref[...] = jnp.zeros_like(o_ref)


def _moe_experts(x_slots, blk_expert, n_used, w_gate, w_up, w_down):
    n_slots, D = x_slots.shape
    F = w_gate.shape[2]
    mb = MOE_ROWS
    grid_spec = pltpu.PrefetchScalarGridSpec(
        num_scalar_prefetch=2,
        grid=(n_slots // mb,),
        in_specs=[pl.BlockSpec((mb, D), lambda i, be, nu: (i, 0)),
                  pl.BlockSpec((None, D, F), lambda i, be, nu: (be[i], 0, 0)),
                  pl.BlockSpec((None, D, F), lambda i, be, nu: (be[i], 0, 0)),
                  pl.BlockSpec((None, F, D), lambda i, be, nu: (be[i], 0, 0))],
        out_specs=pl.BlockSpec((mb, D), lambda i, be, nu: (i, 0)))
    return pl.pallas_call(
        _expert_body,
        grid_spec=grid_spec,
        out_shape=jax.ShapeDtypeStruct((n_slots, D), jnp.float32),
        compiler_params=pltpu.CompilerParams(dimension_semantics=("arbitrary",),
                                             vmem_limit_bytes=56 * 1024 * 1024),
        name="moe_experts",
    )(blk_expert, n_used, x_slots, w_gate, w_up, w_down)


def _hier_moe(x1b, logits, w_gate, w_up, w_down):
    T, D = x1b.shape
    f32 = jnp.float32
    grp_logits = logits[:, :N_GROUPS]
    grp = jnp.argmax(grp_logits, axis=-1)
    grp_w = jnp.take_along_axis(jax.nn.softmax(grp_logits, -1), grp[:, None], axis=-1)
    exp_logits = logits[:, N_GROUPS:N_GROUPS + N_EXPERTS].reshape(T, N_GROUPS, EXPERTS_PER_GROUP)
    in_grp = jnp.take_along_axis(exp_logits, grp[:, None, None], axis=1)[:, 0]
    e_iota = jnp.arange(EXPERTS_PER_GROUP, dtype=jnp.int32)[None, :]
    i1 = jnp.argmax(in_grp, axis=-1).astype(jnp.int32)
    rest = jnp.where(e_iota == i1[:, None], -jnp.inf, in_grp)
    i2 = jnp.argmax(rest, axis=-1).astype(jnp.int32)
    top_idx = jnp.stack([i1, i2], -1)
    top_val = jnp.stack([jnp.max(in_grp, -1), jnp.max(rest, -1)], -1)
    gate_w = (jax.nn.softmax(top_val, -1) * grp_w).reshape(-1)
    expert = (grp[:, None] * EXPERTS_PER_GROUP + top_idx).reshape(-1).astype(jnp.int32)
    token = jnp.repeat(jnp.arange(T, dtype=jnp.int32), EXPERT_TOPK)
    n_assign = T * EXPERT_TOPK
    mb = MOE_ROWS
    onehot = (expert[:, None] == jnp.arange(N_EXPERTS, dtype=jnp.int32)[None, :]).astype(jnp.int32)
    csum = jnp.cumsum(onehot, axis=0)
    pos = jnp.take_along_axis(csum, expert[:, None], axis=1)[:, 0] - 1
    counts = csum[-1]
    padded = (counts + mb - 1) // mb * mb
    pad_end = jnp.cumsum(padded)
    slot = (pad_end - padded)[expert] + pos
    n_blocks = -(-n_assign // mb) + N_EXPERTS
    n_slots = n_blocks * mb
    slot_tok = jnp.full((n_slots,), T, jnp.int32).at[slot].set(token)
    blk_start = jnp.arange(n_blocks, dtype=jnp.int32) * mb
    blk_expert = jnp.minimum(jnp.sum((pad_end[None, :] <= blk_start[:, None]).astype(jnp.int32), axis=1),
                             N_EXPERTS - 1)
    n_used = (pad_end[-1:] // mb).astype(jnp.int32)
    x_slots = jnp.concatenate([x1b, jnp.zeros((1, D), x1b.dtype)])[slot_tok]
    y = _moe_experts(x_slots, blk_expert, n_used, w_gate, w_up, w_down)
    ys = y[slot].reshape(T, EXPERT_TOPK, D) * gate_w.reshape(T, EXPERT_TOPK, 1)
    return ys[:, 0] + ys[:, 1]


def kernel(x, w_in_first, w_in_deep, rw_mu_first, rw_mu_deep, rw_w0, rw_w_up, rw_a0, rw_a_up, rw_v0, rw_v_up, rw_g_up, rw_k_k, rw_k_a, rw_r_k, rw_lnx_g, rw_lnx_b, nsa_cmp_pe, nsa_cmp_w1, nsa_cmp_b1, nsa_cmp_w2, w_br_rw, w_br_nsa, w_br_ret, w_out, ln1_g, ln1_b, moe_w_grp, moe_b_grp, moe_w_exp, moe_b_exp, moe_w_gate, moe_w_up, moe_w_down, ln2_g, ln2_b):
    B, S, D = x.shape
    assert B == 1
    x = x.reshape(S, D)
    xb = _bf(x)
    v_first = None
    for l in range(DEPTH):
        first = l == 0
        w_in = w_in_first if first else w_in_deep[l - 1]
        mu = rw_mu_first if first else rw_mu_deep[l - 1]
        rw_cols = RW_COLS_FIRST if first else RW_COLS_DEEP
        rw_sizes = RW_BASE_SIZES if first else RW_BASE_SIZES + (RW_V_LORA,)
        rw_padded = RW_PAD_SIZES if first else RW_PAD_SIZES + (LORA_PAD,)
        rw_tn = 512 if first else 768
        tail = -sum(rw_padded) % rw_tn
        mu_p = jnp.pad(_pad_cols(mu, rw_sizes, rw_padded), (0, tail))
        sizes = rw_sizes + (0, NSA_COLS, RET_COLS, 3 * D_MODEL)
        padded = rw_padded + (tail, NSA_PCOLS, RET_COLS, 3 * D_MODEL)
        wt = lax.optimization_barrier(jnp.transpose(w_in))
        rows, r0 = [], 0
        for n, n_pad in zip(sizes, padded):
            rows.append(jnp.pad(wt[r0:r0 + n], ((0, n_pad - n), (0, 0))))
            r0 += n
        wt = lax.optimization_barrier(_bf(jnp.concatenate(rows, 0)))
        w_all = jnp.transpose(wt)
        c0 = sum(rw_padded) + tail
        w_rw = w_all[:, :c0]
        w_nsa = w_all[:, c0:c0 + NSA_PCOLS]
        c0 += NSA_PCOLS
        w_ret = w_all[:, c0:c0 + RET_COLS]
        w_gates = w_all[:, c0 + RET_COLS:]
        p_rw = _matmul(xb, w_rw, tn=rw_tn)
        p_nsa = _matmul(xb, w_nsa, tn=NSA_PCOLS // 3)
        p_ret = _matmul(xb, w_ret, tn=512)
        gates = _matmul(xb, w_gates, tn=512, out_dtype=jnp.bfloat16)
        y_rw, v_first = _rwkv7_time_mix(p_rw, mu_p, rw_w0[l], rw_w_up[l], rw_a0[l],
                                        rw_a_up[l], rw_g_up[l], rw_k_k[l], rw_k_a[l], rw_r_k[l], rw_lnx_g[l],
                                        rw_lnx_b[l], v_first,
                                        None if first else rw_v0[l - 1], None if first else rw_v_up[l - 1])
        y_nsa = _nsa(p_nsa, nsa_cmp_pe[l], nsa_cmp_w1[l], nsa_cmp_b1[l], nsa_cmp_w2[l])
        y_ret = _retention(p_ret)
        w_br = _bf(jnp.stack([w_br_rw[l], w_br_nsa[l], w_br_ret[l]]))
        merged = _merge_branches((y_rw, y_nsa, y_ret), gates, w_br)
        w_router = jnp.pad(jnp.concatenate([moe_w_grp[l], moe_w_exp[l]], 1),
                           ((0, 0), (0, ROUTER_PAD - N_GROUPS - N_EXPERTS)))
        b_router = jnp.pad(jnp.concatenate([moe_b_grp[l], moe_b_exp[l]]),
                           (0, ROUTER_PAD - N_GROUPS - N_EXPERTS)).reshape(1, ROUTER_PAD)
        x1, x1b, logits = _out_ln_router(merged, _bf(w_out[l]), x, ln1_g[l], ln1_b[l], _bf(w_router), b_router)
        moe = _hier_moe(x1b, logits, moe_w_gate[l], moe_w_up[l], moe_w_down[l])
        x, xb = _residual_ln(x1, moe, ln2_g[l], ln2_b[l])
    return x.reshape(B, S, D)
```

```python
import functools

import jax
import jax.numpy as jnp
from jax import lax
import numpy as np
from jax.experimental import pallas as pl
from jax.experimental.pallas import tpu as pltpu

D_MODEL = 2048
DEPTH = 2

RW_HEADS = 16
RW_HEAD = 64
RW_WIDTH = RW_HEADS * RW_HEAD
RW_DECAY_LORA = 96
RW_A_LORA = 96
RW_V_LORA = 64
RW_G_LORA = 256
RW_GN_EPS = 64e-5
RW_BASE_SIZES = (RW_WIDTH, RW_WIDTH, RW_WIDTH, RW_DECAY_LORA, RW_A_LORA, RW_G_LORA)
RW_COLS_FIRST = sum(RW_BASE_SIZES)
RW_COLS_DEEP = RW_COLS_FIRST + RW_V_LORA

NSA_HEADS = 16
NSA_KV_GROUPS = 4
NSA_HPG = NSA_HEADS // NSA_KV_GROUPS
NSA_HEAD = 64
NSA_WIDTH = NSA_HEADS * NSA_HEAD
NSA_KV = NSA_KV_GROUPS * NSA_HEAD
CMP_LEN = 32
CMP_STRIDE = 16
CMP_HIDDEN = 128
SEL_BLOCK = 64
SEL_TOPN = 16
WINDOW = 512
Q_BLOCK = 128
FORCE_BONUS = 1e4
NEG_INF = -1e30
NSA_SIZES = (NSA_WIDTH,) + (NSA_KV,) * 6 + (3 * NSA_HEADS,)
NSA_COLS = sum(NSA_SIZES)

RET_HEADS = 8
RET_HEAD = 128
RET_WIDTH = RET_HEADS * RET_HEAD
RET_CHUNK = 128
RET_THETA = 10000.0
RET_GN_EPS = 1e-5
RET_SIZES = (RET_WIDTH,) * 4
RET_COLS = sum(RET_SIZES)

N_GROUPS = 4
EXPERTS_PER_GROUP = 8
N_EXPERTS = N_GROUPS * EXPERTS_PER_GROUP
EXPERT_FF = 512
EXPERT_TOPK = 2
MOE_BLOCK = 128

DN_ALPHA = (2 * DEPTH) ** 0.25
LN_EPS = 1e-5

LANES = 128


MM_ROWS = 1024


def _matmul_body(a_ref, b_ref, o_ref):
    o_ref[...] = jnp.dot(a_ref[...], b_ref[...], preferred_element_type=jnp.float32).astype(o_ref.dtype)


def _matmul(a, b, tn, out_dtype=jnp.float32):
    m, k = a.shape
    n = b.shape[1]
    tm = min(MM_ROWS, m)
    return pl.pallas_call(
        _matmul_body,
        grid=(m // tm, n // tn),
        in_specs=[pl.BlockSpec((tm, k), lambda i, j: (i, 0)),
                  pl.BlockSpec((k, tn), lambda i, j: (0, j))],
        out_specs=pl.BlockSpec((tm, tn), lambda i, j: (i, j)),
        out_shape=jax.ShapeDtypeStruct((m, n), out_dtype),
        compiler_params=pltpu.CompilerParams(
            dimension_semantics=("parallel", "parallel"),
            vmem_limit_bytes=48 * 1024 * 1024),
        name="dense_matmul",
    )(a, b)


MERGE_ROWS = 512
MERGE_COLS = 512


def _merge_body(y0_ref, y1_ref, y2_ref, g0_ref, g1_ref, g2_ref, w_ref, o_ref):
    acc = None
    for b, (y_ref, g_ref) in enumerate(((y0_ref, g0_ref), (y1_ref, g1_ref), (y2_ref, g2_ref))):
        t = _sigmoid(g_ref[...].astype(jnp.float32)) * _dot(y_ref[...], w_ref[b])
        acc = t if acc is None else acc + t
    o_ref[...] = acc.astype(o_ref.dtype)


def _merge_branches(ys, gates, w_br):
    S, wb = ys[0].shape
    D = w_br.shape[2]
    tm, tn = min(MERGE_ROWS, S), MERGE_COLS
    nj = D // tn
    y_spec = pl.BlockSpec((tm, wb), lambda i, j: (i, 0))
    g_spec = lambda b: pl.BlockSpec((tm, tn), lambda i, j: (i, b * nj + j))
    return pl.pallas_call(
        _merge_body,
        grid=(S // tm, nj),
        in_specs=[y_spec, y_spec, y_spec, g_spec(0), g_spec(1), g_spec(2),
                  pl.BlockSpec((3, wb, tn), lambda i, j: (0, 0, j))],
        out_specs=pl.BlockSpec((tm, tn), lambda i, j: (i, j)),
        out_shape=jax.ShapeDtypeStruct((S, D), jnp.bfloat16),
        compiler_params=pltpu.CompilerParams(dimension_semantics=("parallel", "parallel"),
                                             vmem_limit_bytes=48 * 1024 * 1024),
        name="merge_branches",
    )(*ys, gates, gates, gates, w_br)


LN_ROWS = 256
ROUTER_PAD = LANES


def _ln_rows(z, g, b):
    mu = jnp.mean(z, axis=-1, keepdims=True)
    zc = z - mu
    var = jnp.mean(zc * zc, axis=-1, keepdims=True)
    return zc * lax.rsqrt(var + LN_EPS) * g + b


def _out_ln_body(m_ref, w_ref, x_ref, g_ref, b_ref, wr_ref, br_ref, x1_ref, x1b_ref, lg_ref):
    z = DN_ALPHA * x_ref[...] + jnp.dot(m_ref[...], w_ref[...], preferred_element_type=jnp.float32)
    x1 = _ln_rows(z, g_ref[...], b_ref[...])
    x1_ref[...] = x1
    x1b = _bf(x1)
    x1b_ref[...] = x1b
    lg_ref[...] = jnp.dot(x1b, wr_ref[...], preferred_element_type=jnp.float32) + br_ref[...]


def _out_ln_router(merged, w_out, x, ln_g, ln_b, w_router, b_router):
    S, D = x.shape
    tm = min(LN_ROWS, S)
    row = pl.BlockSpec((tm, D), lambda i: (i, 0))
    full = lambda shape: pl.BlockSpec(shape, lambda i: (0, 0))
    return pl.pallas_call(
        _out_ln_body,
        grid=(S // tm,),
        in_specs=[row, full((D, D)), row, full((1, D)), full((1, D)), full((D, ROUTER_PAD)), full((1, ROUTER_PAD))],
        out_specs=(row, row, pl.BlockSpec((tm, ROUTER_PAD), lambda i: (i, 0))),
        out_shape=(jax.ShapeDtypeStruct((S, D), jnp.float32), jax.ShapeDtypeStruct((S, D), jnp.bfloat16),
                   jax.ShapeDtypeStruct((S, ROUTER_PAD), jnp.float32)),
        compiler_params=pltpu.CompilerParams(dimension_semantics=("parallel",),
                                             vmem_limit_bytes=48 * 1024 * 1024),
        name="out_proj_ln_router",
    )(merged, w_out, x, ln_g.reshape(1, D), ln_b.reshape(1, D), w_router, b_router)


def _residual_ln_body(x_ref, m_ref, g_ref, b_ref, o_ref, ob_ref):
    y = _ln_rows(DN_ALPHA * x_ref[...] + m_ref[...], g_ref[...], b_ref[...])
    o_ref[...] = y
    ob_ref[...] = _bf(y)


def _residual_ln(x, m, ln_g, ln_b):
    S, D = x.shape
    tm = min(LN_ROWS, S)
    row = pl.BlockSpec((tm, D), lambda i: (i, 0))
    full = pl.BlockSpec((1, D), lambda i: (0, 0))
    return pl.pallas_call(
        _residual_ln_body,
        grid=(S // tm,),
        in_specs=[row, row, full, full],
        out_specs=(row, row),
        out_shape=(jax.ShapeDtypeStruct((S, D), jnp.float32), jax.ShapeDtypeStruct((S, D), jnp.bfloat16)),
        compiler_params=pltpu.CompilerParams(dimension_semantics=("parallel",)),
        name="residual_ln",
    )(x, m, ln_g.reshape(1, D), ln_b.reshape(1, D))


def _split(p, sizes):
    idx = [int(i) for i in np.cumsum(sizes)[:-1]]
    return jnp.split(p, idx, axis=-1)


def _layer_norm(x, g, b):
    xf = x.astype(jnp.float32)
    mu = jnp.mean(xf, -1, keepdims=True)
    var = jnp.mean(jnp.square(xf - mu), -1, keepdims=True)
    return ((xf - mu) * lax.rsqrt(var + LN_EPS)).astype(x.dtype) * g + b


def _head_norm(x, eps):
    x = x.astype(jnp.float32)
    mu = jnp.mean(x, -1, keepdims=True)
    var = jnp.mean(jnp.square(x - mu), -1, keepdims=True)
    return (x - mu) * lax.rsqrt(var + eps)


def _token_shift(z, mu):
    prev = jnp.pad(z, ((0, 0), (1, 0), (0, 0)))[:, :-1]
    return z + (prev - z) * mu


def _masked_softmax(s, mask):
    s = jnp.where(mask, s, NEG_INF)
    p = jax.nn.softmax(s, axis=-1)
    return jnp.where(mask, p, 0.0)


RW_CHUNK = 64
LORA_PAD = LANES
RW_OFF_XW = 3 * RW_WIDTH
RW_OFF_XA = RW_OFF_XW + LORA_PAD
RW_OFF_XG = RW_OFF_XA + LORA_PAD
RW_OFF_XV = RW_OFF_XG + RW_G_LORA
RW_PAD_SIZES = (RW_WIDTH, RW_WIDTH, RW_WIDTH, LORA_PAD, LORA_PAD, RW_G_LORA)

_NT = (((1,), (1,)), ((), ()))
_TN = (((0,), (0,)), ((), ()))


def _bf(x):
    return x.astype(jnp.bfloat16)


def _dot(a, b, dims=None):
    if dims is None:
        return jnp.dot(_bf(a), _bf(b), preferred_element_type=jnp.float32)
    return lax.dot_general(_bf(a), _bf(b), dims, preferred_element_type=jnp.float32)


def _split3(x):
    h = _bf(x)
    r = x - h.astype(jnp.float32)
    m = _bf(r)
    l = _bf(r - m.astype(jnp.float32))
    return h, m, l


def _dot_exact_rhs(a, b_bf16, passes):
    acc = None
    for p in _split3(a)[:passes]:
        t = jnp.dot(p, b_bf16, preferred_element_type=jnp.float32)
        acc = t if acc is None else acc + t
    return acc


def _dot_exact_lhs(a_bf16, b, passes):
    acc = None
    for p in _split3(b)[:passes]:
        t = jnp.dot(a_bf16, p, preferred_element_type=jnp.float32)
        acc = t if acc is None else acc + t
    return acc


def _sigmoid(x):
    return 1.0 / (1.0 + jnp.exp(-x))


def _softplus(x):
    return jnp.maximum(x, 0.0) + jnp.log(1.0 + jnp.exp(-jnp.abs(x)))


def _rwkv_body(first, *refs):
    if first:
        (p_ref, mu_ref, w0_ref, wup_ref, a0_ref, aup_ref, gup_ref, kk_ref, ka_ref, rk_ref,
         lg_ref, lb_ref, ones_ref, y_ref, vf_out_ref, st_ref, prev_ref) = refs
    else:
        (p_ref, mu_ref, w0_ref, wup_ref, a0_ref, aup_ref, gup_ref, kk_ref, ka_ref, rk_ref,
         lg_ref, lb_ref, ones_ref, v0_ref, vup_ref, vf_in_ref, y_ref, st_ref, prev_ref) = refs
    C, H, N, W = RW_CHUNK, RW_HEADS, RW_HEAD, RW_WIDTH
    f32 = jnp.float32
    i = pl.program_id(0)

    @pl.when(i == 0)
    def _():
        st_ref[...] = jnp.zeros_like(st_ref)
        prev_ref[...] = jnp.zeros_like(prev_ref)

    z = p_ref[...]
    row = lax.broadcasted_iota(jnp.int32, z.shape, 0)
    prev = jnp.where(row == 0, prev_ref[...], pltpu.roll(z, 1, axis=0))
    prev_ref[...] = z[C - 1:C, :]
    xs = z + (prev - z) * mu_ref[...]

    r = xs[:, 0:W]
    k = xs[:, W:2 * W]
    v = xs[:, 2 * W:3 * W]
    xw = xs[:, RW_OFF_XW:RW_OFF_XW + LORA_PAD]
    xa = xs[:, RW_OFF_XA:RW_OFF_XA + LORA_PAD]
    xg = xs[:, RW_OFF_XG:RW_OFF_XG + RW_G_LORA]
    if first:
        vf_out_ref[...] = v
    else:
        xv = xs[:, RW_OFF_XV:RW_OFF_XV + LORA_PAD]
        v = v + (vf_in_ref[...] - v) * _sigmoid(v0_ref[...] + _dot(xv, vup_ref[...]))

    w_log = -_softplus(-(w0_ref[...] + _dot(jnp.tanh(xw), wup_ref[...]))) - 0.5
    logd = -jnp.exp(w_log)
    a = _sigmoid(a0_ref[...] + _dot(xa, aup_ref[...]))
    g = _dot(_sigmoid(xg), gup_ref[...])
    hsum = ones_ref[...]

    def head_sums(t):
        acc = None
        for piece in _split3(_dot_exact_rhs(t, hsum, 2)):
            u = lax.dot_general(piece, hsum, _NT, preferred_element_type=f32)
            acc = u if acc is None else acc + u
        return acc

    kk = k * kk_ref[...]
    kk_n2 = head_sums(kk * kk)
    kk = kk / jnp.maximum(jnp.sqrt(kk_n2), 1e-12)
    k = k * (1.0 + (a - 1.0) * ka_ref[...])
    b = kk * a
    bonus = head_sums(r * k * rk_ref[...]) * v

    tr = lax.broadcasted_iota(jnp.int32, (C, C), 0)
    tc = lax.broadcasted_iota(jnp.int32, (C, C), 1)
    low_incl = tr >= tc
    low_strict = tr > tc
    cum = _dot_exact_lhs(low_incl.astype(jnp.bfloat16), logd, 3)
    total = cum[C - 1:C, :]
    e_in = jnp.exp(cum)
    e_ex = jnp.exp(cum - logd)
    e_neg = jnp.exp(-cum)
    e_rem = jnp.exp(total - cum)
    kkd = kk * e_ex
    kh = k * e_neg
    bh = b * e_neg
    rd = r * e_in
    khg = k * e_rem
    bhg = b * e_rem
    gam = jnp.exp(total)

    eye = (tr == tc).astype(f32)
    heads = range(H)
    hs = lambda t: [t[:, h * N:(h + 1) * N] for h in heads]
    kkd_h, kh_h, bh_h, rd_h, v_h, khg_h, bhg_h = (hs(t) for t in (kkd, kh, bh, rd, v, khg, bhg))
    st = [st_ref[h] for h in heads]
    gm = [_dot(jnp.concatenate([kkd_h[h], rd_h[h]], axis=0),
               jnp.concatenate([kh_h[h], bh_h[h]], axis=0), _NT) for h in heads]
    a_k = [jnp.where(low_strict, gm[h][0:C, 0:C], 0.0) for h in heads]
    a_b = [jnp.where(low_strict, gm[h][0:C, C:2 * C], 0.0) for h in heads]
    b_k = [jnp.where(low_incl, gm[h][C:2 * C, 0:C], 0.0) for h in heads]
    b_b = [jnp.where(low_incl, gm[h][C:2 * C, C:2 * C], 0.0) for h in heads]
    rhs = [_dot(kkd_h[h], st[h]) + _dot(a_k[h], v_h[h]) for h in heads]
    y0 = [_dot(rd_h[h], st[h]) + _dot(b_k[h], v_h[h]) for h in heads]
    t_inv = [eye - jnp.where((tr == tc + 1) & (tr % 2 == 1), a_b[h], 0.0) for h in heads]
    m = 2
    while m < C:
        rb, cb = tr // m, tc // m
        sib = (rb == cb + 1) & (rb % 2 == 1)
        tmp = [_dot(jnp.where(sib, a_b[h], 0.0), t_inv[h]) for h in heads]
        t_inv = [t_inv[h] - _dot(t_inv[h], tmp[h]) for h in heads]
        m *= 2
    zz = [_dot(t_inv[h], rhs[h]) for h in heads]
    ys = [y0[h] - _dot(b_b[h], zz[h]) for h in heads]
    upd = [_dot(khg_h[h], v_h[h], _TN) - _dot(bhg_h[h], zz[h], _TN) for h in heads]
    for h in heads:
        st_ref[h] = st[h] * jnp.transpose(gam[:, h * N:(h + 1) * N]) + upd[h]
    y = jnp.concatenate(ys, axis=1)

    inv_n = 1.0 / N
    mean = head_sums(y) * inv_n
    yc = y - mean
    var = head_sums(yc * yc) * inv_n
    yn = yc * lax.rsqrt(var + RW_GN_EPS) * lg_ref[...] + lb_ref[...]
    y_ref[...] = (yn + bonus) * g


def _pad_rows(w, rows):
    return jnp.pad(w, ((0, rows - w.shape[0]), (0, 0)))


def _pad_cols(p, sizes, padded):
    parts = _split(p, sizes)
    return jnp.concatenate([jnp.pad(t, [(0, 0)] * (t.ndim - 1) + [(0, n - t.shape[-1])])
                            for t, n in zip(parts, padded)], -1)


def _rwkv7_time_mix(p, mu, w0, w_up, a0, a_up, g_up, k_k, k_a, r_k, lnx_g, lnx_b, v_first, v0, v_up):
    S, cols = p.shape
    first = v_first is None
    C, W = RW_CHUNK, RW_WIDTH
    row = lambda t: t.reshape(1, -1)
    hid = np.arange(W) // RW_HEAD
    ones_blk = jnp.asarray(hid[:, None] == np.arange(LANES)[None, :], jnp.bfloat16)
    full = lambda shape: pl.BlockSpec(shape, lambda i: (0,) * len(shape))
    tok = lambda width: pl.BlockSpec((C, width), lambda i: (i, 0))
    args = [p, row(mu), row(w0), _pad_rows(w_up, LORA_PAD), row(a0), _pad_rows(a_up, LORA_PAD), g_up,
            row(k_k), row(k_a), row(r_k), row(lnx_g), row(lnx_b), ones_blk]
    specs = [tok(cols), full((1, cols)), full((1, W)), full((LORA_PAD, W)), full((1, W)), full((LORA_PAD, W)),
             full((RW_G_LORA, W)), full((1, W)), full((1, W)), full((1, W)), full((1, W)), full((1, W)),
             full((W, LANES))]
    if first:
        out_shape = (jax.ShapeDtypeStruct((S, W), jnp.float32), jax.ShapeDtypeStruct((S, W), jnp.float32))
        out_specs = (tok(W), tok(W))
    else:
        args += [row(v0), _pad_rows(v_up, LORA_PAD), v_first]
        specs += [full((1, W)), full((LORA_PAD, W)), tok(W)]
        out_shape = jax.ShapeDtypeStruct((S, W), jnp.float32)
        out_specs = tok(W)
    res = pl.pallas_call(
        functools.partial(_rwkv_body, first),
        grid=(S // C,),
        in_specs=specs,
        out_specs=out_specs,
        out_shape=out_shape,
        scratch_shapes=[pltpu.VMEM((RW_HEADS, RW_HEAD, RW_HEAD), jnp.float32),
                        pltpu.VMEM((1, cols), jnp.float32)],
        compiler_params=pltpu.CompilerParams(dimension_semantics=("arbitrary",),
                                             vmem_limit_bytes=48 * 1024 * 1024),
        name="rwkv7_chunked",
    )(*args)
    if first:
        return res[0], res[1]
    return res, v_first


CMP_ROWS = 256


def _compress_body(r_ref, nx_ref, pe_ref, wa_ref, wb_ref, b1_ref, w2_ref, o_ref):
    r = r_ref[...]
    rb, half = r.shape
    row = lax.broadcasted_iota(jnp.int32, r.shape, 0)
    nxt = jnp.where(row == rb - 1, nx_ref[0:1, :], pltpu.roll(r, rb - 1, axis=0))
    pe = pe_ref[...]
    h = (_dot(r + pe[:, :half], wa_ref[...]) + _dot(nxt + pe[:, half:], wb_ref[...]) + b1_ref[...])
    h = 0.5 * h * (1.0 + jnp.tanh(0.7978845608028654 * (h + 0.044715 * (h * h * h))))
    o_ref[...] = _dot(h, w2_ref[...])


def _nsa_compress(t, pe, w1, b1, w2):
    S = t.shape[0]
    G, DH, L, ST = NSA_KV_GROUPS, NSA_HEAD, CMP_LEN, CMP_STRIDE
    nr = S // ST
    rb = min(CMP_ROWS, nr)
    cols = ST * G * DH
    r = t.reshape(nr, cols)
    eye_g = jnp.eye(G, dtype=w1.dtype)
    big = jnp.einsum('ldh,gk->lgdkh', w1.reshape(L, DH, CMP_HIDDEN), eye_g).reshape(L * G * DH, G * CMP_HIDDEN)
    wa, wb = big[:cols], big[cols:]
    pe_big = jnp.broadcast_to(pe[:, None, :], (L, G, DH)).reshape(1, L * G * DH)
    b1_big = jnp.tile(b1, G).reshape(1, G * CMP_HIDDEN)
    w2_big = jnp.einsum('hd,gk->ghkd', w2, eye_g).reshape(G * CMP_HIDDEN, G * DH)
    full = lambda shape: pl.BlockSpec(shape, lambda i: (0, 0))
    return pl.pallas_call(
        _compress_body,
        grid=(nr // rb,),
        in_specs=[pl.BlockSpec((rb, cols), lambda i: (i, 0)),
                  pl.BlockSpec((8, cols), lambda i: (jnp.minimum((i + 1) * (rb // 8), nr // 8 - 1), 0)),
                  full((1, 2 * cols)), full((cols, G * CMP_HIDDEN)), full((cols, G * CMP_HIDDEN)),
                  full((1, G * CMP_HIDDEN)), full((G * CMP_HIDDEN, G * DH))],
        out_specs=pl.BlockSpec((rb, G * DH), lambda i: (i, 0)),
        out_shape=jax.ShapeDtypeStruct((nr, G * DH), jnp.float32),
        compiler_params=pltpu.CompilerParams(dimension_semantics=("parallel",),
                                             vmem_limit_bytes=48 * 1024 * 1024),
        name="nsa_compress",
    )(r, r, pe_big, _bf(wa), _bf(wb), b1_big, _bf(w2_big))


NSA_QB = 128
NSA_KC = 512
NSA_WC = 128
NSA_GATE_PAD = LANES
NSA_PCOLS = NSA_WIDTH + 6 * NSA_KV + NSA_GATE_PAD
NSA_GATE_ROWS = 16
LOG2E = 1.4426950408889634
NSA_HOT = 16


def _softmax_cols(s, mask):
    s = jnp.where(mask, s, NEG_INF)
    m = jnp.max(s, axis=0, keepdims=True)
    p = jnp.where(mask, jnp.exp2(s - m), 0.0)
    l = jnp.sum(p, axis=0, keepdims=True)
    return p * (1.0 / jnp.where(l > 0.0, l, 1.0))


def _nsa_body(*refs):
    n_win = (WINDOW + NSA_QB) // NSA_WC
    qt_ref, kvc_ref, kvct_ref, kq_ref, vst_ref = refs[:5]
    kvw_refs = refs[5:5 + n_win]
    kvwt_refs = refs[5 + n_win:5 + 2 * n_win]
    ovl_ref, gate_ref, o_ref, bias_ref, acc_ref, sca_ref, scb_ref = refs[5 + 2 * n_win:]
    QB, HPG, DH = NSA_QB, NSA_HPG, NSA_HEAD
    f32 = jnp.float32
    qi = pl.program_id(1)
    q0 = qi * QB
    qt = qt_ref[...]
    ncmp = kvc_ref.shape[0]
    t_row = q0 + lax.broadcasted_iota(jnp.int32, (1, QB), 1)
    head = lambda a, h: a[:, h * QB:(h + 1) * QB]

    s = jnp.dot(kvc_ref[...], qt, preferred_element_type=f32)
    c_end = lax.broadcasted_iota(jnp.int32, (ncmp, QB), 0) * CMP_STRIDE + (CMP_LEN - 1)
    m_cmp = c_end <= t_row
    kvct = kvct_ref[...]
    p_sum = None
    o_cmp = []
    for h in range(HPG):
        p = _softmax_cols(head(s, h), m_cmp)
        o_cmp.append(jnp.dot(kvct, _bf(p), preferred_element_type=f32))
        p_sum = p if p_sum is None else p_sum + p
    imp = _dot_exact_lhs(ovl_ref[...], p_sum, 2)
    nsel = imp.shape[0]

    n_row = lax.broadcasted_iota(jnp.int32, (nsel, QB), 0)
    cur = t_row // SEL_BLOCK
    causal = n_row <= cur
    forced = (n_row == 0) | (n_row == cur) | (n_row == cur - 1)
    score = jnp.where(causal & jnp.logical_not(forced), imp, NEG_INF)
    bias = jnp.where(causal & forced, 0.0, NEG_INF)
    for _ in range(min(SEL_TOPN, nsel) - 3):
        best = jnp.max(score, axis=0, keepdims=True)
        first = jnp.min(jnp.where(score == best, n_row, nsel), axis=0, keepdims=True)
        hit = n_row == first
        bias = jnp.where(hit & (best > 0.5 * NEG_INF), 0.0, bias)
        score = jnp.where(hit, -jnp.inf, score)
    bias_ref[...] = bias

    KC = NSA_KC
    n_sub = KC // SEL_BLOCK
    key_row = lax.broadcasted_iota(jnp.int32, (KC, QB), 0)
    acc_ref[...] = jnp.zeros_like(acc_ref)
    last_chunk = kq_ref.shape[0] // KC - 1
    q_rows = qt[0:DH]
    pad_rows = jnp.zeros((LANES - DH - NSA_HOT, HPG * QB), jnp.bfloat16)

    def chunk_scores(idx):
        c = jnp.minimum(idx, last_chunk)
        b8 = bias_ref[pl.ds(pl.multiple_of(c * n_sub, n_sub), n_sub), :]
        b_rows = jnp.concatenate([b8, jnp.zeros((NSA_HOT - n_sub, QB), f32)], axis=0)
        q_aug = jnp.concatenate([q_rows, _bf(jnp.concatenate([b_rows] * HPG, axis=1)), pad_rows], axis=0)
        k0 = pl.multiple_of(c * KC, KC)
        return jnp.dot(kq_ref[pl.ds(k0, KC), :], q_aug, preferred_element_type=f32)

    def consume(sc_ref, idx, ms, ls, causal):
        c = jnp.minimum(idx, last_chunk)
        vt = vst_ref[:, pl.ds(pl.multiple_of(c * KC, KC), KC)]
        sh = sc_ref[...]
        if causal:
            madd = jnp.where(idx * KC + key_row <= t_row, 0.0, NEG_INF)
            sh = sh + jnp.concatenate([madd] * HPG, axis=1)
        new_m = jnp.maximum(ms, jnp.max(sh, axis=0, keepdims=True))
        alpha = jnp.exp2(ms - new_m)
        p = jnp.exp2(sh - new_m)
        new_l = alpha * ls + jnp.sum(p, axis=0, keepdims=True)
        acc_ref[...] = alpha * acc_ref[...] + jnp.dot(vt, _bf(p), preferred_element_type=f32)
        return new_m, new_l

    sca_ref[...] = chunk_scores(0)

    def pair_step(jj, carry):
        ms, ls = carry
        scb_ref[...] = chunk_scores(2 * jj + 1)
        ms, ls = consume(sca_ref, 2 * jj, ms, ls, False)
        sca_ref[...] = chunk_scores(2 * jj + 2)
        return consume(scb_ref, 2 * jj + 1, ms, ls, False)

    init = (jnp.full((1, HPG * QB), NEG_INF, f32), jnp.zeros((1, HPG * QB), f32))
    n_pairs = (q0 // KC) // 2
    ms, ls = lax.fori_loop(0, n_pairs, pair_step, init)
    scb_ref[...] = chunk_scores(2 * n_pairs + 1)
    ms, ls = consume(sca_ref, 2 * n_pairs, ms, ls, True)
    _, l_sel = consume(scb_ref, 2 * n_pairs + 1, ms, ls, True)

    WC = NSA_WC
    s_win, m_win = [], []
    for c in range(n_win):
        s_pos = q0 - WINDOW + c * WC + lax.broadcasted_iota(jnp.int32, (WC, QB), 0)
        dist = t_row - s_pos
        m_win.append((dist >= 0) & (dist < WINDOW) & (s_pos >= 0))
        s_win.append(jnp.dot(kvw_refs[c][...], qt, preferred_element_type=f32))
    m_all = jnp.concatenate(m_win, axis=0)
    s_all = jnp.concatenate(s_win, axis=0)

    gate = _sigmoid(gate_ref[...])
    outs = []
    for h in range(HPG):
        p = _bf(_softmax_cols(head(s_all, h), m_all))
        o_win = None
        for c in range(n_win):
            t = jnp.dot(kvwt_refs[c][...], p[c * WC:(c + 1) * WC], preferred_element_type=f32)
            o_win = t if o_win is None else o_win + t
        o_sel = head(acc_ref[...], h) * (1.0 / head(l_sel, h))
        gr = lambda b: gate[h * 3 + b:h * 3 + b + 1, :]
        outs.append(gr(0) * o_cmp[h][DH:, :] + gr(1) * o_sel + gr(2) * o_win[DH:, :])
    o_ref[...] = jnp.transpose(jnp.concatenate(outs, axis=0))


def _nsa(p, cmp_pe, cmp_w1, cmp_b1, cmp_w2):
    S = p.shape[0]
    G, HPG, DH, QB, WC = NSA_KV_GROUPS, NSA_HPG, NSA_HEAD, NSA_QB, NSA_WC
    W, KV = NSA_WIDTH, NSA_KV
    nqb = S // QB
    n_win = (WINDOW + QB) // WC
    q = p[:, :W]
    kc, vc, ks, vs, kw, vw = (p[:, W + i * KV:W + (i + 1) * KV] for i in range(6))
    gate = p[:, W + 6 * KV:W + 6 * KV + 3 * NSA_HEADS]
    k_cmp = _nsa_compress(kc, cmp_pe[0], cmp_w1[0], cmp_b1[0], cmp_w2[0])
    v_cmp = _nsa_compress(vc, cmp_pe[1], cmp_w1[1], cmp_b1[1], cmp_w2[1])
    pack = lambda k, v: _bf(jnp.concatenate([k.reshape(-1, G, DH), v.reshape(-1, G, DH)], -1)).transpose(1, 0, 2)
    kvc, kvw = pack(k_cmp, v_cmp), pack(kw, vw)
    hot = jax.nn.one_hot((jnp.arange(S) // SEL_BLOCK) % (NSA_KC // SEL_BLOCK), LANES - DH, dtype=jnp.bfloat16)
    kq = jnp.concatenate([_bf(ks.reshape(S, G, DH)), jnp.broadcast_to(hot[:, None, :], (S, G, LANES - DH))], -1)
    kq = kq.transpose(1, 0, 2)
    vst = _bf(vs.reshape(S, G, DH)).transpose(1, 2, 0)
    tr = lambda a: a.transpose(0, 2, 1)
    qh = q.reshape(nqb, QB, G, HPG, DH) * (DH ** -0.5 * LOG2E)
    qh = _bf(jnp.concatenate([qh, jnp.zeros_like(qh)], -1)).transpose(2, 0, 4, 3, 1).reshape(G, nqb, LANES, HPG * QB)
    gate_t = jnp.pad(gate.reshape(S, G, 3 * HPG).transpose(1, 2, 0), ((0, 0), (0, NSA_GATE_ROWS - 3 * HPG), (0, 0)))
    ncmp, nsel = S // CMP_STRIDE, S // SEL_BLOCK
    c_start = np.arange(ncmp)[None, :] * CMP_STRIDE
    s_start = np.arange(nsel)[:, None] * SEL_BLOCK
    overlap_t = jnp.asarray((c_start < s_start + SEL_BLOCK) & (c_start + CMP_LEN > s_start), jnp.bfloat16)
    win_blk = lambda c: (lambda g, i: (g, jnp.maximum(i - WINDOW // WC + c, 0), 0))
    win_blk_t = lambda c: (lambda g, i: (g, 0, jnp.maximum(i - WINDOW // WC + c, 0)))
    in_specs = ([pl.BlockSpec((None, None, LANES, HPG * QB), lambda g, i: (g, i, 0, 0)),
                 pl.BlockSpec((None, ncmp, LANES), lambda g, i: (g, 0, 0)),
                 pl.BlockSpec((None, LANES, ncmp), lambda g, i: (g, 0, 0)),
                 pl.BlockSpec((None, S, LANES), lambda g, i: (g, 0, 0)),
                 pl.BlockSpec((None, DH, S), lambda g, i: (g, 0, 0))]
                + [pl.BlockSpec((None, WC, LANES), win_blk(c)) for c in range(n_win)]
                + [pl.BlockSpec((None, LANES, WC), win_blk_t(c)) for c in range(n_win)]
                + [pl.BlockSpec((nsel, ncmp), lambda g, i: (0, 0)),
                   pl.BlockSpec((None, NSA_GATE_ROWS, QB), lambda g, i: (g, 0, i))])
    kvwt = tr(kvw)
    return pl.pallas_call(
        _nsa_body,
        grid=(G, nqb),
        in_specs=in_specs,
        out_specs=pl.BlockSpec((QB, HPG * DH), lambda g, i: (i, g)),
        out_shape=jax.ShapeDtypeStruct((S, W), jnp.float32),
        scratch_shapes=[pltpu.VMEM((nsel, QB), jnp.float32),
                        pltpu.VMEM((DH, HPG * QB), jnp.float32),
                        pltpu.VMEM((NSA_KC, HPG * QB), jnp.float32),
                        pltpu.VMEM((NSA_KC, HPG * QB), jnp.float32)],
        compiler_params=pltpu.CompilerParams(dimension_semantics=("arbitrary", "arbitrary"),
                                             vmem_limit_bytes=56 * 1024 * 1024),
        name="nsa_attention",
    )(qh, kvc, tr(kvc), kq, vst, *([kvw] * n_win), *([kvwt] * n_win), overlap_t, gate_t)


def _retention_body(p_ref, cc_ref, ss_ref, inner_ref, qd_ref, kd_ref, cd_ref, o_ref, r_ref):
    H, DK, W = RET_HEADS, RET_HEAD, RET_WIDTH
    f32 = jnp.float32

    @pl.when(pl.program_id(0) == 0)
    def _():
        r_ref[...] = jnp.zeros_like(r_ref)

    cc, ss = cc_ref[...], ss_ref[...]
    hh = range(H)
    col = lambda part, h: p_ref[:, part * W + h * DK:part * W + (h + 1) * DK].astype(f32)
    rot = lambda t: t * cc + pltpu.roll(t, DK // 2, axis=1) * ss
    q = [rot(col(0, h)) for h in hh]
    k = [rot(col(1, h)) * (DK ** -0.5) for h in hh]
    v = [col(2, h) for h in hh]
    r = [r_ref[h] for h in hh]
    att = [_dot(q[h], k[h], _NT) * inner_ref[h] for h in hh]
    o = [_dot(att[h], v[h]) + _dot(q[h], r[h]) * qd_ref[h] for h in hh]
    upd = [_dot(k[h] * kd_ref[h], v[h], _TN) for h in hh]
    for h in hh:
        r_ref[h] = r[h] * cd_ref[h] + upd[h]
        mu = jnp.mean(o[h], axis=-1, keepdims=True)
        oc = o[h] - mu
        var = jnp.mean(oc * oc, axis=-1, keepdims=True)
        g = col(3, h)
        o_ref[:, h * DK:(h + 1) * DK] = (g * _sigmoid(g)) * (oc * lax.rsqrt(var + RET_GN_EPS))


def _retention(p):
    S = p.shape[0]
    H, DK, C = RET_HEADS, RET_HEAD, RET_CHUNK
    f32 = jnp.float32
    inv_freq = 1.0 / (RET_THETA ** jnp.linspace(0.0, 1.0, DK // 2))
    ang = jnp.arange(S, dtype=f32)[:, None] * inv_freq[None, :]
    cos, sin = jnp.cos(ang), jnp.sin(ang)
    cc = jnp.concatenate([cos, cos], -1)
    ss = jnp.concatenate([-sin, sin], -1)
    log_g = jnp.log1p(-jnp.exp2(-5.0 - jnp.arange(H, dtype=f32)))
    i = jnp.arange(C, dtype=f32)
    diff = i[:, None] - i[None, :]
    inner_decay = jnp.where(diff >= 0, jnp.exp(log_g[:, None, None] * jnp.maximum(diff, 0.0)), 0.0)
    lanes = lambda t: jnp.broadcast_to(t, (H, C, DK))
    q_decay = lanes(jnp.exp(log_g[:, None] * (i + 1.0))[..., None])
    k_decay = lanes(jnp.exp(log_g[:, None] * (C - 1.0 - i))[..., None])
    c_decay = lanes(jnp.exp(log_g * C)[:, None, None])
    tok = lambda width: pl.BlockSpec((C, width), lambda n: (n, 0))
    const = pl.BlockSpec((H, C, DK), lambda n: (0, 0, 0))
    return pl.pallas_call(
        _retention_body,
        grid=(S // C,),
        in_specs=[tok(p.shape[1]), tok(DK), tok(DK), const, const, const, const],
        out_specs=tok(RET_WIDTH),
        out_shape=jax.ShapeDtypeStruct((S, RET_WIDTH), f32),
        scratch_shapes=[pltpu.VMEM((H, DK, DK), f32)],
        compiler_params=pltpu.CompilerParams(dimension_semantics=("arbitrary",),
                                             vmem_limit_bytes=48 * 1024 * 1024),
        name="retention_chunked",
    )(p, cc, ss, inner_decay, q_decay, k_decay, c_decay)


MOE_ROWS = 256


def _expert_body(blk_e_ref, n_used_ref, x_ref, wg_ref, wu_ref, wd_ref, o_ref):
    i = pl.program_id(0)

    @pl.when(i < n_used_ref[0])
    def _():
        x = x_ref[...]
        hg = _dot(x, wg_ref[...])
        h = (hg * _sigmoid(hg)) * _dot(x, wu_ref[...])
        o_ref[...] = _dot(h, wd_ref[...])

    @pl.when(i >= n_used_ref[0])
    def _():
        o_ref[...] = jnp.zeros_like(o_ref)


def _moe_experts(x_slots, blk_expert, n_used, layer, w_gate, w_up, w_down):
    n_slots, D = x_slots.shape
    F = w_gate.shape[3]
    mb = MOE_ROWS
    grid_spec = pltpu.PrefetchScalarGridSpec(
        num_scalar_prefetch=2,
        grid=(n_slots // mb,),
        in_specs=[pl.BlockSpec((mb, D), lambda i, be, nu: (i, 0)),
                  pl.BlockSpec((None, None, D, F), lambda i, be, nu: (layer, be[i], 0, 0)),
                  pl.BlockSpec((None, None, D, F), lambda i, be, nu: (layer, be[i], 0, 0)),
                  pl.BlockSpec((None, None, F, D), lambda i, be, nu: (layer, be[i], 0, 0))],
        out_specs=pl.BlockSpec((mb, D), lambda i, be, nu: (i, 0)))
    return pl.pallas_call(
        _expert_body,
        grid_spec=grid_spec,
        out_shape=jax.ShapeDtypeStruct((n_slots, D), jnp.float32),
        compiler_params=pltpu.CompilerParams(dimension_semantics=("arbitrary",),
                                             vmem_limit_bytes=56 * 1024 * 1024),
        name="moe_experts",
    )(blk_expert, n_used, x_slots, w_gate, w_up, w_down)


def _hier_moe(x1b, logits, layer, w_gate, w_up, w_down):
    T, D = x1b.shape
    f32 = jnp.float32
    grp_logits = logits[:, :N_GROUPS]
    grp = jnp.argmax(grp_logits, axis=-1)
    grp_w = jnp.take_along_axis(jax.nn.softmax(grp_logits, -1), grp[:, None], axis=-1)
    exp_logits = logits[:, N_GROUPS:N_GROUPS + N_EXPERTS].reshape(T, N_GROUPS, EXPERTS_PER_GROUP)
    in_grp = jnp.take_along_axis(exp_logits, grp[:, None, None], axis=1)[:, 0]
    e_iota = jnp.arange(EXPERTS_PER_GROUP, dtype=jnp.int32)[None, :]
    i1 = jnp.argmax(in_grp, axis=-1).astype(jnp.int32)
    rest = jnp.where(e_iota == i1[:, None], -jnp.inf, in_grp)
    i2 = jnp.argmax(rest, axis=-1).astype(jnp.int32)
    top_idx = jnp.stack([i1, i2], -1)
    top_val = jnp.stack([jnp.max(in_grp, -1), jnp.max(rest, -1)], -1)
    gate_w = (jax.nn.softmax(top_val, -1) * grp_w).reshape(-1)
    expert = (grp[:, None] * EXPERTS_PER_GROUP + top_idx).reshape(-1).astype(jnp.int32)
    token = jnp.repeat(jnp.arange(T, dtype=jnp.int32), EXPERT_TOPK)
    n_assign = T * EXPERT_TOPK
    mb = MOE_ROWS
    onehot = (expert[:, None] == jnp.arange(N_EXPERTS, dtype=jnp.int32)[None, :]).astype(jnp.int32)
    csum = jnp.cumsum(onehot, axis=0)
    pos = jnp.take_along_axis(csum, expert[:, None], axis=1)[:, 0] - 1
    counts = csum[-1]
    padded = (counts + mb - 1) // mb * mb
    pad_end = jnp.cumsum(padded)
    slot = (pad_end - padded)[expert] + pos
    n_blocks = -(-n_assign // mb) + N_EXPERTS
    n_slots = n_blocks * mb
    slot_tok = jnp.full((n_slots,), T, jnp.int32).at[slot].set(token)
    blk_start = jnp.arange(n_blocks, dtype=jnp.int32) * mb
    blk_expert = jnp.minimum(jnp.sum((pad_end[None, :] <= blk_start[:, None]).astype(jnp.int32), axis=1),
                             N_EXPERTS - 1)
    n_used = (pad_end[-1:] // mb).astype(jnp.int32)
    x_slots = jnp.concatenate([x1b, jnp.zeros((1, D), x1b.dtype)])[slot_tok]
    y = _moe_experts(x_slots, blk_expert, n_used, layer, w_gate, w_up, w_down)
    ys = y[slot].reshape(T, EXPERT_TOPK, D) * gate_w.reshape(T, EXPERT_TOPK, 1)
    return ys[:, 0] + ys[:, 1]


def kernel(x, w_in_first, w_in_deep, rw_mu_first, rw_mu_deep, rw_w0, rw_w_up, rw_a0, rw_a_up, rw_v0, rw_v_up, rw_g_up, rw_k_k, rw_k_a, rw_r_k, rw_lnx_g, rw_lnx_b, nsa_cmp_pe, nsa_cmp_w1, nsa_cmp_b1, nsa_cmp_w2, w_br_rw, w_br_nsa, w_br_ret, w_out, ln1_g, ln1_b, moe_w_grp, moe_b_grp, moe_w_exp, moe_b_exp, moe_w_gate, moe_w_up, moe_w_down, ln2_g, ln2_b):
    B, S, D = x.shape
    assert B == 1
    x = x.reshape(S, D)
    xb = _bf(x)
    v_first = None
    for l in range(DEPTH):
        first = l == 0
        w_in = w_in_first if first else w_in_deep[l - 1]
        mu = rw_mu_first if first else rw_mu_deep[l - 1]
        rw_cols = RW_COLS_FIRST if first else RW_COLS_DEEP
        rw_sizes = RW_BASE_SIZES if first else RW_BASE_SIZES + (RW_V_LORA,)
        rw_padded = RW_PAD_SIZES if first else RW_PAD_SIZES + (LORA_PAD,)
        rw_tn = 512 if first else 768
        tail = -sum(rw_padded) % rw_tn
        mu_p = jnp.pad(_pad_cols(mu, rw_sizes, rw_padded), (0, tail))
        sizes = rw_sizes + (0, NSA_COLS, RET_COLS, 3 * D_MODEL)
        padded = rw_padded + (tail, NSA_PCOLS, RET_COLS, 3 * D_MODEL)
        wt = lax.optimization_barrier(jnp.transpose(w_in))
        rows, r0 = [], 0
        for n, n_pad in zip(sizes, padded):
            rows.append(jnp.pad(wt[r0:r0 + n], ((0, n_pad - n), (0, 0))))
            r0 += n
        wt = lax.optimization_barrier(_bf(jnp.concatenate(rows, 0)))
        w_all = jnp.transpose(wt)
        c0 = sum(rw_padded) + tail
        w_rw = w_all[:, :c0]
        w_nsa = w_all[:, c0:c0 + NSA_PCOLS]
        c0 += NSA_PCOLS
        w_ret = w_all[:, c0:c0 + RET_COLS]
        w_gates = w_all[:, c0 + RET_COLS:]
        p_rw = _matmul(xb, w_rw, tn=rw_tn)
        p_nsa = _matmul(xb, w_nsa, tn=NSA_PCOLS // 3)
        p_ret = _matmul(xb, w_ret, tn=512)
        gates = _matmul(xb, w_gates, tn=512, out_dtype=jnp.bfloat16)
        y_rw, v_first = _rwkv7_time_mix(p_rw, mu_p, rw_w0[l], rw_w_up[l], rw_a0[l],
                                        rw_a_up[l], rw_g_up[l], rw_k_k[l], rw_k_a[l], rw_r_k[l], rw_lnx_g[l],
                                        rw_lnx_b[l], v_first,
                                        None if first else rw_v0[l - 1], None if first else rw_v_up[l - 1])
        y_nsa = _nsa(p_nsa, nsa_cmp_pe[l], nsa_cmp_w1[l], nsa_cmp_b1[l], nsa_cmp_w2[l])
        y_ret = _retention(p_ret)
        w_br = _bf(jnp.stack([w_br_rw[l], w_br_nsa[l], w_br_ret[l]]))
        merged = _merge_branches((y_rw, y_nsa, y_ret), gates, w_br)
        w_router = jnp.pad(jnp.concatenate([moe_w_grp[l], moe_w_exp[l]], 1),
                           ((0, 0), (0, ROUTER_PAD - N_GROUPS - N_EXPERTS)))
        b_router = jnp.pad(jnp.concatenate([moe_b_grp[l], moe_b_exp[l]]),
                           (0, ROUTER_PAD - N_GROUPS - N_EXPERTS)).reshape(1, ROUTER_PAD)
        x1, x1b, logits = _out_ln_router(merged, _bf(w_out[l]), x, ln1_g[l], ln1_b[l], _bf(w_router), b_router)
        moe = _hier_moe(x1b, logits, l, moe_w_gate, moe_w_up, moe_w_down)
        x, xb = _residual_ln(x1, moe, ln2_g[l], ln2_b[l])
    return x.reshape(B, S, D)
```

```python
import functools

import jax
import jax.numpy as jnp
from jax import lax
import numpy as np
from jax.experimental import pallas as pl
from jax.experimental.pallas import tpu as pltpu

D_MODEL = 2048
DEPTH = 2

RW_HEADS = 16
RW_HEAD = 64
RW_WIDTH = RW_HEADS * RW_HEAD
RW_DECAY_LORA = 96
RW_A_LORA = 96
RW_V_LORA = 64
RW_G_LORA = 256
RW_GN_EPS = 64e-5
RW_BASE_SIZES = (RW_WIDTH, RW_WIDTH, RW_WIDTH, RW_DECAY_LORA, RW_A_LORA, RW_G_LORA)
RW_COLS_FIRST = sum(RW_BASE_SIZES)
RW_COLS_DEEP = RW_COLS_FIRST + RW_V_LORA

NSA_HEADS = 16
NSA_KV_GROUPS = 4
NSA_HPG = NSA_HEADS // NSA_KV_GROUPS
NSA_HEAD = 64
NSA_WIDTH = NSA_HEADS * NSA_HEAD
NSA_KV = NSA_KV_GROUPS * NSA_HEAD
CMP_LEN = 32
CMP_STRIDE = 16
CMP_HIDDEN = 128
SEL_BLOCK = 64
SEL_TOPN = 16
WINDOW = 512
Q_BLOCK = 128
FORCE_BONUS = 1e4
NEG_INF = -1e30
NSA_SIZES = (NSA_WIDTH,) + (NSA_KV,) * 6 + (3 * NSA_HEADS,)
NSA_COLS = sum(NSA_SIZES)

RET_HEADS = 8
RET_HEAD = 128
RET_WIDTH = RET_HEADS * RET_HEAD
RET_CHUNK = 128
RET_THETA = 10000.0
RET_GN_EPS = 1e-5
RET_SIZES = (RET_WIDTH,) * 4
RET_COLS = sum(RET_SIZES)

N_GROUPS = 4
EXPERTS_PER_GROUP = 8
N_EXPERTS = N_GROUPS * EXPERTS_PER_GROUP
EXPERT_FF = 512
EXPERT_TOPK = 2
MOE_BLOCK = 128

DN_ALPHA = (2 * DEPTH) ** 0.25
LN_EPS = 1e-5

LANES = 128


MM_ROWS = 1024


def _matmul_body(a_ref, b_ref, o_ref):
    o_ref[...] = jnp.dot(a_ref[...], b_ref[...], preferred_element_type=jnp.float32).astype(o_ref.dtype)


def _matmul(a, b, tn, out_dtype=jnp.float32):
    m, k = a.shape
    n = b.shape[1]
    tm = min(MM_ROWS, m)
    return pl.pallas_call(
        _matmul_body,
        grid=(m // tm, n // tn),
        in_specs=[pl.BlockSpec((tm, k), lambda i, j: (i, 0)),
                  pl.BlockSpec((k, tn), lambda i, j: (0, j))],
        out_specs=pl.BlockSpec((tm, tn), lambda i, j: (i, j)),
        out_shape=jax.ShapeDtypeStruct((m, n), out_dtype),
        compiler_params=pltpu.CompilerParams(
            dimension_semantics=("parallel", "parallel"),
            vmem_limit_bytes=48 * 1024 * 1024),
        name="dense_matmul",
    )(a, b)


MERGE_ROWS = 512
MERGE_COLS = 512


def _merge_body(y0_ref, y1_ref, y2_ref, g0_ref, g1_ref, g2_ref, w_ref, o_ref):
    acc = None
    for b, (y_ref, g_ref) in enumerate(((y0_ref, g0_ref), (y1_ref, g1_ref), (y2_ref, g2_ref))):
        t = _sigmoid(g_ref[...].astype(jnp.float32)) * _dot(y_ref[...], w_ref[b])
        acc = t if acc is None else acc + t
    o_ref[...] = acc.astype(o_ref.dtype)


def _merge_branches(ys, gates, w_br):
    S, wb = ys[0].shape
    D = w_br.shape[2]
    tm, tn = min(MERGE_ROWS, S), MERGE_COLS
    nj = D // tn
    y_spec = pl.BlockSpec((tm, wb), lambda i, j: (i, 0))
    g_spec = lambda b: pl.BlockSpec((tm, tn), lambda i, j: (i, b * nj + j))
    return pl.pallas_call(
        _merge_body,
        grid=(S // tm, nj),
        in_specs=[y_spec, y_spec, y_spec, g_spec(0), g_spec(1), g_spec(2),
                  pl.BlockSpec((3, wb, tn), lambda i, j: (0, 0, j))],
        out_specs=pl.BlockSpec((tm, tn), lambda i, j: (i, j)),
        out_shape=jax.ShapeDtypeStruct((S, D), jnp.bfloat16),
        compiler_params=pltpu.CompilerParams(dimension_semantics=("parallel", "parallel"),
                                             vmem_limit_bytes=48 * 1024 * 1024),
        name="merge_branches",
    )(*ys, gates, gates, gates, w_br)


LN_ROWS = 256
ROUTER_PAD = LANES


def _ln_rows(z, g, b):
    mu = jnp.mean(z, axis=-1, keepdims=True)
    zc = z - mu
    var = jnp.mean(zc * zc, axis=-1, keepdims=True)
    return zc * lax.rsqrt(var + LN_EPS) * g + b


def _out_ln_body(m_ref, w_ref, x_ref, g_ref, b_ref, wr_ref, br_ref, x1_ref, x1b_ref, lg_ref):
    z = DN_ALPHA * x_ref[...] + jnp.dot(m_ref[...], w_ref[...], preferred_element_type=jnp.float32)
    x1 = _ln_rows(z, g_ref[...], b_ref[...])
    x1_ref[...] = x1
    x1b = _bf(x1)
    x1b_ref[...] = x1b
    lg_ref[...] = jnp.dot(x1b, wr_ref[...], preferred_element_type=jnp.float32) + br_ref[...]


def _out_ln_router(merged, w_out, x, ln_g, ln_b, w_router, b_router):
    S, D = x.shape
    tm = min(LN_ROWS, S)
    row = pl.BlockSpec((tm, D), lambda i: (i, 0))
    full = lambda shape: pl.BlockSpec(shape, lambda i: (0, 0))
    return pl.pallas_call(
        _out_ln_body,
        grid=(S // tm,),
        in_specs=[row, full((D, D)), row, full((1, D)), full((1, D)), full((D, ROUTER_PAD)), full((1, ROUTER_PAD))],
        out_specs=(row, row, pl.BlockSpec((tm, ROUTER_PAD), lambda i: (i, 0))),
        out_shape=(jax.ShapeDtypeStruct((S, D), jnp.float32), jax.ShapeDtypeStruct((S, D), jnp.bfloat16),
                   jax.ShapeDtypeStruct((S, ROUTER_PAD), jnp.float32)),
        compiler_params=pltpu.CompilerParams(dimension_semantics=("parallel",),
                                             vmem_limit_bytes=48 * 1024 * 1024),
        name="out_proj_ln_router",
    )(merged, w_out, x, ln_g.reshape(1, D), ln_b.reshape(1, D), w_router, b_router)


def _residual_ln_body(x_ref, m_ref, g_ref, b_ref, o_ref, ob_ref):
    y = _ln_rows(DN_ALPHA * x_ref[...] + m_ref[...], g_ref[...], b_ref[...])
    o_ref[...] = y
    ob_ref[...] = _bf(y)


def _residual_ln(x, m, ln_g, ln_b):
    S, D = x.shape
    tm = min(LN_ROWS, S)
    row = pl.BlockSpec((tm, D), lambda i: (i, 0))
    full = pl.BlockSpec((1, D), lambda i: (0, 0))
    return pl.pallas_call(
        _residual_ln_body,
        grid=(S // tm,),
        in_specs=[row, row, full, full],
        out_specs=(row, row),
        out_shape=(jax.ShapeDtypeStruct((S, D), jnp.float32), jax.ShapeDtypeStruct((S, D), jnp.bfloat16)),
        compiler_params=pltpu.CompilerParams(dimension_semantics=("parallel",)),
        name="residual_ln",
    )(x, m, ln_g.reshape(1, D), ln_b.reshape(1, D))


def _split(p, sizes):
    idx = [int(i) for i in np.cumsum(sizes)[:-1]]
    return jnp.split(p, idx, axis=-1)


def _layer_norm(x, g, b):
    xf = x.astype(jnp.float32)
    mu = jnp.mean(xf, -1, keepdims=True)
    var = jnp.mean(jnp.square(xf - mu), -1, keepdims=True)
    return ((xf - mu) * lax.rsqrt(var + LN_EPS)).astype(x.dtype) * g + b


def _head_norm(x, eps):
    x = x.astype(jnp.float32)
    mu = jnp.mean(x, -1, keepdims=True)
    var = jnp.mean(jnp.square(x - mu), -1, keepdims=True)
    return (x - mu) * lax.rsqrt(var + eps)


def _token_shift(z, mu):
    prev = jnp.pad(z, ((0, 0), (1, 0), (0, 0)))[:, :-1]
    return z + (prev - z) * mu


def _masked_softmax(s, mask):
    s = jnp.where(mask, s, NEG_INF)
    p = jax.nn.softmax(s, axis=-1)
    return jnp.where(mask, p, 0.0)


RW_CHUNK = 64
LORA_PAD = LANES
RW_OFF_XW = 3 * RW_WIDTH
RW_OFF_XA = RW_OFF_XW + LORA_PAD
RW_OFF_XG = RW_OFF_XA + LORA_PAD
RW_OFF_XV = RW_OFF_XG + RW_G_LORA
RW_PAD_SIZES = (RW_WIDTH, RW_WIDTH, RW_WIDTH, LORA_PAD, LORA_PAD, RW_G_LORA)

_NT = (((1,), (1,)), ((), ()))
_TN = (((0,), (0,)), ((), ()))


def _bf(x):
    return x.astype(jnp.bfloat16)


def _dot(a, b, dims=None):
    if dims is None:
        return jnp.dot(_bf(a), _bf(b), preferred_element_type=jnp.float32)
    return lax.dot_general(_bf(a), _bf(b), dims, preferred_element_type=jnp.float32)


def _split3(x):
    h = _bf(x)
    r = x - h.astype(jnp.float32)
    m = _bf(r)
    l = _bf(r - m.astype(jnp.float32))
    return h, m, l


def _dot_exact_rhs(a, b_bf16, passes):
    acc = None
    for p in _split3(a)[:passes]:
        t = jnp.dot(p, b_bf16, preferred_element_type=jnp.float32)
        acc = t if acc is None else acc + t
    return acc


def _dot_exact_lhs(a_bf16, b, passes):
    acc = None
    for p in _split3(b)[:passes]:
        t = jnp.dot(a_bf16, p, preferred_element_type=jnp.float32)
        acc = t if acc is None else acc + t
    return acc


def _sigmoid(x):
    return 1.0 / (1.0 + jnp.exp(-x))


def _softplus(x):
    return jnp.maximum(x, 0.0) + jnp.log(1.0 + jnp.exp(-jnp.abs(x)))


def _rwkv_body(first, *refs):
    if first:
        (p_ref, mu_ref, w0_ref, wup_ref, a0_ref, aup_ref, gup_ref, kk_ref, ka_ref, rk_ref,
         lg_ref, lb_ref, ones_ref, y_ref, vf_out_ref, st_ref, prev_ref) = refs
    else:
        (p_ref, mu_ref, w0_ref, wup_ref, a0_ref, aup_ref, gup_ref, kk_ref, ka_ref, rk_ref,
         lg_ref, lb_ref, ones_ref, v0_ref, vup_ref, vf_in_ref, y_ref, st_ref, prev_ref) = refs
    C, H, N, W = RW_CHUNK, RW_HEADS, RW_HEAD, RW_WIDTH
    f32 = jnp.float32
    i = pl.program_id(0)

    @pl.when(i == 0)
    def _():
        st_ref[...] = jnp.zeros_like(st_ref)
        prev_ref[...] = jnp.zeros_like(prev_ref)

    z = p_ref[...]
    row = lax.broadcasted_iota(jnp.int32, z.shape, 0)
    prev = jnp.where(row == 0, prev_ref[...], pltpu.roll(z, 1, axis=0))
    prev_ref[...] = z[C - 1:C, :]
    xs = z + (prev - z) * mu_ref[...]

    r = xs[:, 0:W]
    k = xs[:, W:2 * W]
    v = xs[:, 2 * W:3 * W]
    xw = xs[:, RW_OFF_XW:RW_OFF_XW + LORA_PAD]
    xa = xs[:, RW_OFF_XA:RW_OFF_XA + LORA_PAD]
    xg = xs[:, RW_OFF_XG:RW_OFF_XG + RW_G_LORA]
    if first:
        vf_out_ref[...] = v
    else:
        xv = xs[:, RW_OFF_XV:RW_OFF_XV + LORA_PAD]
        v = v + (vf_in_ref[...] - v) * _sigmoid(v0_ref[...] + _dot(xv, vup_ref[...]))

    w_log = -_softplus(-(w0_ref[...] + _dot(jnp.tanh(xw), wup_ref[...]))) - 0.5
    logd = -jnp.exp(w_log)
    a = _sigmoid(a0_ref[...] + _dot(xa, aup_ref[...]))
    g = _dot(_sigmoid(xg), gup_ref[...])
    hsum = ones_ref[...]

    def head_sums(t):
        acc = None
        for piece in _split3(_dot_exact_rhs(t, hsum, 2)):
            u = lax.dot_general(piece, hsum, _NT, preferred_element_type=f32)
            acc = u if acc is None else acc + u
        return acc

    kk = k * kk_ref[...]
    kk_n2 = head_sums(kk * kk)
    kk = kk / jnp.maximum(jnp.sqrt(kk_n2), 1e-12)
    k = k * (1.0 + (a - 1.0) * ka_ref[...])
    b = kk * a
    bonus = head_sums(r * k * rk_ref[...]) * v

    tr = lax.broadcasted_iota(jnp.int32, (C, C), 0)
    tc = lax.broadcasted_iota(jnp.int32, (C, C), 1)
    low_incl = tr >= tc
    low_strict = tr > tc
    cum = _dot_exact_lhs(low_incl.astype(jnp.bfloat16), logd, 3)
    total = cum[C - 1:C, :]
    e_in = jnp.exp(cum)
    e_ex = jnp.exp(cum - logd)
    e_neg = jnp.exp(-cum)
    e_rem = jnp.exp(total - cum)
    kkd = kk * e_ex
    kh = k * e_neg
    bh = b * e_neg
    rd = r * e_in
    khg = k * e_rem
    bhg = b * e_rem
    gam = jnp.exp(total)

    eye = (tr == tc).astype(f32)
    heads = range(H)
    hs = lambda t: [t[:, h * N:(h + 1) * N] for h in heads]
    kkd_h, kh_h, bh_h, rd_h, v_h, khg_h, bhg_h = (hs(t) for t in (kkd, kh, bh, rd, v, khg, bhg))
    st = [st_ref[h] for h in heads]
    gm = [_dot(jnp.concatenate([kkd_h[h], rd_h[h]], axis=0),
               jnp.concatenate([kh_h[h], bh_h[h]], axis=0), _NT) for h in heads]
    a_k = [jnp.where(low_strict, gm[h][0:C, 0:C], 0.0) for h in heads]
    a_b = [jnp.where(low_strict, gm[h][0:C, C:2 * C], 0.0) for h in heads]
    b_k = [jnp.where(low_incl, gm[h][C:2 * C, 0:C], 0.0) for h in heads]
    b_b = [jnp.where(low_incl, gm[h][C:2 * C, C:2 * C], 0.0) for h in heads]
    rhs = [_dot(kkd_h[h], st[h]) + _dot(a_k[h], v_h[h]) for h in heads]
    y0 = [_dot(rd_h[h], st[h]) + _dot(b_k[h], v_h[h]) for h in heads]
    t_inv = [eye - jnp.where((tr == tc + 1) & (tr % 2 == 1), a_b[h], 0.0) for h in heads]
    m = 2
    while m < C:
        rb, cb = tr // m, tc // m
        sib = (rb == cb + 1) & (rb % 2 == 1)
        tmp = [_dot(jnp.where(sib, a_b[h], 0.0), t_inv[h]) for h in heads]
        t_inv = [t_inv[h] - _dot(t_inv[h], tmp[h]) for h in heads]
        m *= 2
    zz = [_dot(t_inv[h], rhs[h]) for h in heads]
    ys = [y0[h] - _dot(b_b[h], zz[h]) for h in heads]
    upd = [_dot(khg_h[h], v_h[h], _TN) - _dot(bhg_h[h], zz[h], _TN) for h in heads]
    for h in heads:
        st_ref[h] = st[h] * jnp.transpose(gam[:, h * N:(h + 1) * N]) + upd[h]
    y = jnp.concatenate(ys, axis=1)

    inv_n = 1.0 / N
    mean = head_sums(y) * inv_n
    yc = y - mean
    var = head_sums(yc * yc) * inv_n
    yn = yc * lax.rsqrt(var + RW_GN_EPS) * lg_ref[...] + lb_ref[...]
    y_ref[...] = (yn + bonus) * g


def _pad_rows(w, rows):
    return jnp.pad(w, ((0, rows - w.shape[0]), (0, 0)))


def _pad_cols(p, sizes, padded):
    parts = _split(p, sizes)
    return jnp.concatenate([jnp.pad(t, [(0, 0)] * (t.ndim - 1) + [(0, n - t.shape[-1])])
                            for t, n in zip(parts, padded)], -1)


def _rwkv7_time_mix(p, mu, w0, w_up, a0, a_up, g_up, k_k, k_a, r_k, lnx_g, lnx_b, v_first, v0, v_up):
    S, cols = p.shape
    first = v_first is None
    C, W = RW_CHUNK, RW_WIDTH
    row = lambda t: t.reshape(1, -1)
    hid = np.arange(W) // RW_HEAD
    ones_blk = jnp.asarray(hid[:, None] == np.arange(LANES)[None, :], jnp.bfloat16)
    full = lambda shape: pl.BlockSpec(shape, lambda i: (0,) * len(shape))
    tok = lambda width: pl.BlockSpec((C, width), lambda i: (i, 0))
    args = [p, row(mu), row(w0), _pad_rows(w_up, LORA_PAD), row(a0), _pad_rows(a_up, LORA_PAD), g_up,
            row(k_k), row(k_a), row(r_k), row(lnx_g), row(lnx_b), ones_blk]
    specs = [tok(cols), full((1, cols)), full((1, W)), full((LORA_PAD, W)), full((1, W)), full((LORA_PAD, W)),
             full((RW_G_LORA, W)), full((1, W)), full((1, W)), full((1, W)), full((1, W)), full((1, W)),
             full((W, LANES))]
    if first:
        out_shape = (jax.ShapeDtypeStruct((S, W), jnp.float32), jax.ShapeDtypeStruct((S, W), jnp.float32))
        out_specs = (tok(W), tok(W))
    else:
        args += [row(v0), _pad_rows(v_up, LORA_PAD), v_first]
        specs += [full((1, W)), full((LORA_PAD, W)), tok(W)]
        out_shape = jax.ShapeDtypeStruct((S, W), jnp.float32)
        out_specs = tok(W)
    res = pl.pallas_call(
        functools.partial(_rwkv_body, first),
        grid=(S // C,),
        in_specs=specs,
        out_specs=out_specs,
        out_shape=out_shape,
        scratch_shapes=[pltpu.VMEM((RW_HEADS, RW_HEAD, RW_HEAD), jnp.float32),
                        pltpu.VMEM((1, cols), jnp.float32)],
        compiler_params=pltpu.CompilerParams(dimension_semantics=("arbitrary",),
                                             vmem_limit_bytes=48 * 1024 * 1024),
        name="rwkv7_chunked",
    )(*args)
    if first:
        return res[0], res[1]
    return res, v_first


CMP_ROWS = 256


def _compress_body(r_ref, nx_ref, pe_ref, wa_ref, wb_ref, b1_ref, w2_ref, o_ref):
    r = r_ref[...]
    rb, half = r.shape
    row = lax.broadcasted_iota(jnp.int32, r.shape, 0)
    nxt = jnp.where(row == rb - 1, nx_ref[0:1, :], pltpu.roll(r, rb - 1, axis=0))
    pe = pe_ref[...]
    h = (_dot(r + pe[:, :half], wa_ref[...]) + _dot(nxt + pe[:, half:], wb_ref[...]) + b1_ref[...])
    h = 0.5 * h * (1.0 + jnp.tanh(0.7978845608028654 * (h + 0.044715 * (h * h * h))))
    o_ref[...] = _dot(h, w2_ref[...])


def _nsa_compress(t, pe, w1, b1, w2):
    S = t.shape[0]
    G, DH, L, ST = NSA_KV_GROUPS, NSA_HEAD, CMP_LEN, CMP_STRIDE
    nr = S // ST
    rb = min(CMP_ROWS, nr)
    cols = ST * G * DH
    r = t.reshape(nr, cols)
    eye_g = jnp.eye(G, dtype=w1.dtype)
    big = jnp.einsum('ldh,gk->lgdkh', w1.reshape(L, DH, CMP_HIDDEN), eye_g).reshape(L * G * DH, G * CMP_HIDDEN)
    wa, wb = big[:cols], big[cols:]
    pe_big = jnp.broadcast_to(pe[:, None, :], (L, G, DH)).reshape(1, L * G * DH)
    b1_big = jnp.tile(b1, G).reshape(1, G * CMP_HIDDEN)
    w2_big = jnp.einsum('hd,gk->ghkd', w2, eye_g).reshape(G * CMP_HIDDEN, G * DH)
    full = lambda shape: pl.BlockSpec(shape, lambda i: (0, 0))
    return pl.pallas_call(
        _compress_body,
        grid=(nr // rb,),
        in_specs=[pl.BlockSpec((rb, cols), lambda i: (i, 0)),
                  pl.BlockSpec((8, cols), lambda i: (jnp.minimum((i + 1) * (rb // 8), nr // 8 - 1), 0)),
                  full((1, 2 * cols)), full((cols, G * CMP_HIDDEN)), full((cols, G * CMP_HIDDEN)),
                  full((1, G * CMP_HIDDEN)), full((G * CMP_HIDDEN, G * DH))],
        out_specs=pl.BlockSpec((rb, G * DH), lambda i: (i, 0)),
        out_shape=jax.ShapeDtypeStruct((nr, G * DH), jnp.float32),
        compiler_params=pltpu.CompilerParams(dimension_semantics=("parallel",),
                                             vmem_limit_bytes=48 * 1024 * 1024),
        name="nsa_compress",
    )(r, r, pe_big, _bf(wa), _bf(wb), b1_big, _bf(w2_big))


NSA_QB = 128
NSA_KC = 512
NSA_WC = 128
NSA_GATE_PAD = LANES
NSA_PCOLS = NSA_WIDTH + 6 * NSA_KV + NSA_GATE_PAD
NSA_GATE_ROWS = 16
LOG2E = 1.4426950408889634
NSA_CSEG = 256
NSA_HOT = 16


def _softmax_cols(s, mask):
    s = jnp.where(mask, s, NEG_INF)
    m = jnp.max(s, axis=0, keepdims=True)
    p = jnp.where(mask, jnp.exp2(s - m), 0.0)
    l = jnp.sum(p, axis=0, keepdims=True)
    return p * (1.0 / jnp.where(l > 0.0, l, 1.0))


def _nsa_body(*refs):
    n_win = (WINDOW + NSA_QB) // NSA_WC
    qt_ref, kvc_ref, kvct_ref, kq_ref, vst_ref = refs[:5]
    kvw_refs = refs[5:5 + n_win]
    kvwt_refs = refs[5 + n_win:5 + 2 * n_win]
    ovl_ref, gate_ref, o_ref, bias_ref, acc_ref, sca_ref, scb_ref = refs[5 + 2 * n_win:]
    QB, HPG, DH = NSA_QB, NSA_HPG, NSA_HEAD
    f32 = jnp.float32
    qi = pl.program_id(1)
    q0 = qi * QB
    q_t = jnp.transpose(qt_ref[...] * (DH ** -0.5 * LOG2E))
    q_t = jnp.concatenate([q_t[h * DH:(h + 1) * DH] for h in range(HPG)], axis=1)
    qt = _bf(jnp.concatenate([q_t, jnp.zeros_like(q_t)], axis=0))
    ncmp = kvc_ref.shape[0]
    t_row = q0 + lax.broadcasted_iota(jnp.int32, (1, QB), 1)
    head = lambda a, h: a[:, h * QB:(h + 1) * QB]

    nsel = ovl_ref.shape[0]

    def cmp_prefix(nc):
        def run():
            s = jnp.dot(kvc_ref[0:nc, :], qt, preferred_element_type=f32)
            c_end = lax.broadcasted_iota(jnp.int32, (nc, QB), 0) * CMP_STRIDE + (CMP_LEN - 1)
            m_cmp = c_end <= t_row
            kvct = kvct_ref[:, 0:nc]
            p_sum = None
            o_parts = []
            for h in range(HPG):
                p = _softmax_cols(head(s, h), m_cmp)
                o_parts.append(jnp.dot(kvct, _bf(p), preferred_element_type=f32))
                p_sum = p if p_sum is None else p_sum + p
            return tuple(o_parts) + (_dot_exact_lhs(ovl_ref[:, 0:nc], p_sum, 2),)
        return run

    seg = min(NSA_CSEG, ncmp)
    n_seg = ncmp // seg
    visible = (q0 + QB - CMP_LEN) // CMP_STRIDE + 1
    seg_idx = jnp.clip((visible + seg - 1) // seg - 1, 0, n_seg - 1)
    res = lax.switch(seg_idx, [cmp_prefix((k + 1) * seg) for k in range(n_seg)])
    o_cmp, imp = res[:HPG], res[HPG]

    n_row = lax.broadcasted_iota(jnp.int32, (nsel, QB), 0)
    cur = t_row // SEL_BLOCK
    causal = n_row <= cur
    forced = (n_row == 0) | (n_row == cur) | (n_row == cur - 1)
    score = jnp.where(causal & jnp.logical_not(forced), imp, NEG_INF)
    bias = jnp.where(causal & forced, 0.0, NEG_INF)
    for _ in range(min(SEL_TOPN, nsel) - 3):
        best = jnp.max(score, axis=0, keepdims=True)
        first = jnp.min(jnp.where(score == best, n_row, nsel), axis=0, keepdims=True)
        hit = n_row == first
        bias = jnp.where(hit & (best > 0.5 * NEG_INF), 0.0, bias)
        score = jnp.where(hit, -jnp.inf, score)
    bias_ref[...] = bias

    KC = NSA_KC
    n_sub = KC // SEL_BLOCK
    key_row = lax.broadcasted_iota(jnp.int32, (KC, QB), 0)
    acc_ref[...] = jnp.zeros_like(acc_ref)
    last_chunk = kq_ref.shape[0] // KC - 1
    q_rows = qt[0:DH]
    pad_rows = jnp.zeros((LANES - DH - NSA_HOT, HPG * QB), jnp.bfloat16)

    def chunk_scores(idx):
        c = jnp.minimum(idx, last_chunk)
        b8 = bias_ref[pl.ds(pl.multiple_of(c * n_sub, n_sub), n_sub), :]
        b_rows = jnp.concatenate([b8, jnp.zeros((NSA_HOT - n_sub, QB), f32)], axis=0)
        q_aug = jnp.concatenate([q_rows, _bf(jnp.concatenate([b_rows] * HPG, axis=1)), pad_rows], axis=0)
        k0 = pl.multiple_of(c * KC, KC)
        return jnp.dot(kq_ref[pl.ds(k0, KC), :], q_aug, preferred_element_type=f32)

    def consume(sc_ref, idx, ms, ls, causal):
        c = jnp.minimum(idx, last_chunk)
        vt = vst_ref[:, pl.ds(pl.multiple_of(c * KC, KC), KC)]
        sh = sc_ref[...]
        if causal:
            madd = jnp.where(idx * KC + key_row <= t_row, 0.0, NEG_INF)
            sh = sh + jnp.concatenate([madd] * HPG, axis=1)
        new_m = jnp.maximum(ms, jnp.max(sh, axis=0, keepdims=True))
        alpha = jnp.exp2(ms - new_m)
        p = jnp.exp2(sh - new_m)
        new_l = alpha * ls + jnp.sum(p, axis=0, keepdims=True)
        acc_ref[...] = alpha * acc_ref[...] + jnp.dot(vt, _bf(p), preferred_element_type=f32)
        return new_m, new_l

    sca_ref[...] = chunk_scores(0)

    def pair_step(jj, carry):
        ms, ls = carry
        scb_ref[...] = chunk_scores(2 * jj + 1)
        ms, ls = consume(sca_ref, 2 * jj, ms, ls, False)
        sca_ref[...] = chunk_scores(2 * jj + 2)
        return consume(scb_ref, 2 * jj + 1, ms, ls, False)

    init = (jnp.full((1, HPG * QB), NEG_INF, f32), jnp.zeros((1, HPG * QB), f32))
    n_pairs = (q0 // KC) // 2
    ms, ls = lax.fori_loop(0, n_pairs, pair_step, init)
    scb_ref[...] = chunk_scores(2 * n_pairs + 1)
    ms, ls = consume(sca_ref, 2 * n_pairs, ms, ls, True)
    _, l_sel = consume(scb_ref, 2 * n_pairs + 1, ms, ls, True)

    WC = NSA_WC
    s_win, m_win = [], []
    for c in range(n_win):
        s_pos = q0 - WINDOW + c * WC + lax.broadcasted_iota(jnp.int32, (WC, QB), 0)
        dist = t_row - s_pos
        m_win.append((dist >= 0) & (dist < WINDOW) & (s_pos >= 0))
        s_win.append(jnp.dot(kvw_refs[c][...], qt, preferred_element_type=f32))
    m_all = jnp.concatenate([jnp.concatenate(m_win, axis=0)] * HPG, axis=1)
    s_all = jnp.where(m_all, jnp.concatenate(s_win, axis=0), NEG_INF)
    p_win = jnp.exp2(s_all - jnp.max(s_all, axis=0, keepdims=True))
    l_win = jnp.sum(p_win, axis=0, keepdims=True)
    p_win = _bf(p_win)
    o_win = None
    for c in range(n_win):
        t = jnp.dot(kvwt_refs[c][DH:, :], p_win[c * WC:(c + 1) * WC], preferred_element_type=f32)
        o_win = t if o_win is None else o_win + t
    o_win = o_win * (1.0 / l_win)

    gate = _sigmoid(gate_ref[...])
    outs = []
    for h in range(HPG):
        o_sel = head(acc_ref[...], h) * (1.0 / head(l_sel, h))
        gr = lambda b: gate[h * 3 + b:h * 3 + b + 1, :]
        outs.append(gr(0) * o_cmp[h][DH:, :] + gr(1) * o_sel + gr(2) * head(o_win, h))
    o_ref[...] = jnp.transpose(jnp.concatenate(outs, axis=0))


def _nsa(p, cmp_pe, cmp_w1, cmp_b1, cmp_w2):
    S = p.shape[0]
    G, HPG, DH, QB, WC = NSA_KV_GROUPS, NSA_HPG, NSA_HEAD, NSA_QB, NSA_WC
    W, KV = NSA_WIDTH, NSA_KV
    nqb = S // QB
    n_win = (WINDOW + QB) // WC
    q = p[:, :W]
    kc, vc, ks, vs, kw, vw = (p[:, W + i * KV:W + (i + 1) * KV] for i in range(6))
    gate = p[:, W + 6 * KV:W + 6 * KV + 3 * NSA_HEADS]
    k_cmp = _nsa_compress(kc, cmp_pe[0], cmp_w1[0], cmp_b1[0], cmp_w2[0])
    v_cmp = _nsa_compress(vc, cmp_pe[1], cmp_w1[1], cmp_b1[1], cmp_w2[1])
    pack = lambda k, v: _bf(jnp.concatenate([k.reshape(-1, G, DH), v.reshape(-1, G, DH)], -1)).transpose(1, 0, 2)
    kvc, kvw = pack(k_cmp, v_cmp), pack(kw, vw)
    hot = jax.nn.one_hot((jnp.arange(S) // SEL_BLOCK) % (NSA_KC // SEL_BLOCK), LANES - DH, dtype=jnp.bfloat16)
    kq = jnp.concatenate([_bf(ks.reshape(S, G, DH)), jnp.broadcast_to(hot[:, None, :], (S, G, LANES - DH))], -1)
    kq = kq.transpose(1, 0, 2)
    vst = _bf(vs.reshape(S, G, DH)).transpose(1, 2, 0)
    tr = lambda a: a.transpose(0, 2, 1)
    gate_t = jnp.pad(gate.reshape(S, G, 3 * HPG).transpose(1, 2, 0), ((0, 0), (0, NSA_GATE_ROWS - 3 * HPG), (0, 0)))
    ncmp, nsel = S // CMP_STRIDE, S // SEL_BLOCK
    c_start = np.arange(ncmp)[None, :] * CMP_STRIDE
    s_start = np.arange(nsel)[:, None] * SEL_BLOCK
    overlap_t = jnp.asarray((c_start < s_start + SEL_BLOCK) & (c_start + CMP_LEN > s_start), jnp.bfloat16)
    win_blk = lambda c: (lambda g, i: (g, jnp.maximum(i - WINDOW // WC + c, 0), 0))
    win_blk_t = lambda c: (lambda g, i: (g, 0, jnp.maximum(i - WINDOW // WC + c, 0)))
    in_specs = ([pl.BlockSpec((QB, HPG * DH), lambda g, i: (i, g)),
                 pl.BlockSpec((None, ncmp, LANES), lambda g, i: (g, 0, 0)),
                 pl.BlockSpec((None, LANES, ncmp), lambda g, i: (g, 0, 0)),
                 pl.BlockSpec((None, S, LANES), lambda g, i: (g, 0, 0)),
                 pl.BlockSpec((None, DH, S), lambda g, i: (g, 0, 0))]
                + [pl.BlockSpec((None, WC, LANES), win_blk(c)) for c in range(n_win)]
                + [pl.BlockSpec((None, LANES, WC), win_blk_t(c)) for c in range(n_win)]
                + [pl.BlockSpec((nsel, ncmp), lambda g, i: (0, 0)),
                   pl.BlockSpec((None, NSA_GATE_ROWS, QB), lambda g, i: (g, 0, i))])
    kvwt = tr(kvw)
    return pl.pallas_call(
        _nsa_body,
        grid=(G, nqb),
        in_specs=in_specs,
        out_specs=pl.BlockSpec((QB, HPG * DH), lambda g, i: (i, g)),
        out_shape=jax.ShapeDtypeStruct((S, W), jnp.float32),
        scratch_shapes=[pltpu.VMEM((nsel, QB), jnp.float32),
                        pltpu.VMEM((DH, HPG * QB), jnp.float32),
                        pltpu.VMEM((NSA_KC, HPG * QB), jnp.float32),
                        pltpu.VMEM((NSA_KC, HPG * QB), jnp.float32)],
        compiler_params=pltpu.CompilerParams(dimension_semantics=("arbitrary", "arbitrary"),
                                             vmem_limit_bytes=56 * 1024 * 1024),
        name="nsa_attention",
    )(p, kvc, tr(kvc), kq, vst, *([kvw] * n_win), *([kvwt] * n_win), overlap_t, gate_t)


def _retention_body(p_ref, cc_ref, ss_ref, inner_ref, qd_ref, kd_ref, cd_ref, o_ref, r_ref):
    H, DK, W = RET_HEADS, RET_HEAD, RET_WIDTH
    f32 = jnp.float32

    @pl.when(pl.program_id(0) == 0)
    def _():
        r_ref[...] = jnp.zeros_like(r_ref)

    cc, ss = cc_ref[...], ss_ref[...]
    hh = range(H)
    col = lambda part, h: p_ref[:, part * W + h * DK:part * W + (h + 1) * DK].astype(f32)
    rot = lambda t: t * cc + pltpu.roll(t, DK // 2, axis=1) * ss
    q = [rot(col(0, h)) for h in hh]
    k = [rot(col(1, h)) * (DK ** -0.5) for h in hh]
    v = [col(2, h) for h in hh]
    r = [r_ref[h] for h in hh]
    att = [_dot(q[h], k[h], _NT) * inner_ref[h] for h in hh]
    o = [_dot(att[h], v[h]) + _dot(q[h], r[h]) * qd_ref[h] for h in hh]
    upd = [_dot(k[h] * kd_ref[h], v[h], _TN) for h in hh]
    for h in hh:
        r_ref[h] = r[h] * cd_ref[h] + upd[h]
        mu = jnp.mean(o[h], axis=-1, keepdims=True)
        oc = o[h] - mu
        var = jnp.mean(oc * oc, axis=-1, keepdims=True)
        g = col(3, h)
        o_ref[:, h * DK:(h + 1) * DK] = (g * _sigmoid(g)) * (oc * lax.rsqrt(var + RET_GN_EPS))


def _retention(p):
    S = p.shape[0]
    H, DK, C = RET_HEADS, RET_HEAD, RET_CHUNK
    f32 = jnp.float32
    inv_freq = 1.0 / (RET_THETA ** jnp.linspace(0.0, 1.0, DK // 2))
    ang = jnp.arange(S, dtype=f32)[:, None] * inv_freq[None, :]
    cos, sin = jnp.cos(ang), jnp.sin(ang)
    cc = jnp.concatenate([cos, cos], -1)
    ss = jnp.concatenate([-sin, sin], -1)
    log_g = jnp.log1p(-jnp.exp2(-5.0 - jnp.arange(H, dtype=f32)))
    i = jnp.arange(C, dtype=f32)
    diff = i[:, None] - i[None, :]
    inner_decay = jnp.where(diff >= 0, jnp.exp(log_g[:, None, None] * jnp.maximum(diff, 0.0)), 0.0)
    lanes = lambda t: jnp.broadcast_to(t, (H, C, DK))
    q_decay = lanes(jnp.exp(log_g[:, None] * (i + 1.0))[..., None])
    k_decay = lanes(jnp.exp(log_g[:, None] * (C - 1.0 - i))[..., None])
    c_decay = lanes(jnp.exp(log_g * C)[:, None, None])
    tok = lambda width: pl.BlockSpec((C, width), lambda n: (n, 0))
    const = pl.BlockSpec((H, C, DK), lambda n: (0, 0, 0))
    return pl.pallas_call(
        _retention_body,
        grid=(S // C,),
        in_specs=[tok(p.shape[1]), tok(DK), tok(DK), const, const, const, const],
        out_specs=tok(RET_WIDTH),
        out_shape=jax.ShapeDtypeStruct((S, RET_WIDTH), f32),
        scratch_shapes=[pltpu.VMEM((H, DK, DK), f32)],
        compiler_params=pltpu.CompilerParams(dimension_semantics=("arbitrary",),
                                             vmem_limit_bytes=48 * 1024 * 1024),
        name="retention_chunked",
    )(p, cc, ss, inner_decay, q_decay, k_decay, c_decay)


MOE_ROWS = 256


def _expert_body(blk_e_ref, n_used_ref, x_ref, wg_ref, wu_ref, wd_ref, o_ref):
    i = pl.program_id(0)

    @pl.when(i < n_used_ref[0])
    def _():
        x = x_ref[...]
        hg = _dot(x, wg_ref[...])
        h = (hg * _sigmoid(hg)) * _dot(x, wu_ref[...])
        o_ref[...] = _dot(h, wd_ref[...])

    @pl.when(i >= n_used_ref[0])
    def _():
        o_ref[...] = jnp.zeros_like(o_ref)


def _moe_experts(x_slots, blk_expert, n_used, layer, w_gate, w_up, w_down):
    n_slots, D = x_slots.shape
    F = w_gate.shape[3]
    mb = MOE_ROWS
    grid_spec = pltpu.PrefetchScalarGridSpec(
        num_scalar_prefetch=2,
        grid=(n_slots // mb,),
        in_specs=[pl.BlockSpec((mb, D), lambda i, be, nu: (i, 0)),
                  pl.BlockSpec((None, None, D, F), lambda i, be, nu: (layer, be[i], 0, 0)),
                  pl.BlockSpec((None, None, D, F), lambda i, be, nu: (layer, be[i], 0, 0)),
                  pl.BlockSpec((None, None, F, D), lambda i, be, nu: (layer, be[i], 0, 0))],
        out_specs=pl.BlockSpec((mb, D), lambda i, be, nu: (i, 0)))
    return pl.pallas_call(
        _expert_body,
        grid_spec=grid_spec,
        out_shape=jax.ShapeDtypeStruct((n_slots, D), jnp.float32),
        compiler_params=pltpu.CompilerParams(dimension_semantics=("arbitrary",),
                                             vmem_limit_bytes=56 * 1024 * 1024),
        name="moe_experts",
    )(blk_expert, n_used, x_slots, w_gate, w_up, w_down)


def _hier_moe(x1b, logits, layer, w_gate, w_up, w_down):
    T, D = x1b.shape
    f32 = jnp.float32
    grp_logits = logits[:, :N_GROUPS]
    grp = jnp.argmax(grp_logits, axis=-1)
    grp_w = jnp.take_along_axis(jax.nn.softmax(grp_logits, -1), grp[:, None], axis=-1)
    exp_logits = logits[:, N_GROUPS:N_GROUPS + N_EXPERTS].reshape(T, N_GROUPS, EXPERTS_PER_GROUP)
    in_grp = jnp.take_along_axis(exp_logits, grp[:, None, None], axis=1)[:, 0]
    e_iota = jnp.arange(EXPERTS_PER_GROUP, dtype=jnp.int32)[None, :]
    i1 = jnp.argmax(in_grp, axis=-1).astype(jnp.int32)
    rest = jnp.where(e_iota == i1[:, None], -jnp.inf, in_grp)
    i2 = jnp.argmax(rest, axis=-1).astype(jnp.int32)
    top_idx = jnp.stack([i1, i2], -1)
    top_val = jnp.stack([jnp.max(in_grp, -1), jnp.max(rest, -1)], -1)
    gate_w = (jax.nn.softmax(top_val, -1) * grp_w).reshape(-1)
    expert = (grp[:, None] * EXPERTS_PER_GROUP + top_idx).reshape(-1).astype(jnp.int32)
    token = jnp.repeat(jnp.arange(T, dtype=jnp.int32), EXPERT_TOPK)
    n_assign = T * EXPERT_TOPK
    mb = MOE_ROWS
    onehot = (expert[:, None] == jnp.arange(N_EXPERTS, dtype=jnp.int32)[None, :]).astype(jnp.int32)
    csum = jnp.cumsum(onehot, axis=0)
    pos = jnp.take_along_axis(csum, expert[:, None], axis=1)[:, 0] - 1
    counts = csum[-1]
    padded = (counts + mb - 1) // mb * mb
    pad_end = jnp.cumsum(padded)
    slot = (pad_end - padded)[expert] + pos
    n_blocks = -(-n_assign // mb) + N_EXPERTS
    n_slots = n_blocks * mb
    slot_tok = jnp.full((n_slots,), T, jnp.int32).at[slot].set(token)
    blk_start = jnp.arange(n_blocks, dtype=jnp.int32) * mb
    blk_expert = jnp.minimum(jnp.sum((pad_end[None, :] <= blk_start[:, None]).astype(jnp.int32), axis=1),
                             N_EXPERTS - 1)
    n_used = (pad_end[-1:] // mb).astype(jnp.int32)
    x_slots = jnp.concatenate([x1b, jnp.zeros((1, D), x1b.dtype)])[slot_tok]
    y = _moe_experts(x_slots, blk_expert, n_used, layer, w_gate, w_up, w_down)
    ys = y[slot].reshape(T, EXPERT_TOPK, D) * gate_w.reshape(T, EXPERT_TOPK, 1)
    return ys[:, 0] + ys[:, 1]


def kernel(x, w_in_first, w_in_deep, rw_mu_first, rw_mu_deep, rw_w0, rw_w_up, rw_a0, rw_a_up, rw_v0, rw_v_up, rw_g_up, rw_k_k, rw_k_a, rw_r_k, rw_lnx_g, rw_lnx_b, nsa_cmp_pe, nsa_cmp_w1, nsa_cmp_b1, nsa_cmp_w2, w_br_rw, w_br_nsa, w_br_ret, w_out, ln1_g, ln1_b, moe_w_grp, moe_b_grp, moe_w_exp, moe_b_exp, moe_w_gate, moe_w_up, moe_w_down, ln2_g, ln2_b):
    B, S, D = x.shape
    assert B == 1
    x = x.reshape(S, D)
    xb = _bf(x)
    v_first = None
    for l in range(DEPTH):
        first = l == 0
        w_in = w_in_first if first else w_in_deep[l - 1]
        mu = rw_mu_first if first else rw_mu_deep[l - 1]
        rw_cols = RW_COLS_FIRST if first else RW_COLS_DEEP
        rw_sizes = RW_BASE_SIZES if first else RW_BASE_SIZES + (RW_V_LORA,)
        rw_padded = RW_PAD_SIZES if first else RW_PAD_SIZES + (LORA_PAD,)
        rw_tn = 512 if first else 768
        tail = -sum(rw_padded) % rw_tn
        mu_p = jnp.pad(_pad_cols(mu, rw_sizes, rw_padded), (0, tail))
        sizes = rw_sizes + (0, NSA_COLS, RET_COLS, 3 * D_MODEL)
        padded = rw_padded + (tail, NSA_PCOLS, RET_COLS, 3 * D_MODEL)
        wt = lax.optimization_barrier(jnp.transpose(w_in))
        rows, r0 = [], 0
        for n, n_pad in zip(sizes, padded):
            rows.append(jnp.pad(wt[r0:r0 + n], ((0, n_pad - n), (0, 0))))
            r0 += n
        wt = lax.optimization_barrier(_bf(jnp.concatenate(rows, 0)))
        w_all = jnp.transpose(wt)
        c0 = sum(rw_padded) + tail
        w_rw = w_all[:, :c0]
        w_nsa = w_all[:, c0:c0 + NSA_PCOLS]
        c0 += NSA_PCOLS
        w_ret = w_all[:, c0:c0 + RET_COLS]
        w_gates = w_all[:, c0 + RET_COLS:]
        p_rw = _matmul(xb, w_rw, tn=rw_tn)
        p_nsa = _matmul(xb, w_nsa, tn=NSA_PCOLS // 3)
        p_ret = _matmul(xb, w_ret, tn=512)
        gates = _matmul(xb, w_gates, tn=512, out_dtype=jnp.bfloat16)
        y_rw, v_first = _rwkv7_time_mix(p_rw, mu_p, rw_w0[l], rw_w_up[l], rw_a0[l],
                                        rw_a_up[l], rw_g_up[l], rw_k_k[l], rw_k_a[l], rw_r_k[l], rw_lnx_g[l],
                                        rw_lnx_b[l], v_first,
                                        None if first else rw_v0[l - 1], None if first else rw_v_up[l - 1])
        y_nsa = _nsa(p_nsa, nsa_cmp_pe[l], nsa_cmp_w1[l], nsa_cmp_b1[l], nsa_cmp_w2[l])
        y_ret = _retention(p_ret)
        w_br = _bf(jnp.stack([w_br_rw[l], w_br_nsa[l], w_br_ret[l]]))
        merged = _merge_branches((y_rw, y_nsa, y_ret), gates, w_br)
        w_router = jnp.pad(jnp.concatenate([moe_w_grp[l], moe_w_exp[l]], 1),
                           ((0, 0), (0, ROUTER_PAD - N_GROUPS - N_EXPERTS)))
        b_router = jnp.pad(jnp.concatenate([moe_b_grp[l], moe_b_exp[l]]),
                           (0, ROUTER_PAD - N_GROUPS - N_EXPERTS)).reshape(1, ROUTER_PAD)
        x1, x1b, logits = _out_ln_router(merged, _bf(w_out[l]), x, ln1_g[l], ln1_b[l], _bf(w_router), b_router)
        moe = _hier_moe(x1b, logits, l, moe_w_gate, moe_w_up, moe_w_down)
        x, xb = _residual_ln(x1, moe, ln2_g[l], ln2_b[l])
    return x.reshape(B, S, D)
```

```python
import functools

import jax
import jax.numpy as jnp
from jax import lax
import numpy as np
from jax.experimental import pallas as pl
from jax.experimental.pallas import tpu as pltpu

D_MODEL = 2048
DEPTH = 2

RW_HEADS = 16
RW_HEAD = 64
RW_WIDTH = RW_HEADS * RW_HEAD
RW_DECAY_LORA = 96
RW_A_LORA = 96
RW_V_LORA = 64
RW_G_LORA = 256
RW_GN_EPS = 64e-5
RW_BASE_SIZES = (RW_WIDTH, RW_WIDTH, RW_WIDTH, RW_DECAY_LORA, RW_A_LORA, RW_G_LORA)
RW_COLS_FIRST = sum(RW_BASE_SIZES)
RW_COLS_DEEP = RW_COLS_FIRST + RW_V_LORA

NSA_HEADS = 16
NSA_KV_GROUPS = 4
NSA_HPG = NSA_HEADS // NSA_KV_GROUPS
NSA_HEAD = 64
NSA_WIDTH = NSA_HEADS * NSA_HEAD
NSA_KV = NSA_KV_GROUPS * NSA_HEAD
CMP_LEN = 32
CMP_STRIDE = 16
CMP_HIDDEN = 128
SEL_BLOCK = 64
SEL_TOPN = 16
WINDOW = 512
Q_BLOCK = 128
FORCE_BONUS = 1e4
NEG_INF = -1e30
NSA_SIZES = (NSA_WIDTH,) + (NSA_KV,) * 6 + (3 * NSA_HEADS,)
NSA_COLS = sum(NSA_SIZES)

RET_HEADS = 8
RET_HEAD = 128
RET_WIDTH = RET_HEADS * RET_HEAD
RET_CHUNK = 128
RET_THETA = 10000.0
RET_GN_EPS = 1e-5
RET_SIZES = (RET_WIDTH,) * 4
RET_COLS = sum(RET_SIZES)

N_GROUPS = 4
EXPERTS_PER_GROUP = 8
N_EXPERTS = N_GROUPS * EXPERTS_PER_GROUP
EXPERT_FF = 512
EXPERT_TOPK = 2
MOE_BLOCK = 128

DN_ALPHA = (2 * DEPTH) ** 0.25
LN_EPS = 1e-5

LANES = 128


MM_ROWS = 1024


def _matmul_body(a_ref, b_ref, o_ref):
    o_ref[...] = jnp.dot(a_ref[...], b_ref[...], preferred_element_type=jnp.float32).astype(o_ref.dtype)


def _matmul(a, b, tn, out_dtype=jnp.float32):
    m, k = a.shape
    n = b.shape[1]
    tm = min(MM_ROWS, m)
    return pl.pallas_call(
        _matmul_body,
        grid=(m // tm, n // tn),
        in_specs=[pl.BlockSpec((tm, k), lambda i, j: (i, 0)),
                  pl.BlockSpec((k, tn), lambda i, j: (0, j))],
        out_specs=pl.BlockSpec((tm, tn), lambda i, j: (i, j)),
        out_shape=jax.ShapeDtypeStruct((m, n), out_dtype),
        compiler_params=pltpu.CompilerParams(
            dimension_semantics=("parallel", "parallel"),
            vmem_limit_bytes=48 * 1024 * 1024),
        name="dense_matmul",
    )(a, b)


MERGE_ROWS = 512
MERGE_COLS = 512


def _merge_body(y0_ref, y1_ref, y2_ref, g0_ref, g1_ref, g2_ref, w_ref, o_ref):
    acc = None
    for b, (y_ref, g_ref) in enumerate(((y0_ref, g0_ref), (y1_ref, g1_ref), (y2_ref, g2_ref))):
        t = _sigmoid(g_ref[...].astype(jnp.float32)) * _dot(y_ref[...], w_ref[b])
        acc = t if acc is None else acc + t
    o_ref[...] = acc.astype(o_ref.dtype)


def _merge_branches(ys, gates, w_br):
    S, wb = ys[0].shape
    D = w_br.shape[2]
    tm, tn = min(MERGE_ROWS, S), MERGE_COLS
    nj = D // tn
    y_spec = pl.BlockSpec((tm, wb), lambda i, j: (i, 0))
    g_spec = lambda b: pl.BlockSpec((tm, tn), lambda i, j: (i, b * nj + j))
    return pl.pallas_call(
        _merge_body,
        grid=(S // tm, nj),
        in_specs=[y_spec, y_spec, y_spec, g_spec(0), g_spec(1), g_spec(2),
                  pl.BlockSpec((3, wb, tn), lambda i, j: (0, 0, j))],
        out_specs=pl.BlockSpec((tm, tn), lambda i, j: (i, j)),
        out_shape=jax.ShapeDtypeStruct((S, D), jnp.bfloat16),
        compiler_params=pltpu.CompilerParams(dimension_semantics=("parallel", "parallel"),
                                             vmem_limit_bytes=48 * 1024 * 1024),
        name="merge_branches",
    )(*ys, gates, gates, gates, w_br)


LN_ROWS = 256
ROUTER_PAD = LANES


def _ln_rows(z, g, b):
    mu = jnp.mean(z, axis=-1, keepdims=True)
    zc = z - mu
    var = jnp.mean(zc * zc, axis=-1, keepdims=True)
    return zc * lax.rsqrt(var + LN_EPS) * g + b


def _out_ln_body(m_ref, w_ref, x_ref, g_ref, b_ref, wr_ref, br_ref, x1_ref, x1b_ref, lg_ref):
    z = DN_ALPHA * x_ref[...] + jnp.dot(m_ref[...], w_ref[...], preferred_element_type=jnp.float32)
    x1 = _ln_rows(z, g_ref[...], b_ref[...])
    x1_ref[...] = x1
    x1b = _bf(x1)
    x1b_ref[...] = x1b
    lg_ref[...] = jnp.dot(x1b, wr_ref[...], preferred_element_type=jnp.float32) + br_ref[...]


def _out_ln_router(merged, w_out, x, ln_g, ln_b, w_router, b_router):
    S, D = x.shape
    tm = min(LN_ROWS, S)
    row = pl.BlockSpec((tm, D), lambda i: (i, 0))
    full = lambda shape: pl.BlockSpec(shape, lambda i: (0, 0))
    return pl.pallas_call(
        _out_ln_body,
        grid=(S // tm,),
        in_specs=[row, full((D, D)), row, full((1, D)), full((1, D)), full((D, ROUTER_PAD)), full((1, ROUTER_PAD))],
        out_specs=(row, row, pl.BlockSpec((tm, ROUTER_PAD), lambda i: (i, 0))),
        out_shape=(jax.ShapeDtypeStruct((S, D), jnp.float32), jax.ShapeDtypeStruct((S, D), jnp.bfloat16),
                   jax.ShapeDtypeStruct((S, ROUTER_PAD), jnp.float32)),
        compiler_params=pltpu.CompilerParams(dimension_semantics=("parallel",),
                                             vmem_limit_bytes=48 * 1024 * 1024),
        name="out_proj_ln_router",
    )(merged, w_out, x, ln_g.reshape(1, D), ln_b.reshape(1, D), w_router, b_router)


def _moe_combine_ln_body(x_ref, y0_ref, y1_ref, w0_ref, w1_ref, g_ref, b_ref, o_ref, ob_ref):
    moe = y0_ref[...] * w0_ref[...] + y1_ref[...] * w1_ref[...]
    y = _ln_rows(DN_ALPHA * x_ref[...] + moe, g_ref[...], b_ref[...])
    o_ref[...] = y
    ob_ref[...] = _bf(y)


def _moe_combine_ln(x, ys, ws, ln_g, ln_b):
    S, D = x.shape
    tm = min(LN_ROWS, S)
    row = pl.BlockSpec((tm, D), lambda i: (i, 0))
    col = pl.BlockSpec((tm, 1), lambda i: (i, 0))
    full = pl.BlockSpec((1, D), lambda i: (0, 0))
    return pl.pallas_call(
        _moe_combine_ln_body,
        grid=(S // tm,),
        in_specs=[row, row, row, col, col, full, full],
        out_specs=(row, row),
        out_shape=(jax.ShapeDtypeStruct((S, D), jnp.float32), jax.ShapeDtypeStruct((S, D), jnp.bfloat16)),
        compiler_params=pltpu.CompilerParams(dimension_semantics=("parallel",)),
        name="moe_combine_ln",
    )(x, ys[0], ys[1], ws[0], ws[1], ln_g.reshape(1, D), ln_b.reshape(1, D))


def _split(p, sizes):
    idx = [int(i) for i in np.cumsum(sizes)[:-1]]
    return jnp.split(p, idx, axis=-1)


RW_CHUNK = 64
LORA_PAD = LANES
RW_OFF_XW = 3 * RW_WIDTH
RW_OFF_XA = RW_OFF_XW + LORA_PAD
RW_OFF_XG = RW_OFF_XA + LORA_PAD
RW_OFF_XV = RW_OFF_XG + RW_G_LORA
RW_PAD_SIZES = (RW_WIDTH, RW_WIDTH, RW_WIDTH, LORA_PAD, LORA_PAD, RW_G_LORA)

_NT = (((1,), (1,)), ((), ()))
_TN = (((0,), (0,)), ((), ()))


def _bf(x):
    return x.astype(jnp.bfloat16)


def _dot(a, b, dims=None):
    if dims is None:
        return jnp.dot(_bf(a), _bf(b), preferred_element_type=jnp.float32)
    return lax.dot_general(_bf(a), _bf(b), dims, preferred_element_type=jnp.float32)


def _split3(x):
    h = _bf(x)
    r = x - h.astype(jnp.float32)
    m = _bf(r)
    l = _bf(r - m.astype(jnp.float32))
    return h, m, l


def _dot_exact_rhs(a, b_bf16, passes):
    acc = None
    for p in _split3(a)[:passes]:
        t = jnp.dot(p, b_bf16, preferred_element_type=jnp.float32)
        acc = t if acc is None else acc + t
    return acc


def _dot_exact_lhs(a_bf16, b, passes):
    acc = None
    for p in _split3(b)[:passes]:
        t = jnp.dot(a_bf16, p, preferred_element_type=jnp.float32)
        acc = t if acc is None else acc + t
    return acc


def _sigmoid(x):
    return 1.0 / (1.0 + jnp.exp(-x))


def _softplus(x):
    return jnp.maximum(x, 0.0) + jnp.log(1.0 + jnp.exp(-jnp.abs(x)))


def _rwkv_body(first, *refs):
    if first:
        (p_ref, mu_ref, w0_ref, wup_ref, a0_ref, aup_ref, gup_ref, kk_ref, ka_ref, rk_ref,
         lg_ref, lb_ref, ones_ref, y_ref, vf_out_ref, st_ref, prev_ref) = refs
    else:
        (p_ref, mu_ref, w0_ref, wup_ref, a0_ref, aup_ref, gup_ref, kk_ref, ka_ref, rk_ref,
         lg_ref, lb_ref, ones_ref, v0_ref, vup_ref, vf_in_ref, y_ref, st_ref, prev_ref) = refs
    C, H, N, W = RW_CHUNK, RW_HEADS, RW_HEAD, RW_WIDTH
    f32 = jnp.float32
    i = pl.program_id(0)

    @pl.when(i == 0)
    def _():
        st_ref[...] = jnp.zeros_like(st_ref)
        prev_ref[...] = jnp.zeros_like(prev_ref)

    z = p_ref[...]
    row = lax.broadcasted_iota(jnp.int32, z.shape, 0)
    prev = jnp.where(row == 0, prev_ref[...], pltpu.roll(z, 1, axis=0))
    prev_ref[...] = z[C - 1:C, :]
    xs = z + (prev - z) * mu_ref[...]

    r = xs[:, 0:W]
    k = xs[:, W:2 * W]
    v = xs[:, 2 * W:3 * W]
    xw = xs[:, RW_OFF_XW:RW_OFF_XW + LORA_PAD]
    xa = xs[:, RW_OFF_XA:RW_OFF_XA + LORA_PAD]
    xg = xs[:, RW_OFF_XG:RW_OFF_XG + RW_G_LORA]
    if first:
        vf_out_ref[...] = v
    else:
        xv = xs[:, RW_OFF_XV:RW_OFF_XV + LORA_PAD]
        v = v + (vf_in_ref[...] - v) * _sigmoid(v0_ref[...] + _dot(xv, vup_ref[...]))

    w_log = -_softplus(-(w0_ref[...] + _dot(jnp.tanh(xw), wup_ref[...]))) - 0.5
    logd = -jnp.exp(w_log)
    a = _sigmoid(a0_ref[...] + _dot(xa, aup_ref[...]))
    g = _dot(_sigmoid(xg), gup_ref[...])
    hsum = ones_ref[...]

    def head_sums(t):
        acc = None
        for piece in _split3(_dot_exact_rhs(t, hsum, 2)):
            u = lax.dot_general(piece, hsum, _NT, preferred_element_type=f32)
            acc = u if acc is None else acc + u
        return acc

    kk = k * kk_ref[...]
    kk_n2 = head_sums(kk * kk)
    kk = kk / jnp.maximum(jnp.sqrt(kk_n2), 1e-12)
    k = k * (1.0 + (a - 1.0) * ka_ref[...])
    b = kk * a
    bonus = head_sums(r * k * rk_ref[...]) * v

    tr = lax.broadcasted_iota(jnp.int32, (C, C), 0)
    tc = lax.broadcasted_iota(jnp.int32, (C, C), 1)
    low_incl = tr >= tc
    low_strict = tr > tc
    cum = _dot_exact_lhs(low_incl.astype(jnp.bfloat16), logd, 3)
    total = cum[C - 1:C, :]
    e_in = jnp.exp(cum)
    e_ex = jnp.exp(cum - logd)
    e_neg = jnp.exp(-cum)
    e_rem = jnp.exp(total - cum)
    kkd = kk * e_ex
    kh = k * e_neg
    bh = b * e_neg
    rd = r * e_in
    khg = k * e_rem
    bhg = b * e_rem
    gam = jnp.exp(total)

    eye = (tr == tc).astype(f32)
    heads = range(H)
    hs = lambda t: [t[:, h * N:(h + 1) * N] for h in heads]
    kkd_h, kh_h, bh_h, rd_h, v_h, khg_h, bhg_h = (hs(t) for t in (kkd, kh, bh, rd, v, khg, bhg))
    st = [st_ref[h] for h in heads]
    gm = [_dot(jnp.concatenate([kkd_h[h], rd_h[h]], axis=0),
               jnp.concatenate([kh_h[h], bh_h[h]], axis=0), _NT) for h in heads]
    a_k = [jnp.where(low_strict, gm[h][0:C, 0:C], 0.0) for h in heads]
    a_b = [jnp.where(low_strict, gm[h][0:C, C:2 * C], 0.0) for h in heads]
    b_k = [jnp.where(low_incl, gm[h][C:2 * C, 0:C], 0.0) for h in heads]
    b_b = [jnp.where(low_incl, gm[h][C:2 * C, C:2 * C], 0.0) for h in heads]
    rhs = [_dot(kkd_h[h], st[h]) + _dot(a_k[h], v_h[h]) for h in heads]
    y0 = [_dot(rd_h[h], st[h]) + _dot(b_k[h], v_h[h]) for h in heads]
    t_inv = [eye - jnp.where((tr == tc + 1) & (tr % 2 == 1), a_b[h], 0.0) for h in heads]
    m = 2
    while m < C:
        rb, cb = tr // m, tc // m
        sib = (rb == cb + 1) & (rb % 2 == 1)
        tmp = [_dot(jnp.where(sib, a_b[h], 0.0), t_inv[h]) for h in heads]
        t_inv = [t_inv[h] - _dot(t_inv[h], tmp[h]) for h in heads]
        m *= 2
    zz = [_dot(t_inv[h], rhs[h]) for h in heads]
    ys = [y0[h] - _dot(b_b[h], zz[h]) for h in heads]
    upd = [_dot(khg_h[h], v_h[h], _TN) - _dot(bhg_h[h], zz[h], _TN) for h in heads]
    for h in heads:
        st_ref[h] = st[h] * jnp.transpose(gam[:, h * N:(h + 1) * N]) + upd[h]
    y = jnp.concatenate(ys, axis=1)

    inv_n = 1.0 / N
    mean = head_sums(y) * inv_n
    yc = y - mean
    var = head_sums(yc * yc) * inv_n
    yn = yc * lax.rsqrt(var + RW_GN_EPS) * lg_ref[...] + lb_ref[...]
    y_ref[...] = (yn + bonus) * g


def _pad_rows(w, rows):
    return jnp.pad(w, ((0, rows - w.shape[0]), (0, 0)))


def _pad_cols(p, sizes, padded):
    parts = _split(p, sizes)
    return jnp.concatenate([jnp.pad(t, [(0, 0)] * (t.ndim - 1) + [(0, n - t.shape[-1])])
                            for t, n in zip(parts, padded)], -1)


def _rwkv7_time_mix(p, mu, w0, w_up, a0, a_up, g_up, k_k, k_a, r_k, lnx_g, lnx_b, v_first, v0, v_up):
    S, cols = p.shape
    first = v_first is None
    C, W = RW_CHUNK, RW_WIDTH
    row = lambda t: t.reshape(1, -1)
    hid = np.arange(W) // RW_HEAD
    ones_blk = jnp.asarray(hid[:, None] == np.arange(LANES)[None, :], jnp.bfloat16)
    full = lambda shape: pl.BlockSpec(shape, lambda i: (0,) * len(shape))
    tok = lambda width: pl.BlockSpec((C, width), lambda i: (i, 0))
    args = [p, row(mu), row(w0), _pad_rows(w_up, LORA_PAD), row(a0), _pad_rows(a_up, LORA_PAD), g_up,
            row(k_k), row(k_a), row(r_k), row(lnx_g), row(lnx_b), ones_blk]
    specs = [tok(cols), full((1, cols)), full((1, W)), full((LORA_PAD, W)), full((1, W)), full((LORA_PAD, W)),
             full((RW_G_LORA, W)), full((1, W)), full((1, W)), full((1, W)), full((1, W)), full((1, W)),
             full((W, LANES))]
    if first:
        out_shape = (jax.ShapeDtypeStruct((S, W), jnp.float32), jax.ShapeDtypeStruct((S, W), jnp.float32))
        out_specs = (tok(W), tok(W))
    else:
        args += [row(v0), _pad_rows(v_up, LORA_PAD), v_first]
        specs += [full((1, W)), full((LORA_PAD, W)), tok(W)]
        out_shape = jax.ShapeDtypeStruct((S, W), jnp.float32)
        out_specs = tok(W)
    res = pl.pallas_call(
        functools.partial(_rwkv_body, first),
        grid=(S // C,),
        in_specs=specs,
        out_specs=out_specs,
        out_shape=out_shape,
        scratch_shapes=[pltpu.VMEM((RW_HEADS, RW_HEAD, RW_HEAD), jnp.float32),
                        pltpu.VMEM((1, cols), jnp.float32)],
        compiler_params=pltpu.CompilerParams(dimension_semantics=("arbitrary",),
                                             vmem_limit_bytes=48 * 1024 * 1024),
        name="rwkv7_chunked",
    )(*args)
    if first:
        return res[0], res[1]
    return res, v_first


CMP_ROWS = 256


def _compress_body(r_ref, nx_ref, pe_ref, wa_ref, wb_ref, b1_ref, w2_ref, o_ref):
    r = r_ref[...]
    rb, half = r.shape
    row = lax.broadcasted_iota(jnp.int32, r.shape, 0)
    nxt = jnp.where(row == rb - 1, nx_ref[0:1, :], pltpu.roll(r, rb - 1, axis=0))
    pe = pe_ref[...]
    h = (_dot(r + pe[:, :half], wa_ref[...]) + _dot(nxt + pe[:, half:], wb_ref[...]) + b1_ref[...])
    h = 0.5 * h * (1.0 + jnp.tanh(0.7978845608028654 * (h + 0.044715 * (h * h * h))))
    o_ref[...] = _dot(h, w2_ref[...])


def _nsa_compress(t, pe, w1, b1, w2):
    S = t.shape[0]
    G, DH, L, ST = NSA_KV_GROUPS, NSA_HEAD, CMP_LEN, CMP_STRIDE
    nr = S // ST
    rb = min(CMP_ROWS, nr)
    cols = ST * G * DH
    r = t.reshape(nr, cols)
    eye_g = jnp.eye(G, dtype=w1.dtype)
    big = jnp.einsum('ldh,gk->lgdkh', w1.reshape(L, DH, CMP_HIDDEN), eye_g).reshape(L * G * DH, G * CMP_HIDDEN)
    wa, wb = big[:cols], big[cols:]
    pe_big = jnp.broadcast_to(pe[:, None, :], (L, G, DH)).reshape(1, L * G * DH)
    b1_big = jnp.tile(b1, G).reshape(1, G * CMP_HIDDEN)
    w2_big = jnp.einsum('hd,gk->ghkd', w2, eye_g).reshape(G * CMP_HIDDEN, G * DH)
    full = lambda shape: pl.BlockSpec(shape, lambda i: (0, 0))
    return pl.pallas_call(
        _compress_body,
        grid=(nr // rb,),
        in_specs=[pl.BlockSpec((rb, cols), lambda i: (i, 0)),
                  pl.BlockSpec((8, cols), lambda i: (jnp.minimum((i + 1) * (rb // 8), nr // 8 - 1), 0)),
                  full((1, 2 * cols)), full((cols, G * CMP_HIDDEN)), full((cols, G * CMP_HIDDEN)),
                  full((1, G * CMP_HIDDEN)), full((G * CMP_HIDDEN, G * DH))],
        out_specs=pl.BlockSpec((rb, G * DH), lambda i: (i, 0)),
        out_shape=jax.ShapeDtypeStruct((nr, G * DH), jnp.float32),
        compiler_params=pltpu.CompilerParams(dimension_semantics=("parallel",),
                                             vmem_limit_bytes=48 * 1024 * 1024),
        name="nsa_compress",
    )(r, r, pe_big, _bf(wa), _bf(wb), b1_big, _bf(w2_big))


NSA_QB = 128
NSA_KC = 512
NSA_WC = 128
NSA_GATE_PAD = LANES
NSA_PCOLS = NSA_WIDTH + 6 * NSA_KV + NSA_GATE_PAD
NSA_GATE_ROWS = 16
LOG2E = 1.4426950408889634
NSA_CSEG = 256
NSA_HOT = 16


def _softmax_cols(s, mask):
    s = jnp.where(mask, s, NEG_INF)
    m = jnp.max(s, axis=0, keepdims=True)
    p = jnp.where(mask, jnp.exp2(s - m), 0.0)
    l = jnp.sum(p, axis=0, keepdims=True)
    return p * (1.0 / jnp.where(l > 0.0, l, 1.0))


def _nsa_body(*refs):
    n_win = (WINDOW + NSA_QB) // NSA_WC
    qt_ref, kvc_ref, kvct_ref, kq_ref, vst_ref = refs[:5]
    kvw_refs = refs[5:5 + n_win]
    kvwt_refs = refs[5 + n_win:5 + 2 * n_win]
    ovl_ref, gate_ref, o_ref, bias_ref, acc_ref, sca_ref, scb_ref = refs[5 + 2 * n_win:]
    QB, HPG, DH = NSA_QB, NSA_HPG, NSA_HEAD
    f32 = jnp.float32
    qi = pl.program_id(1)
    q0 = qi * QB
    q_t = jnp.transpose(qt_ref[...] * (DH ** -0.5 * LOG2E))
    q_t = jnp.concatenate([q_t[h * DH:(h + 1) * DH] for h in range(HPG)], axis=1)
    qt = _bf(jnp.concatenate([q_t, jnp.zeros_like(q_t)], axis=0))
    ncmp = kvc_ref.shape[0]
    t_row = q0 + lax.broadcasted_iota(jnp.int32, (1, QB), 1)
    head = lambda a, h: a[:, h * QB:(h + 1) * QB]

    nsel = ovl_ref.shape[0]

    def cmp_prefix(nc):
        def run():
            s = jnp.dot(kvc_ref[0:nc, :], qt, preferred_element_type=f32)
            c_end = lax.broadcasted_iota(jnp.int32, (nc, QB), 0) * CMP_STRIDE + (CMP_LEN - 1)
            m_cmp = c_end <= t_row
            kvct = kvct_ref[:, 0:nc]
            p_sum = None
            o_parts = []
            for h in range(HPG):
                p = _softmax_cols(head(s, h), m_cmp)
                o_parts.append(jnp.dot(kvct, _bf(p), preferred_element_type=f32))
                p_sum = p if p_sum is None else p_sum + p
            return tuple(o_parts) + (_dot_exact_lhs(ovl_ref[:, 0:nc], p_sum, 2),)
        return run

    seg = min(NSA_CSEG, ncmp)
    n_seg = ncmp // seg
    visible = (q0 + QB - CMP_LEN) // CMP_STRIDE + 1
    seg_idx = jnp.clip((visible + seg - 1) // seg - 1, 0, n_seg - 1)
    res = lax.switch(seg_idx, [cmp_prefix((k + 1) * seg) for k in range(n_seg)])
    o_cmp, imp = res[:HPG], res[HPG]

    n_row = lax.broadcasted_iota(jnp.int32, (nsel, QB), 0)
    cur = t_row // SEL_BLOCK
    causal = n_row <= cur
    forced = (n_row == 0) | (n_row == cur) | (n_row == cur - 1)
    score = jnp.where(causal & jnp.logical_not(forced), imp, NEG_INF)
    bias = jnp.where(causal & forced, 0.0, NEG_INF)
    for _ in range(min(SEL_TOPN, nsel) - 3):
        best = jnp.max(score, axis=0, keepdims=True)
        first = jnp.min(jnp.where(score == best, n_row, nsel), axis=0, keepdims=True)
        hit = n_row == first
        bias = jnp.where(hit & (best > 0.5 * NEG_INF), 0.0, bias)
        score = jnp.where(hit, -jnp.inf, score)
    bias_ref[...] = bias

    KC = NSA_KC
    n_sub = KC // SEL_BLOCK
    key_row = lax.broadcasted_iota(jnp.int32, (KC, QB), 0)
    acc_ref[...] = jnp.zeros_like(acc_ref)
    last_chunk = kq_ref.shape[0] // KC - 1
    q_rows = qt[0:DH]
    pad_rows = jnp.zeros((LANES - DH - NSA_HOT, HPG * QB), jnp.bfloat16)

    def chunk_scores(idx):
        c = jnp.minimum(idx, last_chunk)
        b8 = bias_ref[pl.ds(pl.multiple_of(c * n_sub, n_sub), n_sub), :]
        b_rows = jnp.concatenate([b8, jnp.zeros((NSA_HOT - n_sub, QB), f32)], axis=0)
        q_aug = jnp.concatenate([q_rows, _bf(jnp.concatenate([b_rows] * HPG, axis=1)), pad_rows], axis=0)
        k0 = pl.multiple_of(c * KC, KC)
        return jnp.dot(kq_ref[pl.ds(k0, KC), :], q_aug, preferred_element_type=f32)

    def consume(sc_ref, idx, ms, ls, causal):
        c = jnp.minimum(idx, last_chunk)
        vt = vst_ref[:, pl.ds(pl.multiple_of(c * KC, KC), KC)]
        sh = sc_ref[...]
        if causal:
            madd = jnp.where(idx * KC + key_row <= t_row, 0.0, NEG_INF)
            sh = sh + jnp.concatenate([madd] * HPG, axis=1)
        new_m = jnp.maximum(ms, jnp.max(sh, axis=0, keepdims=True))
        alpha = jnp.exp2(ms - new_m)
        p = jnp.exp2(sh - new_m)
        new_l = alpha * ls + jnp.sum(p, axis=0, keepdims=True)
        acc_ref[...] = alpha * acc_ref[...] + jnp.dot(vt, _bf(p), preferred_element_type=f32)
        return new_m, new_l

    sca_ref[...] = chunk_scores(0)

    def pair_step(jj, carry):
        ms, ls = carry
        scb_ref[...] = chunk_scores(2 * jj + 1)
        ms, ls = consume(sca_ref, 2 * jj, ms, ls, False)
        sca_ref[...] = chunk_scores(2 * jj + 2)
        return consume(scb_ref, 2 * jj + 1, ms, ls, False)

    init = (jnp.full((1, HPG * QB), NEG_INF, f32), jnp.zeros((1, HPG * QB), f32))
    n_pairs = (q0 // KC) // 2
    ms, ls = lax.fori_loop(0, n_pairs, pair_step, init)
    scb_ref[...] = chunk_scores(2 * n_pairs + 1)
    ms, ls = consume(sca_ref, 2 * n_pairs, ms, ls, True)
    _, l_sel = consume(scb_ref, 2 * n_pairs + 1, ms, ls, True)

    WC = NSA_WC
    s_win, m_win = [], []
    for c in range(n_win):
        s_pos = q0 - WINDOW + c * WC + lax.broadcasted_iota(jnp.int32, (WC, QB), 0)
        dist = t_row - s_pos
        m_win.append((dist >= 0) & (dist < WINDOW) & (s_pos >= 0))
        s_win.append(jnp.dot(kvw_refs[c][...], qt, preferred_element_type=f32))
    m_all = jnp.concatenate([jnp.concatenate(m_win, axis=0)] * HPG, axis=1)
    s_all = jnp.where(m_all, jnp.concatenate(s_win, axis=0), NEG_INF)
    p_win = jnp.exp2(s_all - jnp.max(s_all, axis=0, keepdims=True))
    l_win = jnp.sum(p_win, axis=0, keepdims=True)
    p_win = _bf(p_win)
    o_win = None
    for c in range(n_win):
        t = jnp.dot(kvwt_refs[c][DH:, :], p_win[c * WC:(c + 1) * WC], preferred_element_type=f32)
        o_win = t if o_win is None else o_win + t
    o_win = o_win * (1.0 / l_win)

    gate = _sigmoid(gate_ref[...])
    outs = []
    for h in range(HPG):
        o_sel = head(acc_ref[...], h) * (1.0 / head(l_sel, h))
        gr = lambda b: gate[h * 3 + b:h * 3 + b + 1, :]
        outs.append(gr(0) * o_cmp[h][DH:, :] + gr(1) * o_sel + gr(2) * head(o_win, h))
    o_ref[...] = jnp.transpose(jnp.concatenate(outs, axis=0))


def _nsa(p, cmp_pe, cmp_w1, cmp_b1, cmp_w2):
    S = p.shape[0]
    G, HPG, DH, QB, WC = NSA_KV_GROUPS, NSA_HPG, NSA_HEAD, NSA_QB, NSA_WC
    W, KV = NSA_WIDTH, NSA_KV
    nqb = S // QB
    n_win = (WINDOW + QB) // WC
    kc, vc, ks, vs, kw, vw = (p[:, W + i * KV:W + (i + 1) * KV] for i in range(6))
    gate = p[:, W + 6 * KV:W + 6 * KV + 3 * NSA_HEADS]
    k_cmp = _nsa_compress(kc, cmp_pe[0], cmp_w1[0], cmp_b1[0], cmp_w2[0])
    v_cmp = _nsa_compress(vc, cmp_pe[1], cmp_w1[1], cmp_b1[1], cmp_w2[1])
    pack = lambda k, v: _bf(jnp.concatenate([k.reshape(-1, G, DH), v.reshape(-1, G, DH)], -1)).transpose(1, 0, 2)
    kvc, kvw = pack(k_cmp, v_cmp), pack(kw, vw)
    hot = jax.nn.one_hot((jnp.arange(S) // SEL_BLOCK) % (NSA_KC // SEL_BLOCK), LANES - DH, dtype=jnp.bfloat16)
    kq = jnp.concatenate([_bf(ks.reshape(S, G, DH)), jnp.broadcast_to(hot[:, None, :], (S, G, LANES - DH))], -1)
    kq = kq.transpose(1, 0, 2)
    vst = _bf(vs.reshape(S, G, DH)).transpose(1, 2, 0)
    tr = lambda a: a.transpose(0, 2, 1)
    gate_t = jnp.pad(gate.reshape(S, G, 3 * HPG).transpose(1, 2, 0), ((0, 0), (0, NSA_GATE_ROWS - 3 * HPG), (0, 0)))
    ncmp, nsel = S // CMP_STRIDE, S // SEL_BLOCK
    c_start = np.arange(ncmp)[None, :] * CMP_STRIDE
    s_start = np.arange(nsel)[:, None] * SEL_BLOCK
    overlap_t = jnp.asarray((c_start < s_start + SEL_BLOCK) & (c_start + CMP_LEN > s_start), jnp.bfloat16)
    win_blk = lambda c: (lambda g, i: (g, jnp.maximum(i - WINDOW // WC + c, 0), 0))
    win_blk_t = lambda c: (lambda g, i: (g, 0, jnp.maximum(i - WINDOW // WC + c, 0)))
    in_specs = ([pl.BlockSpec((QB, HPG * DH), lambda g, i: (i, g)),
                 pl.BlockSpec((None, ncmp, LANES), lambda g, i: (g, 0, 0)),
                 pl.BlockSpec((None, LANES, ncmp), lambda g, i: (g, 0, 0)),
                 pl.BlockSpec((None, S, LANES), lambda g, i: (g, 0, 0)),
                 pl.BlockSpec((None, DH, S), lambda g, i: (g, 0, 0))]
                + [pl.BlockSpec((None, WC, LANES), win_blk(c)) for c in range(n_win)]
                + [pl.BlockSpec((None, LANES, WC), win_blk_t(c)) for c in range(n_win)]
                + [pl.BlockSpec((nsel, ncmp), lambda g, i: (0, 0)),
                   pl.BlockSpec((None, NSA_GATE_ROWS, QB), lambda g, i: (g, 0, i))])
    kvwt = tr(kvw)
    return pl.pallas_call(
        _nsa_body,
        grid=(G, nqb),
        in_specs=in_specs,
        out_specs=pl.BlockSpec((QB, HPG * DH), lambda g, i: (i, g)),
        out_shape=jax.ShapeDtypeStruct((S, W), jnp.float32),
        scratch_shapes=[pltpu.VMEM((nsel, QB), jnp.float32),
                        pltpu.VMEM((DH, HPG * QB), jnp.float32),
                        pltpu.VMEM((NSA_KC, HPG * QB), jnp.float32),
                        pltpu.VMEM((NSA_KC, HPG * QB), jnp.float32)],
        compiler_params=pltpu.CompilerParams(dimension_semantics=("arbitrary", "arbitrary"),
                                             vmem_limit_bytes=56 * 1024 * 1024),
        name="nsa_attention",
    )(p, kvc, tr(kvc), kq, vst, *([kvw] * n_win), *([kvwt] * n_win), overlap_t, gate_t)


def _retention_body(p_ref, cc_ref, ss_ref, inner_ref, qd_ref, kd_ref, cd_ref, o_ref, r_ref):
    H, DK, W = RET_HEADS, RET_HEAD, RET_WIDTH
    f32 = jnp.float32

    @pl.when(pl.program_id(0) == 0)
    def _():
        r_ref[...] = jnp.zeros_like(r_ref)

    cc, ss = cc_ref[...], ss_ref[...]
    hh = range(H)
    col = lambda part, h: p_ref[:, part * W + h * DK:part * W + (h + 1) * DK].astype(f32)
    rot = lambda t: t * cc + pltpu.roll(t, DK // 2, axis=1) * ss
    q = [rot(col(0, h)) for h in hh]
    k = [rot(col(1, h)) * (DK ** -0.5) for h in hh]
    v = [col(2, h) for h in hh]
    r = [r_ref[h] for h in hh]
    att = [_dot(q[h], k[h], _NT) * inner_ref[h] for h in hh]
    o = [_dot(att[h], v[h]) + _dot(q[h], r[h]) * qd_ref[h] for h in hh]
    upd = [_dot(k[h] * kd_ref[h], v[h], _TN) for h in hh]
    for h in hh:
        r_ref[h] = r[h] * cd_ref[h] + upd[h]
        mu = jnp.mean(o[h], axis=-1, keepdims=True)
        oc = o[h] - mu
        var = jnp.mean(oc * oc, axis=-1, keepdims=True)
        g = col(3, h)
        o_ref[:, h * DK:(h + 1) * DK] = (g * _sigmoid(g)) * (oc * lax.rsqrt(var + RET_GN_EPS))


def _retention(p):
    S = p.shape[0]
    H, DK, C = RET_HEADS, RET_HEAD, RET_CHUNK
    f32 = jnp.float32
    inv_freq = 1.0 / (RET_THETA ** jnp.linspace(0.0, 1.0, DK // 2))
    ang = jnp.arange(S, dtype=f32)[:, None] * inv_freq[None, :]
    cos, sin = jnp.cos(ang), jnp.sin(ang)
    cc = jnp.concatenate([cos, cos], -1)
    ss = jnp.concatenate([-sin, sin], -1)
    log_g = jnp.log1p(-jnp.exp2(-5.0 - jnp.arange(H, dtype=f32)))
    i = jnp.arange(C, dtype=f32)
    diff = i[:, None] - i[None, :]
    inner_decay = jnp.where(diff >= 0, jnp.exp(log_g[:, None, None] * jnp.maximum(diff, 0.0)), 0.0)
    lanes = lambda t: jnp.broadcast_to(t, (H, C, DK))
    q_decay = lanes(jnp.exp(log_g[:, None] * (i + 1.0))[..., None])
    k_decay = lanes(jnp.exp(log_g[:, None] * (C - 1.0 - i))[..., None])
    c_decay = lanes(jnp.exp(log_g * C)[:, None, None])
    tok = lambda width: pl.BlockSpec((C, width), lambda n: (n, 0))
    const = pl.BlockSpec((H, C, DK), lambda n: (0, 0, 0))
    return pl.pallas_call(
        _retention_body,
        grid=(S // C,),
        in_specs=[tok(p.shape[1]), tok(DK), tok(DK), const, const, const, const],
        out_specs=tok(RET_WIDTH),
        out_shape=jax.ShapeDtypeStruct((S, RET_WIDTH), f32),
        scratch_shapes=[pltpu.VMEM((H, DK, DK), f32)],
        compiler_params=pltpu.CompilerParams(dimension_semantics=("arbitrary",),
                                             vmem_limit_bytes=48 * 1024 * 1024),
        name="retention_chunked",
    )(p, cc, ss, inner_decay, q_decay, k_decay, c_decay)


MOE_ROWS = 256


def _expert_body(blk_e_ref, n_used_ref, x_ref, wg_ref, wu_ref, wd_ref, o_ref):
    i = pl.program_id(0)

    @pl.when(i < n_used_ref[0])
    def _():
        x = x_ref[...]
        hg = _dot(x, wg_ref[...])
        h = (hg * _sigmoid(hg)) * _dot(x, wu_ref[...])
        o_ref[...] = _dot(h, wd_ref[...])

    @pl.when(i >= n_used_ref[0])
    def _():
        o_ref[...] = jnp.zeros_like(o_ref)


def _moe_experts(x_slots, blk_expert, n_used, layer, w_gate, w_up, w_down):
    n_slots, D = x_slots.shape
    F = w_gate.shape[3]
    mb = MOE_ROWS
    grid_spec = pltpu.PrefetchScalarGridSpec(
        num_scalar_prefetch=2,
        grid=(n_slots // mb,),
        in_specs=[pl.BlockSpec((mb, D), lambda i, be, nu: (i, 0)),
                  pl.BlockSpec((None, None, D, F), lambda i, be, nu: (layer, be[i], 0, 0)),
                  pl.BlockSpec((None, None, D, F), lambda i, be, nu: (layer, be[i], 0, 0)),
                  pl.BlockSpec((None, None, F, D), lambda i, be, nu: (layer, be[i], 0, 0))],
        out_specs=pl.BlockSpec((mb, D), lambda i, be, nu: (i, 0)))
    return pl.pallas_call(
        _expert_body,
        grid_spec=grid_spec,
        out_shape=jax.ShapeDtypeStruct((n_slots, D), jnp.float32),
        compiler_params=pltpu.CompilerParams(dimension_semantics=("arbitrary",),
                                             vmem_limit_bytes=56 * 1024 * 1024),
        name="moe_experts",
    )(blk_expert, n_used, x_slots, w_gate, w_up, w_down)


def _hier_moe(x1b, logits, layer, w_gate, w_up, w_down):
    T, D = x1b.shape
    f32 = jnp.float32
    grp_logits = logits[:, :N_GROUPS]
    grp = jnp.argmax(grp_logits, axis=-1)
    grp_w = jnp.take_along_axis(jax.nn.softmax(grp_logits, -1), grp[:, None], axis=-1)
    exp_logits = logits[:, N_GROUPS:N_GROUPS + N_EXPERTS].reshape(T, N_GROUPS, EXPERTS_PER_GROUP)
    in_grp = jnp.take_along_axis(exp_logits, grp[:, None, None], axis=1)[:, 0]
    e_iota = jnp.arange(EXPERTS_PER_GROUP, dtype=jnp.int32)[None, :]
    i1 = jnp.argmax(in_grp, axis=-1).astype(jnp.int32)
    rest = jnp.where(e_iota == i1[:, None], -jnp.inf, in_grp)
    i2 = jnp.argmax(rest, axis=-1).astype(jnp.int32)
    top_idx = jnp.stack([i1, i2], -1)
    top_val = jnp.stack([jnp.max(in_grp, -1), jnp.max(rest, -1)], -1)
    gate_w = (jax.nn.softmax(top_val, -1) * grp_w).reshape(-1)
    expert = (grp[:, None] * EXPERTS_PER_GROUP + top_idx).reshape(-1).astype(jnp.int32)
    token = jnp.repeat(jnp.arange(T, dtype=jnp.int32), EXPERT_TOPK)
    n_assign = T * EXPERT_TOPK
    mb = MOE_ROWS
    onehot = (expert[:, None] == jnp.arange(N_EXPERTS, dtype=jnp.int32)[None, :]).astype(jnp.int32)
    csum = jnp.cumsum(onehot, axis=0)
    pos = jnp.take_along_axis(csum, expert[:, None], axis=1)[:, 0] - 1
    counts = csum[-1]
    padded = (counts + mb - 1) // mb * mb
    pad_end = jnp.cumsum(padded)
    slot = (pad_end - padded)[expert] + pos
    n_blocks = -(-n_assign // mb) + N_EXPERTS
    n_slots = n_blocks * mb
    slot_tok = jnp.full((n_slots,), T, jnp.int32).at[slot].set(token)
    blk_start = jnp.arange(n_blocks, dtype=jnp.int32) * mb
    blk_expert = jnp.minimum(jnp.sum((pad_end[None, :] <= blk_start[:, None]).astype(jnp.int32), axis=1),
                             N_EXPERTS - 1)
    n_used = (pad_end[-1:] // mb).astype(jnp.int32)
    x_slots = x1b[jnp.minimum(slot_tok, T - 1)]
    y = _moe_experts(x_slots, blk_expert, n_used, layer, w_gate, w_up, w_down)
    slot2 = slot.reshape(T, EXPERT_TOPK)
    gate2 = gate_w.reshape(T, EXPERT_TOPK)
    return (y[slot2[:, 0]], y[slot2[:, 1]]), (gate2[:, 0:1], gate2[:, 1:2])


def kernel(x, w_in_first, w_in_deep, rw_mu_first, rw_mu_deep, rw_w0, rw_w_up, rw_a0, rw_a_up, rw_v0, rw_v_up, rw_g_up, rw_k_k, rw_k_a, rw_r_k, rw_lnx_g, rw_lnx_b, nsa_cmp_pe, nsa_cmp_w1, nsa_cmp_b1, nsa_cmp_w2, w_br_rw, w_br_nsa, w_br_ret, w_out, ln1_g, ln1_b, moe_w_grp, moe_b_grp, moe_w_exp, moe_b_exp, moe_w_gate, moe_w_up, moe_w_down, ln2_g, ln2_b):
    B, S, D = x.shape
    assert B == 1
    x = x.reshape(S, D)
    xb = _bf(x)
    v_first = None
    for l in range(DEPTH):
        first = l == 0
        w_in = w_in_first if first else w_in_deep[l - 1]
        mu = rw_mu_first if first else rw_mu_deep[l - 1]
        rw_cols = RW_COLS_FIRST if first else RW_COLS_DEEP
        rw_sizes = RW_BASE_SIZES if first else RW_BASE_SIZES + (RW_V_LORA,)
        rw_padded = RW_PAD_SIZES if first else RW_PAD_SIZES + (LORA_PAD,)
        rw_tn = 512 if first else 768
        tail = -sum(rw_padded) % rw_tn
        mu_p = jnp.pad(_pad_cols(mu, rw_sizes, rw_padded), (0, tail))
        sizes = rw_sizes + (0, NSA_COLS, RET_COLS, 3 * D_MODEL)
        padded = rw_padded + (tail, NSA_PCOLS, RET_COLS, 3 * D_MODEL)
        wt = lax.optimization_barrier(jnp.transpose(w_in))
        rows, r0 = [], 0
        for n, n_pad in zip(sizes, padded):
            rows.append(jnp.pad(wt[r0:r0 + n], ((0, n_pad - n), (0, 0))))
            r0 += n
        wt = lax.optimization_barrier(_bf(jnp.concatenate(rows, 0)))
        w_all = jnp.transpose(wt)
        c0 = sum(rw_padded) + tail
        w_rw = w_all[:, :c0]
        w_nsa = w_all[:, c0:c0 + NSA_PCOLS]
        c0 += NSA_PCOLS
        w_ret = w_all[:, c0:c0 + RET_COLS]
        w_gates = w_all[:, c0 + RET_COLS:]
        p_rw = _matmul(xb, w_rw, tn=rw_tn)
        p_nsa = _matmul(xb, w_nsa, tn=NSA_PCOLS // 3)
        p_ret = _matmul(xb, w_ret, tn=512)
        gates = _matmul(xb, w_gates, tn=512, out_dtype=jnp.bfloat16)
        y_rw, v_first = _rwkv7_time_mix(p_rw, mu_p, rw_w0[l], rw_w_up[l], rw_a0[l],
                                        rw_a_up[l], rw_g_up[l], rw_k_k[l], rw_k_a[l], rw_r_k[l], rw_lnx_g[l],
                                        rw_lnx_b[l], v_first,
                                        None if first else rw_v0[l - 1], None if first else rw_v_up[l - 1])
        y_nsa = _nsa(p_nsa, nsa_cmp_pe[l], nsa_cmp_w1[l], nsa_cmp_b1[l], nsa_cmp_w2[l])
        y_ret = _retention(p_ret)
        w_br = _bf(jnp.stack([w_br_rw[l], w_br_nsa[l], w_br_ret[l]]))
        merged = _merge_branches((y_rw, y_nsa, y_ret), gates, w_br)
        w_router = jnp.pad(jnp.concatenate([moe_w_grp[l], moe_w_exp[l]], 1),
                           ((0, 0), (0, ROUTER_PAD - N_GROUPS - N_EXPERTS)))
        b_router = jnp.pad(jnp.concatenate([moe_b_grp[l], moe_b_exp[l]]),
                           (0, ROUTER_PAD - N_GROUPS - N_EXPERTS)).reshape(1, ROUTER_PAD)
        x1, x1b, logits = _out_ln_router(merged, _bf(w_out[l]), x, ln1_g[l], ln1_b[l], _bf(w_router), b_router)
        ys, ws = _hier_moe(x1b, logits, l, moe_w_gate, moe_w_up, moe_w_down)
        x, xb = _moe_combine_ln(x1, ys, ws, ln2_g[l], ln2_b[l])
    return x.reshape(B, S, D)
```

```python
import functools

import jax
import jax.numpy as jnp
from jax import lax
import numpy as np
from jax.experimental import pallas as pl
from jax.experimental.pallas import tpu as pltpu

D_MODEL = 2048
DEPTH = 2

RW_HEADS = 16
RW_HEAD = 64
RW_WIDTH = RW_HEADS * RW_HEAD
RW_DECAY_LORA = 96
RW_A_LORA = 96
RW_V_LORA = 64
RW_G_LORA = 256
RW_GN_EPS = 64e-5
RW_BASE_SIZES = (RW_WIDTH, RW_WIDTH, RW_WIDTH, RW_DECAY_LORA, RW_A_LORA, RW_G_LORA)
RW_COLS_FIRST = sum(RW_BASE_SIZES)
RW_COLS_DEEP = RW_COLS_FIRST + RW_V_LORA

NSA_HEADS = 16
NSA_KV_GROUPS = 4
NSA_HPG = NSA_HEADS // NSA_KV_GROUPS
NSA_HEAD = 64
NSA_WIDTH = NSA_HEADS * NSA_HEAD
NSA_KV = NSA_KV_GROUPS * NSA_HEAD
CMP_LEN = 32
CMP_STRIDE = 16
CMP_HIDDEN = 128
SEL_BLOCK = 64
SEL_TOPN = 16
WINDOW = 512
Q_BLOCK = 128
FORCE_BONUS = 1e4
NEG_INF = -1e30
NSA_SIZES = (NSA_WIDTH,) + (NSA_KV,) * 6 + (3 * NSA_HEADS,)
NSA_COLS = sum(NSA_SIZES)

RET_HEADS = 8
RET_HEAD = 128
RET_WIDTH = RET_HEADS * RET_HEAD
RET_CHUNK = 128
RET_THETA = 10000.0
RET_GN_EPS = 1e-5
RET_SIZES = (RET_WIDTH,) * 4
RET_COLS = sum(RET_SIZES)

N_GROUPS = 4
EXPERTS_PER_GROUP = 8
N_EXPERTS = N_GROUPS * EXPERTS_PER_GROUP
EXPERT_FF = 512
EXPERT_TOPK = 2
MOE_BLOCK = 128

DN_ALPHA = (2 * DEPTH) ** 0.25
LN_EPS = 1e-5

LANES = 128


MM_ROWS = 1024


def _matmul_body(a_ref, b_ref, o_ref):
    o_ref[...] = jnp.dot(a_ref[...], b_ref[...], preferred_element_type=jnp.float32).astype(o_ref.dtype)


def _matmul(a, b, tn, out_dtype=jnp.float32):
    m, k = a.shape
    n = b.shape[1]
    tm = min(MM_ROWS, m)
    return pl.pallas_call(
        _matmul_body,
        grid=(m // tm, n // tn),
        in_specs=[pl.BlockSpec((tm, k), lambda i, j: (i, 0)),
                  pl.BlockSpec((k, tn), lambda i, j: (0, j))],
        out_specs=pl.BlockSpec((tm, tn), lambda i, j: (i, j)),
        out_shape=jax.ShapeDtypeStruct((m, n), out_dtype),
        compiler_params=pltpu.CompilerParams(
            dimension_semantics=("parallel", "parallel"),
            vmem_limit_bytes=48 * 1024 * 1024),
        name="dense_matmul",
    )(a, b)


MERGE_ROWS = 512
MERGE_COLS = 512


def _merge_body(y0_ref, y1_ref, y2_ref, g0_ref, g1_ref, g2_ref, w_ref, o_ref):
    acc = None
    for b, (y_ref, g_ref) in enumerate(((y0_ref, g0_ref), (y1_ref, g1_ref), (y2_ref, g2_ref))):
        t = _sigmoid(g_ref[...].astype(jnp.float32)) * _dot(y_ref[...], w_ref[b])
        acc = t if acc is None else acc + t
    o_ref[...] = acc.astype(o_ref.dtype)


def _merge_branches(ys, gates, w_br):
    S, wb = ys[0].shape
    D = w_br.shape[2]
    tm, tn = min(MERGE_ROWS, S), MERGE_COLS
    nj = D // tn
    y_spec = pl.BlockSpec((tm, wb), lambda i, j: (i, 0))
    g_spec = lambda b: pl.BlockSpec((tm, tn), lambda i, j: (i, b * nj + j))
    return pl.pallas_call(
        _merge_body,
        grid=(S // tm, nj),
        in_specs=[y_spec, y_spec, y_spec, g_spec(0), g_spec(1), g_spec(2),
                  pl.BlockSpec((3, wb, tn), lambda i, j: (0, 0, j))],
        out_specs=pl.BlockSpec((tm, tn), lambda i, j: (i, j)),
        out_shape=jax.ShapeDtypeStruct((S, D), jnp.bfloat16),
        compiler_params=pltpu.CompilerParams(dimension_semantics=("parallel", "parallel"),
                                             vmem_limit_bytes=48 * 1024 * 1024),
        name="merge_branches",
    )(*ys, gates, gates, gates, w_br)


LN_ROWS = 256
ROUTER_PAD = LANES


def _ln_rows(z, g, b):
    mu = jnp.mean(z, axis=-1, keepdims=True)
    zc = z - mu
    var = jnp.mean(zc * zc, axis=-1, keepdims=True)
    return zc * lax.rsqrt(var + LN_EPS) * g + b


def _out_ln_body(m_ref, w_ref, x_ref, g_ref, b_ref, wr_ref, br_ref, x1_ref, lg_ref):
    z = DN_ALPHA * x_ref[...] + jnp.dot(m_ref[...], w_ref[...], preferred_element_type=jnp.float32)
    x1 = _ln_rows(z, g_ref[...], b_ref[...])
    x1_ref[...] = x1
    lg_ref[...] = jnp.dot(_bf(x1), wr_ref[...], preferred_element_type=jnp.float32) + br_ref[...]


def _out_ln_router(merged, w_out, x, ln_g, ln_b, w_router, b_router):
    S, D = x.shape
    tm = min(LN_ROWS, S)
    row = pl.BlockSpec((tm, D), lambda i: (i, 0))
    full = lambda shape: pl.BlockSpec(shape, lambda i: (0, 0))
    return pl.pallas_call(
        _out_ln_body,
        grid=(S // tm,),
        in_specs=[row, full((D, D)), row, full((1, D)), full((1, D)), full((D, ROUTER_PAD)), full((1, ROUTER_PAD))],
        out_specs=(row, pl.BlockSpec((tm, ROUTER_PAD), lambda i: (i, 0))),
        out_shape=(jax.ShapeDtypeStruct((S, D), jnp.float32), jax.ShapeDtypeStruct((S, ROUTER_PAD), jnp.float32)),
        compiler_params=pltpu.CompilerParams(dimension_semantics=("parallel",),
                                             vmem_limit_bytes=48 * 1024 * 1024),
        name="out_proj_ln_router",
    )(merged, w_out, x, ln_g.reshape(1, D), ln_b.reshape(1, D), w_router, b_router)


def _moe_combine_ln_body(x_ref, y0_ref, y1_ref, w0_ref, w1_ref, g_ref, b_ref, o_ref, ob_ref):
    moe = y0_ref[...] * w0_ref[...] + y1_ref[...] * w1_ref[...]
    y = _ln_rows(DN_ALPHA * x_ref[...] + moe, g_ref[...], b_ref[...])
    o_ref[...] = y
    ob_ref[...] = _bf(y)


def _moe_combine_ln(x, ys, ws, ln_g, ln_b):
    S, D = x.shape
    tm = min(LN_ROWS, S)
    row = pl.BlockSpec((tm, D), lambda i: (i, 0))
    col = pl.BlockSpec((tm, 1), lambda i: (i, 0))
    full = pl.BlockSpec((1, D), lambda i: (0, 0))
    return pl.pallas_call(
        _moe_combine_ln_body,
        grid=(S // tm,),
        in_specs=[row, row, row, col, col, full, full],
        out_specs=(row, row),
        out_shape=(jax.ShapeDtypeStruct((S, D), jnp.float32), jax.ShapeDtypeStruct((S, D), jnp.bfloat16)),
        compiler_params=pltpu.CompilerParams(dimension_semantics=("parallel",)),
        name="moe_combine_ln",
    )(x, ys[0], ys[1], ws[0], ws[1], ln_g.reshape(1, D), ln_b.reshape(1, D))


def _split(p, sizes):
    idx = [int(i) for i in np.cumsum(sizes)[:-1]]
    return jnp.split(p, idx, axis=-1)


RW_CHUNK = 64
LORA_PAD = LANES
RW_OFF_XW = 3 * RW_WIDTH
RW_OFF_XA = RW_OFF_XW + LORA_PAD
RW_OFF_XG = RW_OFF_XA + LORA_PAD
RW_OFF_XV = RW_OFF_XG + RW_G_LORA
RW_PAD_SIZES = (RW_WIDTH, RW_WIDTH, RW_WIDTH, LORA_PAD, LORA_PAD, RW_G_LORA)

_NT = (((1,), (1,)), ((), ()))
_TN = (((0,), (0,)), ((), ()))


def _bf(x):
    return x.astype(jnp.bfloat16)


def _dot(a, b, dims=None):
    if dims is None:
        return jnp.dot(_bf(a), _bf(b), preferred_element_type=jnp.float32)
    return lax.dot_general(_bf(a), _bf(b), dims, preferred_element_type=jnp.float32)


def _split3(x):
    h = _bf(x)
    r = x - h.astype(jnp.float32)
    m = _bf(r)
    l = _bf(r - m.astype(jnp.float32))
    return h, m, l


def _dot_exact_rhs(a, b_bf16, passes):
    acc = None
    for p in _split3(a)[:passes]:
        t = jnp.dot(p, b_bf16, preferred_element_type=jnp.float32)
        acc = t if acc is None else acc + t
    return acc


def _dot_exact_lhs(a_bf16, b, passes):
    acc = None
    for p in _split3(b)[:passes]:
        t = jnp.dot(a_bf16, p, preferred_element_type=jnp.float32)
        acc = t if acc is None else acc + t
    return acc


def _sigmoid(x):
    return 1.0 / (1.0 + jnp.exp(-x))


def _softplus(x):
    return jnp.maximum(x, 0.0) + jnp.log(1.0 + jnp.exp(-jnp.abs(x)))


def _rwkv_body(first, *refs):
    if first:
        (p_ref, mu_ref, w0_ref, wup_ref, a0_ref, aup_ref, gup_ref, kk_ref, ka_ref, rk_ref,
         lg_ref, lb_ref, ones_ref, y_ref, vf_out_ref, st_ref, prev_ref) = refs
    else:
        (p_ref, mu_ref, w0_ref, wup_ref, a0_ref, aup_ref, gup_ref, kk_ref, ka_ref, rk_ref,
         lg_ref, lb_ref, ones_ref, v0_ref, vup_ref, vf_in_ref, y_ref, st_ref, prev_ref) = refs
    C, H, N, W = RW_CHUNK, RW_HEADS, RW_HEAD, RW_WIDTH
    f32 = jnp.float32
    i = pl.program_id(0)

    @pl.when(i == 0)
    def _():
        st_ref[...] = jnp.zeros_like(st_ref)
        prev_ref[...] = jnp.zeros_like(prev_ref)

    z = p_ref[...]
    row = lax.broadcasted_iota(jnp.int32, z.shape, 0)
    prev = jnp.where(row == 0, prev_ref[...], pltpu.roll(z, 1, axis=0))
    prev_ref[...] = z[C - 1:C, :]
    xs = z + (prev - z) * mu_ref[...]

    r = xs[:, 0:W]
    k = xs[:, W:2 * W]
    v = xs[:, 2 * W:3 * W]
    xw = xs[:, RW_OFF_XW:RW_OFF_XW + LORA_PAD]
    xa = xs[:, RW_OFF_XA:RW_OFF_XA + LORA_PAD]
    xg = xs[:, RW_OFF_XG:RW_OFF_XG + RW_G_LORA]
    if first:
        vf_out_ref[...] = v
    else:
        xv = xs[:, RW_OFF_XV:RW_OFF_XV + LORA_PAD]
        v = v + (vf_in_ref[...] - v) * _sigmoid(v0_ref[...] + _dot(xv, vup_ref[...]))

    w_log = -_softplus(-(w0_ref[...] + _dot(jnp.tanh(xw), wup_ref[...]))) - 0.5
    logd = -jnp.exp(w_log)
    a = _sigmoid(a0_ref[...] + _dot(xa, aup_ref[...]))
    g = _dot(_sigmoid(xg), gup_ref[...])
    hsum = ones_ref[...]

    def head_sums(t):
        acc = None
        for piece in _split3(_dot_exact_rhs(t, hsum, 2)):
            u = lax.dot_general(piece, hsum, _NT, preferred_element_type=f32)
            acc = u if acc is None else acc + u
        return acc

    kk = k * kk_ref[...]
    kk_n2 = head_sums(kk * kk)
    kk = kk / jnp.maximum(jnp.sqrt(kk_n2), 1e-12)
    k = k * (1.0 + (a - 1.0) * ka_ref[...])
    b = kk * a
    bonus = head_sums(r * k * rk_ref[...]) * v

    tr = lax.broadcasted_iota(jnp.int32, (C, C), 0)
    tc = lax.broadcasted_iota(jnp.int32, (C, C), 1)
    low_incl = tr >= tc
    low_strict = tr > tc
    cum = _dot_exact_lhs(low_incl.astype(jnp.bfloat16), logd, 3)
    total = cum[C - 1:C, :]
    e_in = jnp.exp(cum)
    e_ex = jnp.exp(cum - logd)
    e_neg = jnp.exp(-cum)
    e_rem = jnp.exp(total - cum)
    kkd = kk * e_ex
    kh = k * e_neg
    bh = b * e_neg
    rd = r * e_in
    khg = k * e_rem
    bhg = b * e_rem
    gam = jnp.exp(total)

    eye = (tr == tc).astype(f32)
    heads = range(H)
    hs = lambda t: [t[:, h * N:(h + 1) * N] for h in heads]
    kkd_h, kh_h, bh_h, rd_h, v_h, khg_h, bhg_h = (hs(t) for t in (kkd, kh, bh, rd, v, khg, bhg))
    st = [st_ref[h] for h in heads]
    gm = [_dot(jnp.concatenate([kkd_h[h], rd_h[h]], axis=0),
               jnp.concatenate([kh_h[h], bh_h[h]], axis=0), _NT) for h in heads]
    a_k = [jnp.where(low_strict, gm[h][0:C, 0:C], 0.0) for h in heads]
    a_b = [jnp.where(low_strict, gm[h][0:C, C:2 * C], 0.0) for h in heads]
    b_k = [jnp.where(low_incl, gm[h][C:2 * C, 0:C], 0.0) for h in heads]
    b_b = [jnp.where(low_incl, gm[h][C:2 * C, C:2 * C], 0.0) for h in heads]
    rhs = [_dot(kkd_h[h], st[h]) + _dot(a_k[h], v_h[h]) for h in heads]
    y0 = [_dot(rd_h[h], st[h]) + _dot(b_k[h], v_h[h]) for h in heads]
    t_inv = [eye - jnp.where((tr == tc + 1) & (tr % 2 == 1), a_b[h], 0.0) for h in heads]
    m = 2
    while m < C:
        rb, cb = tr // m, tc // m
        sib = (rb == cb + 1) & (rb % 2 == 1)
        tmp = [_dot(jnp.where(sib, a_b[h], 0.0), t_inv[h]) for h in heads]
        t_inv = [t_inv[h] - _dot(t_inv[h], tmp[h]) for h in heads]
        m *= 2
    zz = [_dot(t_inv[h], rhs[h]) for h in heads]
    ys = [y0[h] - _dot(b_b[h], zz[h]) for h in heads]
    upd = [_dot(khg_h[h], v_h[h], _TN) - _dot(bhg_h[h], zz[h], _TN) for h in heads]
    for h in heads:
        st_ref[h] = st[h] * jnp.transpose(gam[:, h * N:(h + 1) * N]) + upd[h]
    y = jnp.concatenate(ys, axis=1)

    inv_n = 1.0 / N
    mean = head_sums(y) * inv_n
    yc = y - mean
    var = head_sums(yc * yc) * inv_n
    yn = yc * lax.rsqrt(var + RW_GN_EPS) * lg_ref[...] + lb_ref[...]
    y_ref[...] = (yn + bonus) * g


def _pad_rows(w, rows):
    return jnp.pad(w, ((0, rows - w.shape[0]), (0, 0)))


def _pad_cols(p, sizes, padded):
    parts = _split(p, sizes)
    return jnp.concatenate([jnp.pad(t, [(0, 0)] * (t.ndim - 1) + [(0, n - t.shape[-1])])
                            for t, n in zip(parts, padded)], -1)


def _rwkv7_time_mix(p, mu, w0, w_up, a0, a_up, g_up, k_k, k_a, r_k, lnx_g, lnx_b, v_first, v0, v_up):
    S, cols = p.shape
    first = v_first is None
    C, W = RW_CHUNK, RW_WIDTH
    row = lambda t: t.reshape(1, -1)
    hid = np.arange(W) // RW_HEAD
    ones_blk = jnp.asarray(hid[:, None] == np.arange(LANES)[None, :], jnp.bfloat16)
    full = lambda shape: pl.BlockSpec(shape, lambda i: (0,) * len(shape))
    tok = lambda width: pl.BlockSpec((C, width), lambda i: (i, 0))
    args = [p, row(mu), row(w0), _pad_rows(w_up, LORA_PAD), row(a0), _pad_rows(a_up, LORA_PAD), g_up,
            row(k_k), row(k_a), row(r_k), row(lnx_g), row(lnx_b), ones_blk]
    specs = [tok(cols), full((1, cols)), full((1, W)), full((LORA_PAD, W)), full((1, W)), full((LORA_PAD, W)),
             full((RW_G_LORA, W)), full((1, W)), full((1, W)), full((1, W)), full((1, W)), full((1, W)),
             full((W, LANES))]
    if first:
        out_shape = (jax.ShapeDtypeStruct((S, W), jnp.float32), jax.ShapeDtypeStruct((S, W), jnp.float32))
        out_specs = (tok(W), tok(W))
    else:
        args += [row(v0), _pad_rows(v_up, LORA_PAD), v_first]
        specs += [full((1, W)), full((LORA_PAD, W)), tok(W)]
        out_shape = jax.ShapeDtypeStruct((S, W), jnp.float32)
        out_specs = tok(W)
    res = pl.pallas_call(
        functools.partial(_rwkv_body, first),
        grid=(S // C,),
        in_specs=specs,
        out_specs=out_specs,
        out_shape=out_shape,
        scratch_shapes=[pltpu.VMEM((RW_HEADS, RW_HEAD, RW_HEAD), jnp.float32),
                        pltpu.VMEM((1, cols), jnp.float32)],
        compiler_params=pltpu.CompilerParams(dimension_semantics=("arbitrary",),
                                             vmem_limit_bytes=48 * 1024 * 1024),
        name="rwkv7_chunked",
    )(*args)
    if first:
        return res[0], res[1]
    return res, v_first


CMP_ROWS = 256


def _compress_body(r_ref, nx_ref, pe_ref, wa_ref, wb_ref, b1_ref, w2_ref, o_ref):
    r = r_ref[...]
    rb, half = r.shape
    row = lax.broadcasted_iota(jnp.int32, r.shape, 0)
    nxt = jnp.where(row == rb - 1, nx_ref[0:1, :], pltpu.roll(r, rb - 1, axis=0))
    pe = pe_ref[...]
    h = (_dot(r + pe[:, :half], wa_ref[...]) + _dot(nxt + pe[:, half:], wb_ref[...]) + b1_ref[...])
    h = 0.5 * h * (1.0 + jnp.tanh(0.7978845608028654 * (h + 0.044715 * (h * h * h))))
    o_ref[...] = _dot(h, w2_ref[...])


def _nsa_compress(t, pe, w1, b1, w2):
    S = t.shape[0]
    G, DH, L, ST = NSA_KV_GROUPS, NSA_HEAD, CMP_LEN, CMP_STRIDE
    nr = S // ST
    rb = min(CMP_ROWS, nr)
    cols = ST * G * DH
    r = t.reshape(nr, cols)
    eye_g = jnp.eye(G, dtype=w1.dtype)
    big = jnp.einsum('ldh,gk->lgdkh', w1.reshape(L, DH, CMP_HIDDEN), eye_g).reshape(L * G * DH, G * CMP_HIDDEN)
    wa, wb = big[:cols], big[cols:]
    pe_big = jnp.broadcast_to(pe[:, None, :], (L, G, DH)).reshape(1, L * G * DH)
    b1_big = jnp.tile(b1, G).reshape(1, G * CMP_HIDDEN)
    w2_big = jnp.einsum('hd,gk->ghkd', w2, eye_g).reshape(G * CMP_HIDDEN, G * DH)
    full = lambda shape: pl.BlockSpec(shape, lambda i: (0, 0))
    return pl.pallas_call(
        _compress_body,
        grid=(nr // rb,),
        in_specs=[pl.BlockSpec((rb, cols), lambda i: (i, 0)),
                  pl.BlockSpec((8, cols), lambda i: (jnp.minimum((i + 1) * (rb // 8), nr // 8 - 1), 0)),
                  full((1, 2 * cols)), full((cols, G * CMP_HIDDEN)), full((cols, G * CMP_HIDDEN)),
                  full((1, G * CMP_HIDDEN)), full((G * CMP_HIDDEN, G * DH))],
        out_specs=pl.BlockSpec((rb, G * DH), lambda i: (i, 0)),
        out_shape=jax.ShapeDtypeStruct((nr, G * DH), jnp.float32),
        compiler_params=pltpu.CompilerParams(dimension_semantics=("parallel",),
                                             vmem_limit_bytes=48 * 1024 * 1024),
        name="nsa_compress",
    )(r, r, pe_big, _bf(wa), _bf(wb), b1_big, _bf(w2_big))


NSA_QB = 128
NSA_KC = 512
NSA_WC = 128
NSA_GATE_PAD = LANES
NSA_PCOLS = NSA_WIDTH + 6 * NSA_KV + NSA_GATE_PAD
NSA_GATE_ROWS = 16
LOG2E = 1.4426950408889634
NSA_CSEG = 256
NSA_HOT = 16


def _softmax_cols(s, mask):
    s = jnp.where(mask, s, NEG_INF)
    m = jnp.max(s, axis=0, keepdims=True)
    p = jnp.where(mask, jnp.exp2(s - m), 0.0)
    l = jnp.sum(p, axis=0, keepdims=True)
    return p * (1.0 / jnp.where(l > 0.0, l, 1.0))


def _nsa_body(*refs):
    n_win = (WINDOW + NSA_QB) // NSA_WC
    qt_ref, kvc_ref, kvct_ref, kq_ref, vst_ref = refs[:5]
    kvw_refs = refs[5:5 + n_win]
    kvwt_refs = refs[5 + n_win:5 + 2 * n_win]
    ovl_ref, gate_ref, o_ref, bias_ref, acc_ref, sca_ref, scb_ref = refs[5 + 2 * n_win:]
    QB, HPG, DH = NSA_QB, NSA_HPG, NSA_HEAD
    f32 = jnp.float32
    qi = pl.program_id(1)
    q0 = qi * QB
    q_t = jnp.transpose(qt_ref[...] * (DH ** -0.5 * LOG2E))
    q_t = jnp.concatenate([q_t[h * DH:(h + 1) * DH] for h in range(HPG)], axis=1)
    qt = _bf(jnp.concatenate([q_t, jnp.zeros_like(q_t)], axis=0))
    ncmp = kvc_ref.shape[0]
    t_row = q0 + lax.broadcasted_iota(jnp.int32, (1, QB), 1)
    head = lambda a, h: a[:, h * QB:(h + 1) * QB]

    nsel = ovl_ref.shape[0]

    def cmp_prefix(nc):
        def run():
            s = jnp.dot(kvc_ref[0:nc, :], qt, preferred_element_type=f32)
            c_end = lax.broadcasted_iota(jnp.int32, (nc, QB), 0) * CMP_STRIDE + (CMP_LEN - 1)
            m_cmp = c_end <= t_row
            kvct = kvct_ref[:, 0:nc]
            p_sum = None
            o_parts = []
            for h in range(HPG):
                p = _softmax_cols(head(s, h), m_cmp)
                o_parts.append(jnp.dot(kvct, _bf(p), preferred_element_type=f32))
                p_sum = p if p_sum is None else p_sum + p
            return tuple(o_parts) + (_dot_exact_lhs(ovl_ref[:, 0:nc], p_sum, 2),)
        return run

    seg = min(NSA_CSEG, ncmp)
    n_seg = ncmp // seg
    visible = (q0 + QB - CMP_LEN) // CMP_STRIDE + 1
    seg_idx = jnp.clip((visible + seg - 1) // seg - 1, 0, n_seg - 1)
    res = lax.switch(seg_idx, [cmp_prefix((k + 1) * seg) for k in range(n_seg)])
    o_cmp, imp = res[:HPG], res[HPG]

    n_row = lax.broadcasted_iota(jnp.int32, (nsel, QB), 0)
    cur = t_row // SEL_BLOCK
    causal = n_row <= cur
    forced = (n_row == 0) | (n_row == cur) | (n_row == cur - 1)
    score = jnp.where(causal & jnp.logical_not(forced), imp, NEG_INF)
    bias = jnp.where(causal & forced, 0.0, NEG_INF)
    for _ in range(min(SEL_TOPN, nsel) - 3):
        best = jnp.max(score, axis=0, keepdims=True)
        first = jnp.min(jnp.where(score == best, n_row, nsel), axis=0, keepdims=True)
        hit = n_row == first
        bias = jnp.where(hit & (best > 0.5 * NEG_INF), 0.0, bias)
        score = jnp.where(hit, -jnp.inf, score)
    bias_ref[...] = bias

    KC = NSA_KC
    n_sub = KC // SEL_BLOCK
    key_row = lax.broadcasted_iota(jnp.int32, (KC, QB), 0)
    acc_ref[...] = jnp.zeros_like(acc_ref)
    last_chunk = kq_ref.shape[0] // KC - 1
    q_rows = qt[0:DH]
    pad_rows = jnp.zeros((LANES - DH - NSA_HOT, HPG * QB), jnp.bfloat16)

    def chunk_scores(idx):
        c = jnp.minimum(idx, last_chunk)
        b8 = bias_ref[pl.ds(pl.multiple_of(c * n_sub, n_sub), n_sub), :]
        b_rows = jnp.concatenate([b8, jnp.zeros((NSA_HOT - n_sub, QB), f32)], axis=0)
        q_aug = jnp.concatenate([q_rows, _bf(jnp.concatenate([b_rows] * HPG, axis=1)), pad_rows], axis=0)
        k0 = pl.multiple_of(c * KC, KC)
        return jnp.dot(kq_ref[pl.ds(k0, KC), :], q_aug, preferred_element_type=f32)

    def consume(sc_ref, idx, ms, ls, causal):
        c = jnp.minimum(idx, last_chunk)
        vt = vst_ref[:, pl.ds(pl.multiple_of(c * KC, KC), KC)]
        sh = sc_ref[...]
        if causal:
            madd = jnp.where(idx * KC + key_row <= t_row, 0.0, NEG_INF)
            sh = sh + jnp.concatenate([madd] * HPG, axis=1)
        new_m = jnp.maximum(ms, jnp.max(sh, axis=0, keepdims=True))
        alpha = jnp.exp2(ms - new_m)
        p = jnp.exp2(sh - new_m)
        new_l = alpha * ls + jnp.sum(p, axis=0, keepdims=True)
        acc_ref[...] = alpha * acc_ref[...] + jnp.dot(vt, _bf(p), preferred_element_type=f32)
        return new_m, new_l

    sca_ref[...] = chunk_scores(0)

    def pair_step(jj, carry):
        ms, ls = carry
        scb_ref[...] = chunk_scores(2 * jj + 1)
        ms, ls = consume(sca_ref, 2 * jj, ms, ls, False)
        sca_ref[...] = chunk_scores(2 * jj + 2)
        return consume(scb_ref, 2 * jj + 1, ms, ls, False)

    init = (jnp.full((1, HPG * QB), NEG_INF, f32), jnp.zeros((1, HPG * QB), f32))
    n_pairs = (q0 // KC) // 2
    ms, ls = lax.fori_loop(0, n_pairs, pair_step, init)
    scb_ref[...] = chunk_scores(2 * n_pairs + 1)
    ms, ls = consume(sca_ref, 2 * n_pairs, ms, ls, True)
    _, l_sel = consume(scb_ref, 2 * n_pairs + 1, ms, ls, True)

    WC = NSA_WC
    s_win, m_win = [], []
    for c in range(n_win):
        s_pos = q0 - WINDOW + c * WC + lax.broadcasted_iota(jnp.int32, (WC, QB), 0)
        dist = t_row - s_pos
        m_win.append((dist >= 0) & (dist < WINDOW) & (s_pos >= 0))
        s_win.append(jnp.dot(kvw_refs[c][...], qt, preferred_element_type=f32))
    m_all = jnp.concatenate([jnp.concatenate(m_win, axis=0)] * HPG, axis=1)
    s_all = jnp.where(m_all, jnp.concatenate(s_win, axis=0), NEG_INF)
    p_win = jnp.exp2(s_all - jnp.max(s_all, axis=0, keepdims=True))
    l_win = jnp.sum(p_win, axis=0, keepdims=True)
    p_win = _bf(p_win)
    o_win = None
    for c in range(n_win):
        t = jnp.dot(kvwt_refs[c][DH:, :], p_win[c * WC:(c + 1) * WC], preferred_element_type=f32)
        o_win = t if o_win is None else o_win + t
    o_win = o_win * (1.0 / l_win)

    gate = _sigmoid(gate_ref[...])
    outs = []
    for h in range(HPG):
        o_sel = head(acc_ref[...], h) * (1.0 / head(l_sel, h))
        gr = lambda b: gate[h * 3 + b:h * 3 + b + 1, :]
        outs.append(gr(0) * o_cmp[h][DH:, :] + gr(1) * o_sel + gr(2) * head(o_win, h))
    o_ref[...] = jnp.transpose(jnp.concatenate(outs, axis=0))


def _nsa(p, cmp_pe, cmp_w1, cmp_b1, cmp_w2):
    S = p.shape[0]
    G, HPG, DH, QB, WC = NSA_KV_GROUPS, NSA_HPG, NSA_HEAD, NSA_QB, NSA_WC
    W, KV = NSA_WIDTH, NSA_KV
    nqb = S // QB
    n_win = (WINDOW + QB) // WC
    kc, vc, ks, vs, kw, vw = (p[:, W + i * KV:W + (i + 1) * KV] for i in range(6))
    gate = p[:, W + 6 * KV:W + 6 * KV + 3 * NSA_HEADS]
    k_cmp = _nsa_compress(kc, cmp_pe[0], cmp_w1[0], cmp_b1[0], cmp_w2[0])
    v_cmp = _nsa_compress(vc, cmp_pe[1], cmp_w1[1], cmp_b1[1], cmp_w2[1])
    pack = lambda k, v: _bf(jnp.concatenate([k.reshape(-1, G, DH), v.reshape(-1, G, DH)], -1)).transpose(1, 0, 2)
    kvc, kvw = pack(k_cmp, v_cmp), pack(kw, vw)
    hot = jax.nn.one_hot((jnp.arange(S) // SEL_BLOCK) % (NSA_KC // SEL_BLOCK), LANES - DH, dtype=jnp.bfloat16)
    kq = jnp.concatenate([_bf(ks.reshape(S, G, DH)), jnp.broadcast_to(hot[:, None, :], (S, G, LANES - DH))], -1)
    kq = kq.transpose(1, 0, 2)
    vst = _bf(vs.reshape(S, G, DH)).transpose(1, 2, 0)
    tr = lambda a: a.transpose(0, 2, 1)
    gate_t = jnp.pad(gate.reshape(S, G, 3 * HPG).transpose(1, 2, 0), ((0, 0), (0, NSA_GATE_ROWS - 3 * HPG), (0, 0)))
    ncmp, nsel = S // CMP_STRIDE, S // SEL_BLOCK
    c_start = np.arange(ncmp)[None, :] * CMP_STRIDE
    s_start = np.arange(nsel)[:, None] * SEL_BLOCK
    overlap_t = jnp.asarray((c_start < s_start + SEL_BLOCK) & (c_start + CMP_LEN > s_start), jnp.bfloat16)
    win_blk = lambda c: (lambda g, i: (g, jnp.maximum(i - WINDOW // WC + c, 0), 0))
    win_blk_t = lambda c: (lambda g, i: (g, 0, jnp.maximum(i - WINDOW // WC + c, 0)))
    in_specs = ([pl.BlockSpec((QB, HPG * DH), lambda g, i: (i, g)),
                 pl.BlockSpec((None, ncmp, LANES), lambda g, i: (g, 0, 0)),
                 pl.BlockSpec((None, LANES, ncmp), lambda g, i: (g, 0, 0)),
                 pl.BlockSpec((None, S, LANES), lambda g, i: (g, 0, 0)),
                 pl.BlockSpec((None, DH, S), lambda g, i: (g, 0, 0))]
                + [pl.BlockSpec((None, WC, LANES), win_blk(c)) for c in range(n_win)]
                + [pl.BlockSpec((None, LANES, WC), win_blk_t(c)) for c in range(n_win)]
                + [pl.BlockSpec((nsel, ncmp), lambda g, i: (0, 0)),
                   pl.BlockSpec((None, NSA_GATE_ROWS, QB), lambda g, i: (g, 0, i))])
    kvwt = tr(kvw)
    return pl.pallas_call(
        _nsa_body,
        grid=(G, nqb),
        in_specs=in_specs,
        out_specs=pl.BlockSpec((QB, HPG * DH), lambda g, i: (i, g)),
        out_shape=jax.ShapeDtypeStruct((S, W), jnp.float32),
        scratch_shapes=[pltpu.VMEM((nsel, QB), jnp.float32),
                        pltpu.VMEM((DH, HPG * QB), jnp.float32),
                        pltpu.VMEM((NSA_KC, HPG * QB), jnp.float32),
                        pltpu.VMEM((NSA_KC, HPG * QB), jnp.float32)],
        compiler_params=pltpu.CompilerParams(dimension_semantics=("arbitrary", "arbitrary"),
                                             vmem_limit_bytes=56 * 1024 * 1024),
        name="nsa_attention",
    )(p, kvc, tr(kvc), kq, vst, *([kvw] * n_win), *([kvwt] * n_win), overlap_t, gate_t)


def _retention_body(p_ref, cc_ref, ss_ref, inner_ref, qd_ref, kd_ref, cd_ref, o_ref, r_ref):
    H, DK, W = RET_HEADS, RET_HEAD, RET_WIDTH
    f32 = jnp.float32

    @pl.when(pl.program_id(0) == 0)
    def _():
        r_ref[...] = jnp.zeros_like(r_ref)

    cc, ss = cc_ref[...], ss_ref[...]
    hh = range(H)
    col = lambda part, h: p_ref[:, part * W + h * DK:part * W + (h + 1) * DK].astype(f32)
    rot = lambda t: t * cc + pltpu.roll(t, DK // 2, axis=1) * ss
    q = [rot(col(0, h)) for h in hh]
    k = [rot(col(1, h)) * (DK ** -0.5) for h in hh]
    v = [col(2, h) for h in hh]
    r = [r_ref[h] for h in hh]
    att = [_dot(q[h], k[h], _NT) * inner_ref[h] for h in hh]
    o = [_dot(att[h], v[h]) + _dot(q[h], r[h]) * qd_ref[h] for h in hh]
    upd = [_dot(k[h] * kd_ref[h], v[h], _TN) for h in hh]
    for h in hh:
        r_ref[h] = r[h] * cd_ref[h] + upd[h]
        mu = jnp.mean(o[h], axis=-1, keepdims=True)
        oc = o[h] - mu
        var = jnp.mean(oc * oc, axis=-1, keepdims=True)
        g = col(3, h)
        o_ref[:, h * DK:(h + 1) * DK] = (g * _sigmoid(g)) * (oc * lax.rsqrt(var + RET_GN_EPS))


def _retention(p):
    S = p.shape[0]
    H, DK, C = RET_HEADS, RET_HEAD, RET_CHUNK
    f32 = jnp.float32
    inv_freq = 1.0 / (RET_THETA ** jnp.linspace(0.0, 1.0, DK // 2))
    ang = jnp.arange(S, dtype=f32)[:, None] * inv_freq[None, :]
    cos, sin = jnp.cos(ang), jnp.sin(ang)
    cc = jnp.concatenate([cos, cos], -1)
    ss = jnp.concatenate([-sin, sin], -1)
    log_g = jnp.log1p(-jnp.exp2(-5.0 - jnp.arange(H, dtype=f32)))
    i = jnp.arange(C, dtype=f32)
    diff = i[:, None] - i[None, :]
    inner_decay = jnp.where(diff >= 0, jnp.exp(log_g[:, None, None] * jnp.maximum(diff, 0.0)), 0.0)
    lanes = lambda t: jnp.broadcast_to(t, (H, C, DK))
    q_decay = lanes(jnp.exp(log_g[:, None] * (i + 1.0))[..., None])
    k_decay = lanes(jnp.exp(log_g[:, None] * (C - 1.0 - i))[..., None])
    c_decay = lanes(jnp.exp(log_g * C)[:, None, None])
    tok = lambda width: pl.BlockSpec((C, width), lambda n: (n, 0))
    const = pl.BlockSpec((H, C, DK), lambda n: (0, 0, 0))
    return pl.pallas_call(
        _retention_body,
        grid=(S // C,),
        in_specs=[tok(p.shape[1]), tok(DK), tok(DK), const, const, const, const],
        out_specs=tok(RET_WIDTH),
        out_shape=jax.ShapeDtypeStruct((S, RET_WIDTH), f32),
        scratch_shapes=[pltpu.VMEM((H, DK, DK), f32)],
        compiler_params=pltpu.CompilerParams(dimension_semantics=("arbitrary",),
                                             vmem_limit_bytes=48 * 1024 * 1024),
        name="retention_chunked",
    )(p, cc, ss, inner_decay, q_decay, k_decay, c_decay)


MOE_ROWS = 256


def _expert_body(blk_e_ref, n_used_ref, tok_ref, x_hbm, wg_ref, wu_ref, wd_ref, o_ref, xbuf, sem):
    i = pl.program_id(0)
    n_used = n_used_ref[0]
    mb = MOE_ROWS

    def row_copy(block, buf, r):
        tok = tok_ref[block * mb + r]
        return pltpu.make_async_copy(x_hbm.at[pl.ds(tok, 1), :], xbuf.at[buf, pl.ds(r, 1), :], sem.at[buf])

    def start_block(block, buf):
        def body(r, carry):
            row_copy(block, buf, r).start()
            return carry
        lax.fori_loop(0, mb, body, 0, unroll=8)

    def wait_block(block, buf):
        def body(r, carry):
            row_copy(block, buf, r).wait()
            return carry
        lax.fori_loop(0, mb, body, 0, unroll=8)

    @pl.when((i == 0) & (n_used > 0))
    def _():
        start_block(0, 0)

    @pl.when(i + 1 < n_used)
    def _():
        start_block(i + 1, (i + 1) % 2)

    @pl.when(i < n_used)
    def _():
        wait_block(i, i % 2)
        x = xbuf[i % 2]
        hg = _dot(x, wg_ref[...])
        h = (hg * _sigmoid(hg)) * _dot(x, wu_ref[...])
        o_ref[...] = _dot(h, wd_ref[...])

    @pl.when(i >= n_used)
    def _():
        o_ref[...] = jnp.zeros_like(o_ref)


def _moe_experts(x, slot_tok, blk_expert, n_used, layer, w_gate, w_up, w_down):
    D = x.shape[1]
    n_slots = slot_tok.shape[0]
    F = w_gate.shape[3]
    mb = MOE_ROWS
    grid_spec = pltpu.PrefetchScalarGridSpec(
        num_scalar_prefetch=3,
        grid=(n_slots // mb,),
        in_specs=[pl.BlockSpec(memory_space=pl.ANY),
                  pl.BlockSpec((None, None, D, F), lambda i, be, nu, tk: (layer, be[i], 0, 0)),
                  pl.BlockSpec((None, None, D, F), lambda i, be, nu, tk: (layer, be[i], 0, 0)),
                  pl.BlockSpec((None, None, F, D), lambda i, be, nu, tk: (layer, be[i], 0, 0))],
        out_specs=pl.BlockSpec((mb, D), lambda i, be, nu, tk: (i, 0)),
        scratch_shapes=[pltpu.VMEM((2, mb, D), jnp.float32), pltpu.SemaphoreType.DMA((2,))])
    return pl.pallas_call(
        _expert_body,
        grid_spec=grid_spec,
        out_shape=jax.ShapeDtypeStruct((n_slots, D), jnp.float32),
        compiler_params=pltpu.CompilerParams(dimension_semantics=("arbitrary",),
                                             vmem_limit_bytes=56 * 1024 * 1024),
        name="moe_experts",
    )(blk_expert, n_used, slot_tok, x, w_gate, w_up, w_down)


def _hier_moe(x1, logits, layer, w_gate, w_up, w_down):
    T, D = x1.shape
    f32 = jnp.float32
    grp_logits = logits[:, :N_GROUPS]
    grp = jnp.argmax(grp_logits, axis=-1)
    grp_w = jnp.take_along_axis(jax.nn.softmax(grp_logits, -1), grp[:, None], axis=-1)
    exp_logits = logits[:, N_GROUPS:N_GROUPS + N_EXPERTS].reshape(T, N_GROUPS, EXPERTS_PER_GROUP)
    in_grp = jnp.take_along_axis(exp_logits, grp[:, None, None], axis=1)[:, 0]
    e_iota = jnp.arange(EXPERTS_PER_GROUP, dtype=jnp.int32)[None, :]
    i1 = jnp.argmax(in_grp, axis=-1).astype(jnp.int32)
    rest = jnp.where(e_iota == i1[:, None], -jnp.inf, in_grp)
    i2 = jnp.argmax(rest, axis=-1).astype(jnp.int32)
    top_idx = jnp.stack([i1, i2], -1)
    top_val = jnp.stack([jnp.max(in_grp, -1), jnp.max(rest, -1)], -1)
    gate_w = (jax.nn.softmax(top_val, -1) * grp_w).reshape(-1)
    expert = (grp[:, None] * EXPERTS_PER_GROUP + top_idx).reshape(-1).astype(jnp.int32)
    token = jnp.repeat(jnp.arange(T, dtype=jnp.int32), EXPERT_TOPK)
    n_assign = T * EXPERT_TOPK
    mb = MOE_ROWS
    onehot = (expert[:, None] == jnp.arange(N_EXPERTS, dtype=jnp.int32)[None, :]).astype(jnp.int32)
    csum = jnp.cumsum(onehot, axis=0)
    pos = jnp.take_along_axis(csum, expert[:, None], axis=1)[:, 0] - 1
    counts = csum[-1]
    padded = (counts + mb - 1) // mb * mb
    pad_end = jnp.cumsum(padded)
    slot = (pad_end - padded)[expert] + pos
    n_blocks = -(-n_assign // mb) + N_EXPERTS
    n_slots = n_blocks * mb
    slot_tok = jnp.full((n_slots,), T, jnp.int32).at[slot].set(token)
    blk_start = jnp.arange(n_blocks, dtype=jnp.int32) * mb
    blk_expert = jnp.minimum(jnp.sum((pad_end[None, :] <= blk_start[:, None]).astype(jnp.int32), axis=1),
                             N_EXPERTS - 1)
    n_used = (pad_end[-1:] // mb).astype(jnp.int32)
    y = _moe_experts(x1, jnp.minimum(slot_tok, T - 1), blk_expert, n_used, layer, w_gate, w_up, w_down)
    slot2 = slot.reshape(T, EXPERT_TOPK)
    gate2 = gate_w.reshape(T, EXPERT_TOPK)
    return (y[slot2[:, 0]], y[slot2[:, 1]]), (gate2[:, 0:1], gate2[:, 1:2])


def kernel(x, w_in_first, w_in_deep, rw_mu_first, rw_mu_deep, rw_w0, rw_w_up, rw_a0, rw_a_up, rw_v0, rw_v_up, rw_g_up, rw_k_k, rw_k_a, rw_r_k, rw_lnx_g, rw_lnx_b, nsa_cmp_pe, nsa_cmp_w1, nsa_cmp_b1, nsa_cmp_w2, w_br_rw, w_br_nsa, w_br_ret, w_out, ln1_g, ln1_b, moe_w_grp, moe_b_grp, moe_w_exp, moe_b_exp, moe_w_gate, moe_w_up, moe_w_down, ln2_g, ln2_b):
    B, S, D = x.shape
    assert B == 1
    x = x.reshape(S, D)
    xb = _bf(x)
    v_first = None
    for l in range(DEPTH):
        first = l == 0
        w_in = w_in_first if first else w_in_deep[l - 1]
        mu = rw_mu_first if first else rw_mu_deep[l - 1]
        rw_cols = RW_COLS_FIRST if first else RW_COLS_DEEP
        rw_sizes = RW_BASE_SIZES if first else RW_BASE_SIZES + (RW_V_LORA,)
        rw_padded = RW_PAD_SIZES if first else RW_PAD_SIZES + (LORA_PAD,)
        rw_tn = 512 if first else 768
        tail = -sum(rw_padded) % rw_tn
        mu_p = jnp.pad(_pad_cols(mu, rw_sizes, rw_padded), (0, tail))
        sizes = rw_sizes + (0, NSA_COLS, RET_COLS, 3 * D_MODEL)
        padded = rw_padded + (tail, NSA_PCOLS, RET_COLS, 3 * D_MODEL)
        wt = lax.optimization_barrier(jnp.transpose(w_in))
        rows, r0 = [], 0
        for n, n_pad in zip(sizes, padded):
            rows.append(jnp.pad(wt[r0:r0 + n], ((0, n_pad - n), (0, 0))))
            r0 += n
        wt = lax.optimization_barrier(_bf(jnp.concatenate(rows, 0)))
        w_all = jnp.transpose(wt)
        c0 = sum(rw_padded) + tail
        w_rw = w_all[:, :c0]
        w_nsa = w_all[:, c0:c0 + NSA_PCOLS]
        c0 += NSA_PCOLS
        w_ret = w_all[:, c0:c0 + RET_COLS]
        w_gates = w_all[:, c0 + RET_COLS:]
        p_rw = _matmul(xb, w_rw, tn=rw_tn)
        p_nsa = _matmul(xb, w_nsa, tn=NSA_PCOLS // 3)
        p_ret = _matmul(xb, w_ret, tn=512)
        gates = _matmul(xb, w_gates, tn=512, out_dtype=jnp.bfloat16)
        y_rw, v_first = _rwkv7_time_mix(p_rw, mu_p, rw_w0[l], rw_w_up[l], rw_a0[l],
                                        rw_a_up[l], rw_g_up[l], rw_k_k[l], rw_k_a[l], rw_r_k[l], rw_lnx_g[l],
                                        rw_lnx_b[l], v_first,
                                        None if first else rw_v0[l - 1], None if first else rw_v_up[l - 1])
        y_nsa = _nsa(p_nsa, nsa_cmp_pe[l], nsa_cmp_w1[l], nsa_cmp_b1[l], nsa_cmp_w2[l])
        y_ret = _retention(p_ret)
        w_br = _bf(jnp.stack([w_br_rw[l], w_br_nsa[l], w_br_ret[l]]))
        merged = _merge_branches((y_rw, y_nsa, y_ret), gates, w_br)
        w_router = jnp.pad(jnp.concatenate([moe_w_grp[l], moe_w_exp[l]], 1),
                           ((0, 0), (0, ROUTER_PAD - N_GROUPS - N_EXPERTS)))
        b_router = jnp.pad(jnp.concatenate([moe_b_grp[l], moe_b_exp[l]]),
                           (0, ROUTER_PAD - N_GROUPS - N_EXPERTS)).reshape(1, ROUTER_PAD)
        x1, logits = _out_ln_router(merged, _bf(w_out[l]), x, ln1_g[l], ln1_b[l], _bf(w_router), b_router)
        ys, ws = _hier_moe(x1, logits, l, moe_w_gate, moe_w_up, moe_w_down)
        x, xb = _moe_combine_ln(x1, ys, ws, ln2_g[l], ln2_b[l])
    return x.reshape(B, S, D)
```

```python
import functools

import jax
import jax.numpy as jnp
from jax import lax
import numpy as np
from jax.experimental import pallas as pl
from jax.experimental.pallas import tpu as pltpu

D_MODEL = 2048
DEPTH = 2

RW_HEADS = 16
RW_HEAD = 64
RW_WIDTH = RW_HEADS * RW_HEAD
RW_DECAY_LORA = 96
RW_A_LORA = 96
RW_V_LORA = 64
RW_G_LORA = 256
RW_GN_EPS = 64e-5
RW_BASE_SIZES = (RW_WIDTH, RW_WIDTH, RW_WIDTH, RW_DECAY_LORA, RW_A_LORA, RW_G_LORA)
RW_COLS_FIRST = sum(RW_BASE_SIZES)
RW_COLS_DEEP = RW_COLS_FIRST + RW_V_LORA

NSA_HEADS = 16
NSA_KV_GROUPS = 4
NSA_HPG = NSA_HEADS // NSA_KV_GROUPS
NSA_HEAD = 64
NSA_WIDTH = NSA_HEADS * NSA_HEAD
NSA_KV = NSA_KV_GROUPS * NSA_HEAD
CMP_LEN = 32
CMP_STRIDE = 16
CMP_HIDDEN = 128
SEL_BLOCK = 64
SEL_TOPN = 16
WINDOW = 512
Q_BLOCK = 128
FORCE_BONUS = 1e4
NEG_INF = -1e30
NSA_SIZES = (NSA_WIDTH,) + (NSA_KV,) * 6 + (3 * NSA_HEADS,)
NSA_COLS = sum(NSA_SIZES)

RET_HEADS = 8
RET_HEAD = 128
RET_WIDTH = RET_HEADS * RET_HEAD
RET_CHUNK = 128
RET_THETA = 10000.0
RET_GN_EPS = 1e-5
RET_SIZES = (RET_WIDTH,) * 4
RET_COLS = sum(RET_SIZES)

N_GROUPS = 4
EXPERTS_PER_GROUP = 8
N_EXPERTS = N_GROUPS * EXPERTS_PER_GROUP
EXPERT_FF = 512
EXPERT_TOPK = 2
MOE_BLOCK = 128

DN_ALPHA = (2 * DEPTH) ** 0.25
LN_EPS = 1e-5

LANES = 128


MM_ROWS = 1024


def _matmul_body(a_ref, b_ref, o_ref):
    o_ref[...] = jnp.dot(a_ref[...], b_ref[...], preferred_element_type=jnp.float32).astype(o_ref.dtype)


def _matmul(a, b, tn, out_dtype=jnp.float32):
    m, k = a.shape
    n = b.shape[1]
    tm = min(MM_ROWS, m)
    return pl.pallas_call(
        _matmul_body,
        grid=(m // tm, n // tn),
        in_specs=[pl.BlockSpec((tm, k), lambda i, j: (i, 0)),
                  pl.BlockSpec((k, tn), lambda i, j: (0, j))],
        out_specs=pl.BlockSpec((tm, tn), lambda i, j: (i, j)),
        out_shape=jax.ShapeDtypeStruct((m, n), out_dtype),
        compiler_params=pltpu.CompilerParams(
            dimension_semantics=("parallel", "parallel"),
            vmem_limit_bytes=48 * 1024 * 1024),
        name="dense_matmul",
    )(a, b)


MERGE_ROWS = 512
MERGE_COLS = 512


def _merge_body(y0_ref, y1_ref, y2_ref, g0_ref, g1_ref, g2_ref, w_ref, o_ref):
    acc = None
    for b, (y_ref, g_ref) in enumerate(((y0_ref, g0_ref), (y1_ref, g1_ref), (y2_ref, g2_ref))):
        t = _sigmoid(g_ref[...].astype(jnp.float32)) * _dot(y_ref[...], w_ref[b])
        acc = t if acc is None else acc + t
    o_ref[...] = acc.astype(o_ref.dtype)


def _merge_branches(ys, gates, w_br):
    S, wb = ys[0].shape
    D = w_br.shape[2]
    tm, tn = min(MERGE_ROWS, S), MERGE_COLS
    nj = D // tn
    y_spec = pl.BlockSpec((tm, wb), lambda i, j: (i, 0))
    g_spec = lambda b: pl.BlockSpec((tm, tn), lambda i, j: (i, b * nj + j))
    return pl.pallas_call(
        _merge_body,
        grid=(S // tm, nj),
        in_specs=[y_spec, y_spec, y_spec, g_spec(0), g_spec(1), g_spec(2),
                  pl.BlockSpec((3, wb, tn), lambda i, j: (0, 0, j))],
        out_specs=pl.BlockSpec((tm, tn), lambda i, j: (i, j)),
        out_shape=jax.ShapeDtypeStruct((S, D), jnp.bfloat16),
        compiler_params=pltpu.CompilerParams(dimension_semantics=("parallel", "parallel"),
                                             vmem_limit_bytes=48 * 1024 * 1024),
        name="merge_branches",
    )(*ys, gates, gates, gates, w_br)


LN_ROWS = 256
ROUTER_PAD = LANES


def _ln_rows(z, g, b):
    mu = jnp.mean(z, axis=-1, keepdims=True)
    zc = z - mu
    var = jnp.mean(zc * zc, axis=-1, keepdims=True)
    return zc * lax.rsqrt(var + LN_EPS) * g + b


def _out_ln_body(m_ref, w_ref, x_ref, g_ref, b_ref, wr_ref, br_ref, x1_ref, lg_ref):
    z = DN_ALPHA * x_ref[...] + jnp.dot(m_ref[...], w_ref[...], preferred_element_type=jnp.float32)
    x1 = _ln_rows(z, g_ref[...], b_ref[...])
    x1_ref[...] = x1
    lg_ref[...] = jnp.dot(_bf(x1), wr_ref[...], preferred_element_type=jnp.float32) + br_ref[...]


def _out_ln_router(merged, w_out, x, ln_g, ln_b, w_router, b_router):
    S, D = x.shape
    tm = min(LN_ROWS, S)
    row = pl.BlockSpec((tm, D), lambda i: (i, 0))
    full = lambda shape: pl.BlockSpec(shape, lambda i: (0, 0))
    return pl.pallas_call(
        _out_ln_body,
        grid=(S // tm,),
        in_specs=[row, full((D, D)), row, full((1, D)), full((1, D)), full((D, ROUTER_PAD)), full((1, ROUTER_PAD))],
        out_specs=(row, pl.BlockSpec((tm, ROUTER_PAD), lambda i: (i, 0))),
        out_shape=(jax.ShapeDtypeStruct((S, D), jnp.float32), jax.ShapeDtypeStruct((S, ROUTER_PAD), jnp.float32)),
        compiler_params=pltpu.CompilerParams(dimension_semantics=("parallel",),
                                             vmem_limit_bytes=48 * 1024 * 1024),
        name="out_proj_ln_router",
    )(merged, w_out, x, ln_g.reshape(1, D), ln_b.reshape(1, D), w_router, b_router)


def _moe_combine_ln_body(x_ref, y0_ref, y1_ref, w0_ref, w1_ref, g_ref, b_ref, o_ref, ob_ref):
    moe = y0_ref[...] * w0_ref[...] + y1_ref[...] * w1_ref[...]
    y = _ln_rows(DN_ALPHA * x_ref[...] + moe, g_ref[...], b_ref[...])
    o_ref[...] = y
    ob_ref[...] = _bf(y)


def _moe_combine_ln(x, ys, ws, ln_g, ln_b):
    S, D = x.shape
    tm = min(LN_ROWS, S)
    row = pl.BlockSpec((tm, D), lambda i: (i, 0))
    col = pl.BlockSpec((tm, 1), lambda i: (i, 0))
    full = pl.BlockSpec((1, D), lambda i: (0, 0))
    return pl.pallas_call(
        _moe_combine_ln_body,
        grid=(S // tm,),
        in_specs=[row, row, row, col, col, full, full],
        out_specs=(row, row),
        out_shape=(jax.ShapeDtypeStruct((S, D), jnp.float32), jax.ShapeDtypeStruct((S, D), jnp.bfloat16)),
        compiler_params=pltpu.CompilerParams(dimension_semantics=("parallel",)),
        name="moe_combine_ln",
    )(x, ys[0], ys[1], ws[0], ws[1], ln_g.reshape(1, D), ln_b.reshape(1, D))


def _split(p, sizes):
    idx = [int(i) for i in np.cumsum(sizes)[:-1]]
    return jnp.split(p, idx, axis=-1)


RW_CHUNK = 64
LORA_PAD = LANES
RW_OFF_XW = 3 * RW_WIDTH
RW_OFF_XA = RW_OFF_XW + LORA_PAD
RW_OFF_XG = RW_OFF_XA + LORA_PAD
RW_OFF_XV = RW_OFF_XG + RW_G_LORA
RW_PAD_SIZES = (RW_WIDTH, RW_WIDTH, RW_WIDTH, LORA_PAD, LORA_PAD, RW_G_LORA)

_NT = (((1,), (1,)), ((), ()))
_TN = (((0,), (0,)), ((), ()))


def _bf(x):
    return x.astype(jnp.bfloat16)


def _dot(a, b, dims=None):
    if dims is None:
        return jnp.dot(_bf(a), _bf(b), preferred_element_type=jnp.float32)
    return lax.dot_general(_bf(a), _bf(b), dims, preferred_element_type=jnp.float32)


def _split3(x):
    h = _bf(x)
    r = x - h.astype(jnp.float32)
    m = _bf(r)
    l = _bf(r - m.astype(jnp.float32))
    return h, m, l


def _dot_exact_rhs(a, b_bf16, passes):
    acc = None
    for p in _split3(a)[:passes]:
        t = jnp.dot(p, b_bf16, preferred_element_type=jnp.float32)
        acc = t if acc is None else acc + t
    return acc


def _dot_exact_lhs(a_bf16, b, passes):
    acc = None
    for p in _split3(b)[:passes]:
        t = jnp.dot(a_bf16, p, preferred_element_type=jnp.float32)
        acc = t if acc is None else acc + t
    return acc


def _sigmoid(x):
    return 1.0 / (1.0 + jnp.exp(-x))


def _softplus(x):
    return jnp.maximum(x, 0.0) + jnp.log(1.0 + jnp.exp(-jnp.abs(x)))


def _rwkv_body(first, *refs):
    if first:
        (p_ref, mu_ref, w0_ref, wup_ref, a0_ref, aup_ref, gup_ref, kk_ref, ka_ref, rk_ref,
         lg_ref, lb_ref, ones_ref, y_ref, vf_out_ref, st_ref, prev_ref) = refs
    else:
        (p_ref, mu_ref, w0_ref, wup_ref, a0_ref, aup_ref, gup_ref, kk_ref, ka_ref, rk_ref,
         lg_ref, lb_ref, ones_ref, v0_ref, vup_ref, vf_in_ref, y_ref, st_ref, prev_ref) = refs
    C, H, N, W = RW_CHUNK, RW_HEADS, RW_HEAD, RW_WIDTH
    f32 = jnp.float32
    i = pl.program_id(0)

    @pl.when(i == 0)
    def _():
        st_ref[...] = jnp.zeros_like(st_ref)
        prev_ref[...] = jnp.zeros_like(prev_ref)

    z = p_ref[...]
    row = lax.broadcasted_iota(jnp.int32, z.shape, 0)
    prev = jnp.where(row == 0, prev_ref[...], pltpu.roll(z, 1, axis=0))
    prev_ref[...] = z[C - 1:C, :]
    xs = z + (prev - z) * mu_ref[...]

    r = xs[:, 0:W]
    k = xs[:, W:2 * W]
    v = xs[:, 2 * W:3 * W]
    xw = xs[:, RW_OFF_XW:RW_OFF_XW + LORA_PAD]
    xa = xs[:, RW_OFF_XA:RW_OFF_XA + LORA_PAD]
    xg = xs[:, RW_OFF_XG:RW_OFF_XG + RW_G_LORA]
    if first:
        vf_out_ref[...] = v
    else:
        xv = xs[:, RW_OFF_XV:RW_OFF_XV + LORA_PAD]
        v = v + (vf_in_ref[...] - v) * _sigmoid(v0_ref[...] + _dot(xv, vup_ref[...]))

    w_log = -_softplus(-(w0_ref[...] + _dot(jnp.tanh(xw), wup_ref[...]))) - 0.5
    logd = -jnp.exp(w_log)
    a = _sigmoid(a0_ref[...] + _dot(xa, aup_ref[...]))
    g = _dot(_sigmoid(xg), gup_ref[...])
    hsum = ones_ref[...]

    def head_sums(t):
        acc = None
        for piece in _split3(_dot_exact_rhs(t, hsum, 2)):
            u = lax.dot_general(piece, hsum, _NT, preferred_element_type=f32)
            acc = u if acc is None else acc + u
        return acc

    kk = k * kk_ref[...]
    kk_n2 = head_sums(kk * kk)
    kk = kk / jnp.maximum(jnp.sqrt(kk_n2), 1e-12)
    k = k * (1.0 + (a - 1.0) * ka_ref[...])
    b = kk * a
    bonus = head_sums(r * k * rk_ref[...]) * v

    tr = lax.broadcasted_iota(jnp.int32, (C, C), 0)
    tc = lax.broadcasted_iota(jnp.int32, (C, C), 1)
    low_incl = tr >= tc
    low_strict = tr > tc
    cum = _dot_exact_lhs(low_incl.astype(jnp.bfloat16), logd, 3)
    total = cum[C - 1:C, :]
    e_in = jnp.exp(cum)
    e_ex = jnp.exp(cum - logd)
    e_neg = jnp.exp(-cum)
    e_rem = jnp.exp(total - cum)
    kkd = kk * e_ex
    kh = k * e_neg
    bh = b * e_neg
    rd = r * e_in
    khg = k * e_rem
    bhg = b * e_rem
    gam = jnp.exp(total)

    eye = (tr == tc).astype(f32)
    heads = range(H)
    hs = lambda t: [t[:, h * N:(h + 1) * N] for h in heads]
    kkd_h, kh_h, bh_h, rd_h, v_h, khg_h, bhg_h = (hs(t) for t in (kkd, kh, bh, rd, v, khg, bhg))
    st = [st_ref[h] for h in heads]
    gm = [_dot(jnp.concatenate([kkd_h[h], rd_h[h]], axis=0),
               jnp.concatenate([kh_h[h], bh_h[h]], axis=0), _NT) for h in heads]
    a_k = [jnp.where(low_strict, gm[h][0:C, 0:C], 0.0) for h in heads]
    a_b = [jnp.where(low_strict, gm[h][0:C, C:2 * C], 0.0) for h in heads]
    b_k = [jnp.where(low_incl, gm[h][C:2 * C, 0:C], 0.0) for h in heads]
    b_b = [jnp.where(low_incl, gm[h][C:2 * C, C:2 * C], 0.0) for h in heads]
    rhs = [_dot(kkd_h[h], st[h]) + _dot(a_k[h], v_h[h]) for h in heads]
    y0 = [_dot(rd_h[h], st[h]) + _dot(b_k[h], v_h[h]) for h in heads]
    t_inv = [eye - jnp.where((tr == tc + 1) & (tr % 2 == 1), a_b[h], 0.0) for h in heads]
    m = 2
    while m < C:
        rb, cb = tr // m, tc // m
        sib = (rb == cb + 1) & (rb % 2 == 1)
        tmp = [_dot(jnp.where(sib, a_b[h], 0.0), t_inv[h]) for h in heads]
        t_inv = [t_inv[h] - _dot(t_inv[h], tmp[h]) for h in heads]
        m *= 2
    zz = [_dot(t_inv[h], rhs[h]) for h in heads]
    ys = [y0[h] - _dot(b_b[h], zz[h]) for h in heads]
    upd = [_dot(khg_h[h], v_h[h], _TN) - _dot(bhg_h[h], zz[h], _TN) for h in heads]
    for h in heads:
        st_ref[h] = st[h] * jnp.transpose(gam[:, h * N:(h + 1) * N]) + upd[h]
    y = jnp.concatenate(ys, axis=1)

    inv_n = 1.0 / N
    mean = head_sums(y) * inv_n
    yc = y - mean
    var = head_sums(yc * yc) * inv_n
    yn = yc * lax.rsqrt(var + RW_GN_EPS) * lg_ref[...] + lb_ref[...]
    y_ref[...] = (yn + bonus) * g


def _pad_rows(w, rows):
    return jnp.pad(w, ((0, rows - w.shape[0]), (0, 0)))


def _pad_cols(p, sizes, padded):
    parts = _split(p, sizes)
    return jnp.concatenate([jnp.pad(t, [(0, 0)] * (t.ndim - 1) + [(0, n - t.shape[-1])])
                            for t, n in zip(parts, padded)], -1)


def _rwkv7_time_mix(p, mu, w0, w_up, a0, a_up, g_up, k_k, k_a, r_k, lnx_g, lnx_b, v_first, v0, v_up):
    S, cols = p.shape
    first = v_first is None
    C, W = RW_CHUNK, RW_WIDTH
    row = lambda t: t.reshape(1, -1)
    hid = np.arange(W) // RW_HEAD
    ones_blk = jnp.asarray(hid[:, None] == np.arange(LANES)[None, :], jnp.bfloat16)
    full = lambda shape: pl.BlockSpec(shape, lambda i: (0,) * len(shape))
    tok = lambda width: pl.BlockSpec((C, width), lambda i: (i, 0))
    args = [p, row(mu), row(w0), _pad_rows(w_up, LORA_PAD), row(a0), _pad_rows(a_up, LORA_PAD), g_up,
            row(k_k), row(k_a), row(r_k), row(lnx_g), row(lnx_b), ones_blk]
    specs = [tok(cols), full((1, cols)), full((1, W)), full((LORA_PAD, W)), full((1, W)), full((LORA_PAD, W)),
             full((RW_G_LORA, W)), full((1, W)), full((1, W)), full((1, W)), full((1, W)), full((1, W)),
             full((W, LANES))]
    if first:
        out_shape = (jax.ShapeDtypeStruct((S, W), jnp.float32), jax.ShapeDtypeStruct((S, W), jnp.float32))
        out_specs = (tok(W), tok(W))
    else:
        args += [row(v0), _pad_rows(v_up, LORA_PAD), v_first]
        specs += [full((1, W)), full((LORA_PAD, W)), tok(W)]
        out_shape = jax.ShapeDtypeStruct((S, W), jnp.float32)
        out_specs = tok(W)
    res = pl.pallas_call(
        functools.partial(_rwkv_body, first),
        grid=(S // C,),
        in_specs=specs,
        out_specs=out_specs,
        out_shape=out_shape,
        scratch_shapes=[pltpu.VMEM((RW_HEADS, RW_HEAD, RW_HEAD), jnp.float32),
                        pltpu.VMEM((1, cols), jnp.float32)],
        compiler_params=pltpu.CompilerParams(dimension_semantics=("arbitrary",),
                                             vmem_limit_bytes=48 * 1024 * 1024),
        name="rwkv7_chunked",
    )(*args)
    if first:
        return res[0], res[1]
    return res, v_first


CMP_ROWS = 256


def _compress_body(r_ref, nx_ref, pe_ref, wa_ref, wb_ref, b1_ref, w2_ref, o_ref):
    r = r_ref[...]
    rb, half = r.shape
    row = lax.broadcasted_iota(jnp.int32, r.shape, 0)
    nxt = jnp.where(row == rb - 1, nx_ref[0:1, :], pltpu.roll(r, rb - 1, axis=0))
    pe = pe_ref[...]
    h = (_dot(r + pe[:, :half], wa_ref[...]) + _dot(nxt + pe[:, half:], wb_ref[...]) + b1_ref[...])
    h = 0.5 * h * (1.0 + jnp.tanh(0.7978845608028654 * (h + 0.044715 * (h * h * h))))
    o_ref[...] = _dot(h, w2_ref[...])


def _nsa_compress(t, pe, w1, b1, w2):
    S = t.shape[0]
    G, DH, L, ST = NSA_KV_GROUPS, NSA_HEAD, CMP_LEN, CMP_STRIDE
    nr = S // ST
    rb = min(CMP_ROWS, nr)
    cols = ST * G * DH
    r = t.reshape(nr, cols)
    eye_g = jnp.eye(G, dtype=w1.dtype)
    big = jnp.einsum('ldh,gk->lgdkh', w1.reshape(L, DH, CMP_HIDDEN), eye_g).reshape(L * G * DH, G * CMP_HIDDEN)
    wa, wb = big[:cols], big[cols:]
    pe_big = jnp.broadcast_to(pe[:, None, :], (L, G, DH)).reshape(1, L * G * DH)
    b1_big = jnp.tile(b1, G).reshape(1, G * CMP_HIDDEN)
    w2_big = jnp.einsum('hd,gk->ghkd', w2, eye_g).reshape(G * CMP_HIDDEN, G * DH)
    full = lambda shape: pl.BlockSpec(shape, lambda i: (0, 0))
    return pl.pallas_call(
        _compress_body,
        grid=(nr // rb,),
        in_specs=[pl.BlockSpec((rb, cols), lambda i: (i, 0)),
                  pl.BlockSpec((8, cols), lambda i: (jnp.minimum((i + 1) * (rb // 8), nr // 8 - 1), 0)),
                  full((1, 2 * cols)), full((cols, G * CMP_HIDDEN)), full((cols, G * CMP_HIDDEN)),
                  full((1, G * CMP_HIDDEN)), full((G * CMP_HIDDEN, G * DH))],
        out_specs=pl.BlockSpec((rb, G * DH), lambda i: (i, 0)),
        out_shape=jax.ShapeDtypeStruct((nr, G * DH), jnp.float32),
        compiler_params=pltpu.CompilerParams(dimension_semantics=("parallel",),
                                             vmem_limit_bytes=48 * 1024 * 1024),
        name="nsa_compress",
    )(r, r, pe_big, _bf(wa), _bf(wb), b1_big, _bf(w2_big))


NSA_QB = 128
NSA_KC = 512
NSA_WC = 128
NSA_GATE_PAD = LANES
NSA_PCOLS = NSA_WIDTH + 6 * NSA_KV + NSA_GATE_PAD
NSA_GATE_ROWS = 16
LOG2E = 1.4426950408889634
NSA_CSEG = 256
NSA_HOT = 16


def _softmax_cols(s, mask):
    s = jnp.where(mask, s, NEG_INF)
    m = jnp.max(s, axis=0, keepdims=True)
    p = jnp.where(mask, jnp.exp2(s - m), 0.0)
    l = jnp.sum(p, axis=0, keepdims=True)
    return p * (1.0 / jnp.where(l > 0.0, l, 1.0))


def _nsa_body(*refs):
    n_win = (WINDOW + NSA_QB) // NSA_WC
    qt_ref, kvc_ref, kvct_ref, kq_ref, vst_ref = refs[:5]
    kvw_refs = refs[5:5 + n_win]
    kvwt_refs = refs[5 + n_win:5 + 2 * n_win]
    ovl_ref, gate_ref, o_ref, bias_ref, acc_ref, sca_ref, scb_ref = refs[5 + 2 * n_win:]
    QB, HPG, DH = NSA_QB, NSA_HPG, NSA_HEAD
    f32 = jnp.float32
    qi = pl.program_id(1)
    q0 = qi * QB
    q_t = jnp.transpose(qt_ref[...] * (DH ** -0.5 * LOG2E))
    q_t = jnp.concatenate([q_t[h * DH:(h + 1) * DH] for h in range(HPG)], axis=1)
    qt = _bf(jnp.concatenate([q_t, jnp.zeros_like(q_t)], axis=0))
    ncmp = kvc_ref.shape[0]
    t_row = q0 + lax.broadcasted_iota(jnp.int32, (1, QB), 1)
    head = lambda a, h: a[:, h * QB:(h + 1) * QB]

    nsel = ovl_ref.shape[0]

    def cmp_prefix(nc):
        def run():
            s = jnp.dot(kvc_ref[0:nc, :], qt, preferred_element_type=f32)
            c_end = lax.broadcasted_iota(jnp.int32, (nc, QB), 0) * CMP_STRIDE + (CMP_LEN - 1)
            m_cmp = c_end <= t_row
            kvct = kvct_ref[:, 0:nc]
            p_sum = None
            o_parts = []
            for h in range(HPG):
                p = _softmax_cols(head(s, h), m_cmp)
                o_parts.append(jnp.dot(kvct, _bf(p), preferred_element_type=f32))
                p_sum = p if p_sum is None else p_sum + p
            return tuple(o_parts) + (_dot_exact_lhs(ovl_ref[:, 0:nc], p_sum, 2),)
        return run

    seg = min(NSA_CSEG, ncmp)
    n_seg = ncmp // seg
    visible = (q0 + QB - CMP_LEN) // CMP_STRIDE + 1
    seg_idx = jnp.clip((visible + seg - 1) // seg - 1, 0, n_seg - 1)
    res = lax.switch(seg_idx, [cmp_prefix((k + 1) * seg) for k in range(n_seg)])
    o_cmp, imp = res[:HPG], res[HPG]

    n_row = lax.broadcasted_iota(jnp.int32, (nsel, QB), 0)
    cur = t_row // SEL_BLOCK
    causal = n_row <= cur
    forced = (n_row == 0) | (n_row == cur) | (n_row == cur - 1)
    score = jnp.where(causal & jnp.logical_not(forced), imp, NEG_INF)
    bias = jnp.where(causal & forced, 0.0, NEG_INF)
    for _ in range(min(SEL_TOPN, nsel) - 3):
        best = jnp.max(score, axis=0, keepdims=True)
        first = jnp.min(jnp.where(score == best, n_row, nsel), axis=0, keepdims=True)
        hit = n_row == first
        bias = jnp.where(hit & (best > 0.5 * NEG_INF), 0.0, bias)
        score = jnp.where(hit, -jnp.inf, score)
    bias_ref[...] = bias

    KC = NSA_KC
    n_sub = KC // SEL_BLOCK
    key_row = lax.broadcasted_iota(jnp.int32, (KC, QB), 0)
    acc_ref[...] = jnp.zeros_like(acc_ref)
    last_chunk = kq_ref.shape[0] // KC - 1
    q_rows = qt[0:DH]
    pad_rows = jnp.zeros((LANES - DH - NSA_HOT, HPG * QB), jnp.bfloat16)

    def chunk_scores(idx):
        c = jnp.minimum(idx, last_chunk)
        b8 = bias_ref[pl.ds(pl.multiple_of(c * n_sub, n_sub), n_sub), :]
        b_rows = jnp.concatenate([b8, jnp.zeros((NSA_HOT - n_sub, QB), f32)], axis=0)
        q_aug = jnp.concatenate([q_rows, _bf(jnp.concatenate([b_rows] * HPG, axis=1)), pad_rows], axis=0)
        k0 = pl.multiple_of(c * KC, KC)
        return jnp.dot(kq_ref[pl.ds(k0, KC), :], q_aug, preferred_element_type=f32)

    def consume(sc_ref, idx, ms, ls, causal):
        c = jnp.minimum(idx, last_chunk)
        vt = vst_ref[:, pl.ds(pl.multiple_of(c * KC, KC), KC)]
        sh = sc_ref[...]
        if causal:
            madd = jnp.where(idx * KC + key_row <= t_row, 0.0, NEG_INF)
            sh = sh + jnp.concatenate([madd] * HPG, axis=1)
        new_m = jnp.maximum(ms, jnp.max(sh, axis=0, keepdims=True))
        alpha = jnp.exp2(ms - new_m)
        p = jnp.exp2(sh - new_m)
        new_l = alpha * ls + jnp.sum(p, axis=0, keepdims=True)
        acc_ref[...] = alpha * acc_ref[...] + jnp.dot(vt, _bf(p), preferred_element_type=f32)
        return new_m, new_l

    sca_ref[...] = chunk_scores(0)

    def pair_step(jj, carry):
        ms, ls = carry
        scb_ref[...] = chunk_scores(2 * jj + 1)
        ms, ls = consume(sca_ref, 2 * jj, ms, ls, False)
        sca_ref[...] = chunk_scores(2 * jj + 2)
        return consume(scb_ref, 2 * jj + 1, ms, ls, False)

    init = (jnp.full((1, HPG * QB), NEG_INF, f32), jnp.zeros((1, HPG * QB), f32))
    n_pairs = (q0 // KC) // 2
    ms, ls = lax.fori_loop(0, n_pairs, pair_step, init)
    n_full = q0 // KC

    def last_two(ms, ls):
        scb_ref[...] = chunk_scores(2 * n_pairs + 1)
        ms, ls = consume(sca_ref, 2 * n_pairs, ms, ls, True)
        return consume(scb_ref, 2 * n_pairs + 1, ms, ls, True)

    def last_one(ms, ls):
        return consume(sca_ref, 2 * n_pairs, ms, ls, True)

    _, l_sel = lax.cond(n_full % 2 == 1, last_two, last_one, ms, ls)

    WC = NSA_WC
    s_win, m_win = [], []
    for c in range(n_win):
        s_pos = q0 - WINDOW + c * WC + lax.broadcasted_iota(jnp.int32, (WC, QB), 0)
        dist = t_row - s_pos
        m_win.append((dist >= 0) & (dist < WINDOW) & (s_pos >= 0))
        s_win.append(jnp.dot(kvw_refs[c][...], qt, preferred_element_type=f32))
    m_all = jnp.concatenate([jnp.concatenate(m_win, axis=0)] * HPG, axis=1)
    s_all = jnp.where(m_all, jnp.concatenate(s_win, axis=0), NEG_INF)
    p_win = jnp.exp2(s_all - jnp.max(s_all, axis=0, keepdims=True))
    l_win = jnp.sum(p_win, axis=0, keepdims=True)
    p_win = _bf(p_win)
    o_win = None
    for c in range(n_win):
        t = jnp.dot(kvwt_refs[c][DH:, :], p_win[c * WC:(c + 1) * WC], preferred_element_type=f32)
        o_win = t if o_win is None else o_win + t
    o_win = o_win * (1.0 / l_win)

    gate = _sigmoid(gate_ref[...])
    outs = []
    for h in range(HPG):
        o_sel = head(acc_ref[...], h) * (1.0 / head(l_sel, h))
        gr = lambda b: gate[h * 3 + b:h * 3 + b + 1, :]
        outs.append(gr(0) * o_cmp[h][DH:, :] + gr(1) * o_sel + gr(2) * head(o_win, h))
    o_ref[...] = jnp.transpose(jnp.concatenate(outs, axis=0))


def _nsa(p, cmp_pe, cmp_w1, cmp_b1, cmp_w2):
    S = p.shape[0]
    G, HPG, DH, QB, WC = NSA_KV_GROUPS, NSA_HPG, NSA_HEAD, NSA_QB, NSA_WC
    W, KV = NSA_WIDTH, NSA_KV
    nqb = S // QB
    n_win = (WINDOW + QB) // WC
    kc, vc, ks, vs, kw, vw = (p[:, W + i * KV:W + (i + 1) * KV] for i in range(6))
    gate = p[:, W + 6 * KV:W + 6 * KV + 3 * NSA_HEADS]
    k_cmp = _nsa_compress(kc, cmp_pe[0], cmp_w1[0], cmp_b1[0], cmp_w2[0])
    v_cmp = _nsa_compress(vc, cmp_pe[1], cmp_w1[1], cmp_b1[1], cmp_w2[1])
    pack = lambda k, v: _bf(jnp.concatenate([k.reshape(-1, G, DH), v.reshape(-1, G, DH)], -1)).transpose(1, 0, 2)
    kvc, kvw = pack(k_cmp, v_cmp), pack(kw, vw)
    hot = jax.nn.one_hot((jnp.arange(S) // SEL_BLOCK) % (NSA_KC // SEL_BLOCK), LANES - DH, dtype=jnp.bfloat16)
    kq = jnp.concatenate([_bf(ks.reshape(S, G, DH)), jnp.broadcast_to(hot[:, None, :], (S, G, LANES - DH))], -1)
    kq = kq.transpose(1, 0, 2)
    vst = _bf(vs.reshape(S, G, DH)).transpose(1, 2, 0)
    tr = lambda a: a.transpose(0, 2, 1)
    gate_t = jnp.pad(gate.reshape(S, G, 3 * HPG).transpose(1, 2, 0), ((0, 0), (0, NSA_GATE_ROWS - 3 * HPG), (0, 0)))
    ncmp, nsel = S // CMP_STRIDE, S // SEL_BLOCK
    c_start = np.arange(ncmp)[None, :] * CMP_STRIDE
    s_start = np.arange(nsel)[:, None] * SEL_BLOCK
    overlap_t = jnp.asarray((c_start < s_start + SEL_BLOCK) & (c_start + CMP_LEN > s_start), jnp.bfloat16)
    win_blk = lambda c: (lambda g, i: (g, jnp.maximum(i - WINDOW // WC + c, 0), 0))
    win_blk_t = lambda c: (lambda g, i: (g, 0, jnp.maximum(i - WINDOW // WC + c, 0)))
    in_specs = ([pl.BlockSpec((QB, HPG * DH), lambda g, i: (i, g)),
                 pl.BlockSpec((None, ncmp, LANES), lambda g, i: (g, 0, 0)),
                 pl.BlockSpec((None, LANES, ncmp), lambda g, i: (g, 0, 0)),
                 pl.BlockSpec((None, S, LANES), lambda g, i: (g, 0, 0)),
                 pl.BlockSpec((None, DH, S), lambda g, i: (g, 0, 0))]
                + [pl.BlockSpec((None, WC, LANES), win_blk(c)) for c in range(n_win)]
                + [pl.BlockSpec((None, LANES, WC), win_blk_t(c)) for c in range(n_win)]
                + [pl.BlockSpec((nsel, ncmp), lambda g, i: (0, 0)),
                   pl.BlockSpec((None, NSA_GATE_ROWS, QB), lambda g, i: (g, 0, i))])
    kvwt = tr(kvw)
    return pl.pallas_call(
        _nsa_body,
        grid=(G, nqb),
        in_specs=in_specs,
        out_specs=pl.BlockSpec((QB, HPG * DH), lambda g, i: (i, g)),
        out_shape=jax.ShapeDtypeStruct((S, W), jnp.float32),
        scratch_shapes=[pltpu.VMEM((nsel, QB), jnp.float32),
                        pltpu.VMEM((DH, HPG * QB), jnp.float32),
                        pltpu.VMEM((NSA_KC, HPG * QB), jnp.float32),
                        pltpu.VMEM((NSA_KC, HPG * QB), jnp.float32)],
        compiler_params=pltpu.CompilerParams(dimension_semantics=("arbitrary", "arbitrary"),
                                             vmem_limit_bytes=56 * 1024 * 1024),
        name="nsa_attention",
    )(p, kvc, tr(kvc), kq, vst, *([kvw] * n_win), *([kvwt] * n_win), overlap_t, gate_t)


def _retention_body(p_ref, cc_ref, ss_ref, inner_ref, qd_ref, kd_ref, cd_ref, o_ref, r_ref):
    H, DK, W = RET_HEADS, RET_HEAD, RET_WIDTH
    f32 = jnp.float32

    @pl.when(pl.program_id(0) == 0)
    def _():
        r_ref[...] = jnp.zeros_like(r_ref)

    cc, ss = cc_ref[...], ss_ref[...]
    hh = range(H)
    col = lambda part, h: p_ref[:, part * W + h * DK:part * W + (h + 1) * DK].astype(f32)
    rot = lambda t: t * cc + pltpu.roll(t, DK // 2, axis=1) * ss
    q = [rot(col(0, h)) for h in hh]
    k = [rot(col(1, h)) * (DK ** -0.5) for h in hh]
    v = [col(2, h) for h in hh]
    r = [r_ref[h] for h in hh]
    att = [_dot(q[h], k[h], _NT) * inner_ref[h] for h in hh]
    o = [_dot(att[h], v[h]) + _dot(q[h], r[h]) * qd_ref[h] for h in hh]
    upd = [_dot(k[h] * kd_ref[h], v[h], _TN) for h in hh]
    for h in hh:
        r_ref[h] = r[h] * cd_ref[h] + upd[h]
        mu = jnp.mean(o[h], axis=-1, keepdims=True)
        oc = o[h] - mu
        var = jnp.mean(oc * oc, axis=-1, keepdims=True)
        g = col(3, h)
        o_ref[:, h * DK:(h + 1) * DK] = (g * _sigmoid(g)) * (oc * lax.rsqrt(var + RET_GN_EPS))


def _retention(p):
    S = p.shape[0]
    H, DK, C = RET_HEADS, RET_HEAD, RET_CHUNK
    f32 = jnp.float32
    inv_freq = 1.0 / (RET_THETA ** jnp.linspace(0.0, 1.0, DK // 2))
    ang = jnp.arange(S, dtype=f32)[:, None] * inv_freq[None, :]
    cos, sin = jnp.cos(ang), jnp.sin(ang)
    cc = jnp.concatenate([cos, cos], -1)
    ss = jnp.concatenate([-sin, sin], -1)
    log_g = jnp.log1p(-jnp.exp2(-5.0 - jnp.arange(H, dtype=f32)))
    i = jnp.arange(C, dtype=f32)
    diff = i[:, None] - i[None, :]
    inner_decay = jnp.where(diff >= 0, jnp.exp(log_g[:, None, None] * jnp.maximum(diff, 0.0)), 0.0)
    lanes = lambda t: jnp.broadcast_to(t, (H, C, DK))
    q_decay = lanes(jnp.exp(log_g[:, None] * (i + 1.0))[..., None])
    k_decay = lanes(jnp.exp(log_g[:, None] * (C - 1.0 - i))[..., None])
    c_decay = lanes(jnp.exp(log_g * C)[:, None, None])
    tok = lambda width: pl.BlockSpec((C, width), lambda n: (n, 0))
    const = pl.BlockSpec((H, C, DK), lambda n: (0, 0, 0))
    return pl.pallas_call(
        _retention_body,
        grid=(S // C,),
        in_specs=[tok(p.shape[1]), tok(DK), tok(DK), const, const, const, const],
        out_specs=tok(RET_WIDTH),
        out_shape=jax.ShapeDtypeStruct((S, RET_WIDTH), f32),
        scratch_shapes=[pltpu.VMEM((H, DK, DK), f32)],
        compiler_params=pltpu.CompilerParams(dimension_semantics=("arbitrary",),
                                             vmem_limit_bytes=48 * 1024 * 1024),
        name="retention_chunked",
    )(p, cc, ss, inner_decay, q_decay, k_decay, c_decay)


MOE_ROWS = 256


def _expert_body(blk_e_ref, n_used_ref, tok_ref, x_hbm, wg_ref, wu_ref, wd_ref, o_ref, xbuf, sem):
    i = pl.program_id(0)
    n_used = n_used_ref[0]
    mb = MOE_ROWS

    def row_copy(block, buf, r):
        tok = tok_ref[block * mb + r]
        return pltpu.make_async_copy(x_hbm.at[pl.ds(tok, 1), :], xbuf.at[buf, pl.ds(r, 1), :], sem.at[buf])

    def start_block(block, buf):
        def body(r, carry):
            row_copy(block, buf, r).start()
            return carry
        lax.fori_loop(0, mb, body, 0, unroll=8)

    def wait_block(block, buf):
        def body(r, carry):
            row_copy(block, buf, r).wait()
            return carry
        lax.fori_loop(0, mb, body, 0, unroll=8)

    @pl.when((i == 0) & (n_used > 0))
    def _():
        start_block(0, 0)

    @pl.when(i + 1 < n_used)
    def _():
        start_block(i + 1, (i + 1) % 2)

    @pl.when(i < n_used)
    def _():
        wait_block(i, i % 2)
        x = xbuf[i % 2]
        hg = _dot(x, wg_ref[...])
        h = (hg * _sigmoid(hg)) * _dot(x, wu_ref[...])
        o_ref[...] = _dot(h, wd_ref[...])

    @pl.when(i >= n_used)
    def _():
        o_ref[...] = jnp.zeros_like(o_ref)


def _moe_experts(x, slot_tok, blk_expert, n_used, layer, w_gate, w_up, w_down):
    D = x.shape[1]
    n_slots = slot_tok.shape[0]
    F = w_gate.shape[3]
    mb = MOE_ROWS
    grid_spec = pltpu.PrefetchScalarGridSpec(
        num_scalar_prefetch=3,
        grid=(n_slots // mb,),
        in_specs=[pl.BlockSpec(memory_space=pl.ANY),
                  pl.BlockSpec((None, None, D, F), lambda i, be, nu, tk: (layer, be[i], 0, 0)),
                  pl.BlockSpec((None, None, D, F), lambda i, be, nu, tk: (layer, be[i], 0, 0)),
                  pl.BlockSpec((None, None, F, D), lambda i, be, nu, tk: (layer, be[i], 0, 0))],
        out_specs=pl.BlockSpec((mb, D), lambda i, be, nu, tk: (i, 0)),
        scratch_shapes=[pltpu.VMEM((2, mb, D), jnp.float32), pltpu.SemaphoreType.DMA((2,))])
    return pl.pallas_call(
        _expert_body,
        grid_spec=grid_spec,
        out_shape=jax.ShapeDtypeStruct((n_slots, D), jnp.float32),
        compiler_params=pltpu.CompilerParams(dimension_semantics=("arbitrary",),
                                             vmem_limit_bytes=56 * 1024 * 1024),
        name="moe_experts",
    )(blk_expert, n_used, slot_tok, x, w_gate, w_up, w_down)


def _hier_moe(x1, logits, layer, w_gate, w_up, w_down):
    T, D = x1.shape
    f32 = jnp.float32
    grp_logits = logits[:, :N_GROUPS]
    grp = jnp.argmax(grp_logits, axis=-1)
    grp_w = jnp.take_along_axis(jax.nn.softmax(grp_logits, -1), grp[:, None], axis=-1)
    exp_logits = logits[:, N_GROUPS:N_GROUPS + N_EXPERTS].reshape(T, N_GROUPS, EXPERTS_PER_GROUP)
    in_grp = jnp.take_along_axis(exp_logits, grp[:, None, None], axis=1)[:, 0]
    e_iota = jnp.arange(EXPERTS_PER_GROUP, dtype=jnp.int32)[None, :]
    i1 = jnp.argmax(in_grp, axis=-1).astype(jnp.int32)
    rest = jnp.where(e_iota == i1[:, None], -jnp.inf, in_grp)
    i2 = jnp.argmax(rest, axis=-1).astype(jnp.int32)
    top_idx = jnp.stack([i1, i2], -1)
    top_val = jnp.stack([jnp.max(in_grp, -1), jnp.max(rest, -1)], -1)
    gate_w = (jax.nn.softmax(top_val, -1) * grp_w).reshape(-1)
    expert = (grp[:, None] * EXPERTS_PER_GROUP + top_idx).reshape(-1).astype(jnp.int32)
    token = jnp.repeat(jnp.arange(T, dtype=jnp.int32), EXPERT_TOPK)
    n_assign = T * EXPERT_TOPK
    mb = MOE_ROWS
    onehot = (expert[:, None] == jnp.arange(N_EXPERTS, dtype=jnp.int32)[None, :]).astype(jnp.int32)
    csum = jnp.cumsum(onehot, axis=0)
    pos = jnp.take_along_axis(csum, expert[:, None], axis=1)[:, 0] - 1
    counts = csum[-1]
    padded = (counts + mb - 1) // mb * mb
    pad_end = jnp.cumsum(padded)
    slot = (pad_end - padded)[expert] + pos
    n_blocks = -(-n_assign // mb) + N_EXPERTS
    n_slots = n_blocks * mb
    slot_tok = jnp.full((n_slots,), T, jnp.int32).at[slot].set(token)
    blk_start = jnp.arange(n_blocks, dtype=jnp.int32) * mb
    blk_expert = jnp.minimum(jnp.sum((pad_end[None, :] <= blk_start[:, None]).astype(jnp.int32), axis=1),
                             N_EXPERTS - 1)
    n_used = (pad_end[-1:] // mb).astype(jnp.int32)
    y = _moe_experts(x1, jnp.minimum(slot_tok, T - 1), blk_expert, n_used, layer, w_gate, w_up, w_down)
    slot2 = slot.reshape(T, EXPERT_TOPK)
    gate2 = gate_w.reshape(T, EXPERT_TOPK)
    return (y[slot2[:, 0]], y[slot2[:, 1]]), (gate2[:, 0:1], gate2[:, 1:2])


def kernel(x, w_in_first, w_in_deep, rw_mu_first, rw_mu_deep, rw_w0, rw_w_up, rw_a0, rw_a_up, rw_v0, rw_v_up, rw_g_up, rw_k_k, rw_k_a, rw_r_k, rw_lnx_g, rw_lnx_b, nsa_cmp_pe, nsa_cmp_w1, nsa_cmp_b1, nsa_cmp_w2, w_br_rw, w_br_nsa, w_br_ret, w_out, ln1_g, ln1_b, moe_w_grp, moe_b_grp, moe_w_exp, moe_b_exp, moe_w_gate, moe_w_up, moe_w_down, ln2_g, ln2_b):
    B, S, D = x.shape
    assert B == 1
    x = x.reshape(S, D)
    xb = _bf(x)
    v_first = None
    for l in range(DEPTH):
        first = l == 0
        w_in = w_in_first if first else w_in_deep[l - 1]
        mu = rw_mu_first if first else rw_mu_deep[l - 1]
        rw_cols = RW_COLS_FIRST if first else RW_COLS_DEEP
        rw_sizes = RW_BASE_SIZES if first else RW_BASE_SIZES + (RW_V_LORA,)
        rw_padded = RW_PAD_SIZES if first else RW_PAD_SIZES + (LORA_PAD,)
        rw_tn = 512 if first else 768
        tail = -sum(rw_padded) % rw_tn
        mu_p = jnp.pad(_pad_cols(mu, rw_sizes, rw_padded), (0, tail))
        sizes = rw_sizes + (0, NSA_COLS, RET_COLS, 3 * D_MODEL)
        padded = rw_padded + (tail, NSA_PCOLS, RET_COLS, 3 * D_MODEL)
        wt = lax.optimization_barrier(jnp.transpose(w_in))
        rows, r0 = [], 0
        for n, n_pad in zip(sizes, padded):
            rows.append(jnp.pad(wt[r0:r0 + n], ((0, n_pad - n), (0, 0))))
            r0 += n
        wt = lax.optimization_barrier(_bf(jnp.concatenate(rows, 0)))
        w_all = jnp.transpose(wt)
        c0 = sum(rw_padded) + tail
        w_rw = w_all[:, :c0]
        w_nsa = w_all[:, c0:c0 + NSA_PCOLS]
        c0 += NSA_PCOLS
        w_ret = w_all[:, c0:c0 + RET_COLS]
        w_gates = w_all[:, c0 + RET_COLS:]
        p_rw = _matmul(xb, w_rw, tn=rw_tn)
        p_nsa = _matmul(xb, w_nsa, tn=NSA_PCOLS // 3)
        p_ret = _matmul(xb, w_ret, tn=512)
        gates = _matmul(xb, w_gates, tn=512, out_dtype=jnp.bfloat16)
        y_rw, v_first = _rwkv7_time_mix(p_rw, mu_p, rw_w0[l], rw_w_up[l], rw_a0[l],
                                        rw_a_up[l], rw_g_up[l], rw_k_k[l], rw_k_a[l], rw_r_k[l], rw_lnx_g[l],
                                        rw_lnx_b[l], v_first,
                                        None if first else rw_v0[l - 1], None if first else rw_v_up[l - 1])
        y_nsa = _nsa(p_nsa, nsa_cmp_pe[l], nsa_cmp_w1[l], nsa_cmp_b1[l], nsa_cmp_w2[l])
        y_ret = _retention(p_ret)
        w_br = _bf(jnp.stack([w_br_rw[l], w_br_nsa[l], w_br_ret[l]]))
        merged = _merge_branches((y_rw, y_nsa, y_ret), gates, w_br)
        w_router = jnp.pad(jnp.concatenate([moe_w_grp[l], moe_w_exp[l]], 1),
                           ((0, 0), (0, ROUTER_PAD - N_GROUPS - N_EXPERTS)))
        b_router = jnp.pad(jnp.concatenate([moe_b_grp[l], moe_b_exp[l]]),
                           (0, ROUTER_PAD - N_GROUPS - N_EXPERTS)).reshape(1, ROUTER_PAD)
        x1, logits = _out_ln_router(merged, _bf(w_out[l]), x, ln1_g[l], ln1_b[l], _bf(w_router), b_router)
        ys, ws = _hier_moe(x1, logits, l, moe_w_gate, moe_w_up, moe_w_down)
        x, xb = _moe_combine_ln(x1, ys, ws, ln2_g[l], ln2_b[l])
    return x.reshape(B, S, D)
```

```python
import functools

import jax
import jax.numpy as jnp
from jax import lax
import numpy as np
from jax.experimental import pallas as pl
from jax.experimental.pallas import tpu as pltpu

D_MODEL = 2048
DEPTH = 2

RW_HEADS = 16
RW_HEAD = 64
RW_WIDTH = RW_HEADS * RW_HEAD
RW_DECAY_LORA = 96
RW_A_LORA = 96
RW_V_LORA = 64
RW_G_LORA = 256
RW_GN_EPS = 64e-5
RW_BASE_SIZES = (RW_WIDTH, RW_WIDTH, RW_WIDTH, RW_DECAY_LORA, RW_A_LORA, RW_G_LORA)
RW_COLS_FIRST = sum(RW_BASE_SIZES)
RW_COLS_DEEP = RW_COLS_FIRST + RW_V_LORA

NSA_HEADS = 16
NSA_KV_GROUPS = 4
NSA_HPG = NSA_HEADS // NSA_KV_GROUPS
NSA_HEAD = 64
NSA_WIDTH = NSA_HEADS * NSA_HEAD
NSA_KV = NSA_KV_GROUPS * NSA_HEAD
CMP_LEN = 32
CMP_STRIDE = 16
CMP_HIDDEN = 128
SEL_BLOCK = 64
SEL_TOPN = 16
WINDOW = 512
Q_BLOCK = 128
FORCE_BONUS = 1e4
NEG_INF = -1e30
NSA_SIZES = (NSA_WIDTH,) + (NSA_KV,) * 6 + (3 * NSA_HEADS,)
NSA_COLS = sum(NSA_SIZES)

RET_HEADS = 8
RET_HEAD = 128
RET_WIDTH = RET_HEADS * RET_HEAD
RET_CHUNK = 128
RET_THETA = 10000.0
RET_GN_EPS = 1e-5
RET_SIZES = (RET_WIDTH,) * 4
RET_COLS = sum(RET_SIZES)

N_GROUPS = 4
EXPERTS_PER_GROUP = 8
N_EXPERTS = N_GROUPS * EXPERTS_PER_GROUP
EXPERT_FF = 512
EXPERT_TOPK = 2
MOE_BLOCK = 128

DN_ALPHA = (2 * DEPTH) ** 0.25
LN_EPS = 1e-5

LANES = 128


MM_ROWS = 1024


def _matmul_body(a_ref, b_ref, o_ref):
    o_ref[...] = jnp.dot(a_ref[...], b_ref[...], preferred_element_type=jnp.float32).astype(o_ref.dtype)


def _matmul(a, b, tn, out_dtype=jnp.float32):
    m, k = a.shape
    n = b.shape[1]
    tm = min(MM_ROWS, m)
    return pl.pallas_call(
        _matmul_body,
        grid=(m // tm, n // tn),
        in_specs=[pl.BlockSpec((tm, k), lambda i, j: (i, 0)),
                  pl.BlockSpec((k, tn), lambda i, j: (0, j))],
        out_specs=pl.BlockSpec((tm, tn), lambda i, j: (i, j)),
        out_shape=jax.ShapeDtypeStruct((m, n), out_dtype),
        compiler_params=pltpu.CompilerParams(
            dimension_semantics=("parallel", "parallel"),
            vmem_limit_bytes=48 * 1024 * 1024),
        name="dense_matmul",
    )(a, b)


MERGE_ROWS = 512
MERGE_COLS = 512


def _merge_body(y0_ref, y1_ref, y2_ref, g0_ref, g1_ref, g2_ref, w_ref, o_ref):
    acc = None
    for b, (y_ref, g_ref) in enumerate(((y0_ref, g0_ref), (y1_ref, g1_ref), (y2_ref, g2_ref))):
        t = _sigmoid(g_ref[...].astype(jnp.float32)) * _dot(y_ref[...], w_ref[b])
        acc = t if acc is None else acc + t
    o_ref[...] = acc.astype(o_ref.dtype)


def _merge_branches(ys, gates, w_br):
    S, wb = ys[0].shape
    D = w_br.shape[2]
    tm, tn = min(MERGE_ROWS, S), MERGE_COLS
    nj = D // tn
    y_spec = pl.BlockSpec((tm, wb), lambda i, j: (i, 0))
    g_spec = lambda b: pl.BlockSpec((tm, tn), lambda i, j: (i, b * nj + j))
    return pl.pallas_call(
        _merge_body,
        grid=(S // tm, nj),
        in_specs=[y_spec, y_spec, y_spec, g_spec(0), g_spec(1), g_spec(2),
                  pl.BlockSpec((3, wb, tn), lambda i, j: (0, 0, j))],
        out_specs=pl.BlockSpec((tm, tn), lambda i, j: (i, j)),
        out_shape=jax.ShapeDtypeStruct((S, D), jnp.bfloat16),
        compiler_params=pltpu.CompilerParams(dimension_semantics=("parallel", "parallel"),
                                             vmem_limit_bytes=48 * 1024 * 1024),
        name="merge_branches",
    )(*ys, gates, gates, gates, w_br)


LN_ROWS = 256
ROUTER_PAD = LANES


def _ln_rows(z, g, b):
    mu = jnp.mean(z, axis=-1, keepdims=True)
    zc = z - mu
    var = jnp.mean(zc * zc, axis=-1, keepdims=True)
    return zc * lax.rsqrt(var + LN_EPS) * g + b


def _out_ln_body(m_ref, w_ref, x_ref, g_ref, b_ref, wr_ref, br_ref, x1_ref, lg_ref):
    z = DN_ALPHA * x_ref[...] + jnp.dot(m_ref[...], w_ref[...], preferred_element_type=jnp.float32)
    x1 = _ln_rows(z, g_ref[...], b_ref[...])
    x1_ref[...] = x1
    lg_ref[...] = jnp.dot(_bf(x1), wr_ref[...], preferred_element_type=jnp.float32) + br_ref[...]


def _out_ln_router(merged, w_out, x, ln_g, ln_b, w_router, b_router):
    S, D = x.shape
    tm = min(LN_ROWS, S)
    row = pl.BlockSpec((tm, D), lambda i: (i, 0))
    full = lambda shape: pl.BlockSpec(shape, lambda i: (0, 0))
    return pl.pallas_call(
        _out_ln_body,
        grid=(S // tm,),
        in_specs=[row, full((D, D)), row, full((1, D)), full((1, D)), full((D, ROUTER_PAD)), full((1, ROUTER_PAD))],
        out_specs=(row, pl.BlockSpec((tm, ROUTER_PAD), lambda i: (i, 0))),
        out_shape=(jax.ShapeDtypeStruct((S, D), jnp.float32), jax.ShapeDtypeStruct((S, ROUTER_PAD), jnp.float32)),
        compiler_params=pltpu.CompilerParams(dimension_semantics=("parallel",),
                                             vmem_limit_bytes=48 * 1024 * 1024),
        name="out_proj_ln_router",
    )(merged, w_out, x, ln_g.reshape(1, D), ln_b.reshape(1, D), w_router, b_router)


def _moe_combine_ln_body(x_ref, y0_ref, y1_ref, w0_ref, w1_ref, g_ref, b_ref, o_ref, ob_ref):
    moe = y0_ref[...] * w0_ref[...] + y1_ref[...] * w1_ref[...]
    y = _ln_rows(DN_ALPHA * x_ref[...] + moe, g_ref[...], b_ref[...])
    o_ref[...] = y
    ob_ref[...] = _bf(y)


def _moe_combine_ln(x, ys, ws, ln_g, ln_b):
    S, D = x.shape
    tm = min(LN_ROWS, S)
    row = pl.BlockSpec((tm, D), lambda i: (i, 0))
    col = pl.BlockSpec((tm, 1), lambda i: (i, 0))
    full = pl.BlockSpec((1, D), lambda i: (0, 0))
    return pl.pallas_call(
        _moe_combine_ln_body,
        grid=(S // tm,),
        in_specs=[row, row, row, col, col, full, full],
        out_specs=(row, row),
        out_shape=(jax.ShapeDtypeStruct((S, D), jnp.float32), jax.ShapeDtypeStruct((S, D), jnp.bfloat16)),
        compiler_params=pltpu.CompilerParams(dimension_semantics=("parallel",)),
        name="moe_combine_ln",
    )(x, ys[0], ys[1], ws[0], ws[1], ln_g.reshape(1, D), ln_b.reshape(1, D))


def _split(p, sizes):
    idx = [int(i) for i in np.cumsum(sizes)[:-1]]
    return jnp.split(p, idx, axis=-1)


RW_CHUNK = 64
LORA_PAD = LANES
RW_OFF_XW = 3 * RW_WIDTH
RW_OFF_XA = RW_OFF_XW + LORA_PAD
RW_OFF_XG = RW_OFF_XA + LORA_PAD
RW_OFF_XV = RW_OFF_XG + RW_G_LORA
RW_PAD_SIZES = (RW_WIDTH, RW_WIDTH, RW_WIDTH, LORA_PAD, LORA_PAD, RW_G_LORA)

_NT = (((1,), (1,)), ((), ()))
_TN = (((0,), (0,)), ((), ()))


def _bf(x):
    return x.astype(jnp.bfloat16)


def _dot(a, b, dims=None):
    if dims is None:
        return jnp.dot(_bf(a), _bf(b), preferred_element_type=jnp.float32)
    return lax.dot_general(_bf(a), _bf(b), dims, preferred_element_type=jnp.float32)


def _split3(x):
    h = _bf(x)
    r = x - h.astype(jnp.float32)
    m = _bf(r)
    l = _bf(r - m.astype(jnp.float32))
    return h, m, l


def _dot_exact_rhs(a, b_bf16, passes):
    acc = None
    for p in _split3(a)[:passes]:
        t = jnp.dot(p, b_bf16, preferred_element_type=jnp.float32)
        acc = t if acc is None else acc + t
    return acc


def _dot_exact_lhs(a_bf16, b, passes):
    acc = None
    for p in _split3(b)[:passes]:
        t = jnp.dot(a_bf16, p, preferred_element_type=jnp.float32)
        acc = t if acc is None else acc + t
    return acc


def _sigmoid(x):
    return 1.0 / (1.0 + jnp.exp(-x))


def _softplus(x):
    return jnp.maximum(x, 0.0) + jnp.log(1.0 + jnp.exp(-jnp.abs(x)))


def _rwkv_body(first, *refs):
    if first:
        (p_ref, mu_ref, w0_ref, wup_ref, a0_ref, aup_ref, gup_ref, kk_ref, ka_ref, rk_ref,
         lg_ref, lb_ref, ones_ref, y_ref, vf_out_ref, st_ref, prev_ref) = refs
    else:
        (p_ref, mu_ref, w0_ref, wup_ref, a0_ref, aup_ref, gup_ref, kk_ref, ka_ref, rk_ref,
         lg_ref, lb_ref, ones_ref, v0_ref, vup_ref, vf_in_ref, y_ref, st_ref, prev_ref) = refs
    C, H, N, W = RW_CHUNK, RW_HEADS, RW_HEAD, RW_WIDTH
    f32 = jnp.float32
    i = pl.program_id(0)

    @pl.when(i == 0)
    def _():
        st_ref[...] = jnp.zeros_like(st_ref)
        prev_ref[...] = jnp.zeros_like(prev_ref)

    z = p_ref[...]
    row = lax.broadcasted_iota(jnp.int32, z.shape, 0)
    prev = jnp.where(row == 0, prev_ref[...], pltpu.roll(z, 1, axis=0))
    prev_ref[...] = z[C - 1:C, :]
    xs = z + (prev - z) * mu_ref[...]

    r = xs[:, 0:W]
    k = xs[:, W:2 * W]
    v = xs[:, 2 * W:3 * W]
    xw = xs[:, RW_OFF_XW:RW_OFF_XW + LORA_PAD]
    xa = xs[:, RW_OFF_XA:RW_OFF_XA + LORA_PAD]
    xg = xs[:, RW_OFF_XG:RW_OFF_XG + RW_G_LORA]
    if first:
        vf_out_ref[...] = v
    else:
        xv = xs[:, RW_OFF_XV:RW_OFF_XV + LORA_PAD]
        v = v + (vf_in_ref[...] - v) * _sigmoid(v0_ref[...] + _dot(xv, vup_ref[...]))

    w_log = -_softplus(-(w0_ref[...] + _dot(jnp.tanh(xw), wup_ref[...]))) - 0.5
    logd = -jnp.exp(w_log)
    a = _sigmoid(a0_ref[...] + _dot(xa, aup_ref[...]))
    g = _dot(_sigmoid(xg), gup_ref[...])
    hsum = ones_ref[...]

    def head_sums(t):
        acc = None
        for piece in _split3(_dot_exact_rhs(t, hsum, 2)):
            u = lax.dot_general(piece, hsum, _NT, preferred_element_type=f32)
            acc = u if acc is None else acc + u
        return acc

    kk = k * kk_ref[...]
    kk_n2 = head_sums(kk * kk)
    kk = kk / jnp.maximum(jnp.sqrt(kk_n2), 1e-12)
    k = k * (1.0 + (a - 1.0) * ka_ref[...])
    b = kk * a
    bonus = head_sums(r * k * rk_ref[...]) * v

    tr = lax.broadcasted_iota(jnp.int32, (C, C), 0)
    tc = lax.broadcasted_iota(jnp.int32, (C, C), 1)
    low_incl = tr >= tc
    low_strict = tr > tc
    cum = _dot_exact_lhs(low_incl.astype(jnp.bfloat16), logd, 3)
    total = cum[C - 1:C, :]
    e_in = jnp.exp(cum)
    e_ex = jnp.exp(cum - logd)
    e_neg = jnp.exp(-cum)
    e_rem = jnp.exp(total - cum)
    kkd = kk * e_ex
    kh = k * e_neg
    bh = b * e_neg
    rd = r * e_in
    khg = k * e_rem
    bhg = b * e_rem
    gam = jnp.exp(total)

    eye = (tr == tc).astype(f32)
    heads = range(H)
    hs = lambda t: [t[:, h * N:(h + 1) * N] for h in heads]
    kkd_h, kh_h, bh_h, rd_h, v_h, khg_h, bhg_h = (hs(t) for t in (kkd, kh, bh, rd, v, khg, bhg))
    st = [st_ref[h] for h in heads]
    gm = [_dot(jnp.concatenate([kkd_h[h], rd_h[h]], axis=0),
               jnp.concatenate([kh_h[h], bh_h[h]], axis=0), _NT) for h in heads]
    a_k = [jnp.where(low_strict, gm[h][0:C, 0:C], 0.0) for h in heads]
    a_b = [jnp.where(low_strict, gm[h][0:C, C:2 * C], 0.0) for h in heads]
    b_k = [jnp.where(low_incl, gm[h][C:2 * C, 0:C], 0.0) for h in heads]
    b_b = [jnp.where(low_incl, gm[h][C:2 * C, C:2 * C], 0.0) for h in heads]
    rhs = [_dot(kkd_h[h], st[h]) + _dot(a_k[h], v_h[h]) for h in heads]
    y0 = [_dot(rd_h[h], st[h]) + _dot(b_k[h], v_h[h]) for h in heads]
    t_inv = [eye - jnp.where((tr == tc + 1) & (tr % 2 == 1), a_b[h], 0.0) for h in heads]
    m = 2
    while m < C:
        rb, cb = tr // m, tc // m
        sib = (rb == cb + 1) & (rb % 2 == 1)
        tmp = [_dot(jnp.where(sib, a_b[h], 0.0), t_inv[h]) for h in heads]
        t_inv = [t_inv[h] - _dot(t_inv[h], tmp[h]) for h in heads]
        m *= 2
    zz = [_dot(t_inv[h], rhs[h]) for h in heads]
    ys = [y0[h] - _dot(b_b[h], zz[h]) for h in heads]
    upd = [_dot(khg_h[h], v_h[h], _TN) - _dot(bhg_h[h], zz[h], _TN) for h in heads]
    for h in heads:
        st_ref[h] = st[h] * jnp.transpose(gam[:, h * N:(h + 1) * N]) + upd[h]
    y = jnp.concatenate(ys, axis=1)

    inv_n = 1.0 / N
    mean = head_sums(y) * inv_n
    yc = y - mean
    var = head_sums(yc * yc) * inv_n
    yn = yc * lax.rsqrt(var + RW_GN_EPS) * lg_ref[...] + lb_ref[...]
    y_ref[...] = (yn + bonus) * g


def _pad_rows(w, rows):
    return jnp.pad(w, ((0, rows - w.shape[0]), (0, 0)))


def _pad_cols(p, sizes, padded):
    parts = _split(p, sizes)
    return jnp.concatenate([jnp.pad(t, [(0, 0)] * (t.ndim - 1) + [(0, n - t.shape[-1])])
                            for t, n in zip(parts, padded)], -1)


def _rwkv7_time_mix(p, mu, w0, w_up, a0, a_up, g_up, k_k, k_a, r_k, lnx_g, lnx_b, v_first, v0, v_up):
    S, cols = p.shape
    first = v_first is None
    C, W = RW_CHUNK, RW_WIDTH
    row = lambda t: t.reshape(1, -1)
    hid = np.arange(W) // RW_HEAD
    ones_blk = jnp.asarray(hid[:, None] == np.arange(LANES)[None, :], jnp.bfloat16)
    full = lambda shape: pl.BlockSpec(shape, lambda i: (0,) * len(shape))
    tok = lambda width: pl.BlockSpec((C, width), lambda i: (i, 0))
    args = [p, row(mu), row(w0), _pad_rows(w_up, LORA_PAD), row(a0), _pad_rows(a_up, LORA_PAD), g_up,
            row(k_k), row(k_a), row(r_k), row(lnx_g), row(lnx_b), ones_blk]
    specs = [tok(cols), full((1, cols)), full((1, W)), full((LORA_PAD, W)), full((1, W)), full((LORA_PAD, W)),
             full((RW_G_LORA, W)), full((1, W)), full((1, W)), full((1, W)), full((1, W)), full((1, W)),
             full((W, LANES))]
    if first:
        out_shape = (jax.ShapeDtypeStruct((S, W), jnp.float32), jax.ShapeDtypeStruct((S, W), jnp.float32))
        out_specs = (tok(W), tok(W))
    else:
        args += [row(v0), _pad_rows(v_up, LORA_PAD), v_first]
        specs += [full((1, W)), full((LORA_PAD, W)), tok(W)]
        out_shape = jax.ShapeDtypeStruct((S, W), jnp.float32)
        out_specs = tok(W)
    res = pl.pallas_call(
        functools.partial(_rwkv_body, first),
        grid=(S // C,),
        in_specs=specs,
        out_specs=out_specs,
        out_shape=out_shape,
        scratch_shapes=[pltpu.VMEM((RW_HEADS, RW_HEAD, RW_HEAD), jnp.float32),
                        pltpu.VMEM((1, cols), jnp.float32)],
        compiler_params=pltpu.CompilerParams(dimension_semantics=("arbitrary",),
                                             vmem_limit_bytes=48 * 1024 * 1024),
        name="rwkv7_chunked",
    )(*args)
    if first:
        return res[0], res[1]
    return res, v_first


CMP_ROWS = 256


def _compress_body(r_ref, nx_ref, pe_ref, wa_ref, wb_ref, b1_ref, w2_ref, o_ref):
    r = r_ref[...]
    rb, half = r.shape
    row = lax.broadcasted_iota(jnp.int32, r.shape, 0)
    nxt = jnp.where(row == rb - 1, nx_ref[0:1, :], pltpu.roll(r, rb - 1, axis=0))
    pe = pe_ref[...]
    h = (_dot(r + pe[:, :half], wa_ref[...]) + _dot(nxt + pe[:, half:], wb_ref[...]) + b1_ref[...])
    h = 0.5 * h * (1.0 + jnp.tanh(0.7978845608028654 * (h + 0.044715 * (h * h * h))))
    o_ref[...] = _dot(h, w2_ref[...])


def _nsa_compress(t, pe, w1, b1, w2):
    S = t.shape[0]
    G, DH, L, ST = NSA_KV_GROUPS, NSA_HEAD, CMP_LEN, CMP_STRIDE
    nr = S // ST
    rb = min(CMP_ROWS, nr)
    cols = ST * G * DH
    r = t.reshape(nr, cols)
    eye_g = jnp.eye(G, dtype=w1.dtype)
    big = jnp.einsum('ldh,gk->lgdkh', w1.reshape(L, DH, CMP_HIDDEN), eye_g).reshape(L * G * DH, G * CMP_HIDDEN)
    wa, wb = big[:cols], big[cols:]
    pe_big = jnp.broadcast_to(pe[:, None, :], (L, G, DH)).reshape(1, L * G * DH)
    b1_big = jnp.tile(b1, G).reshape(1, G * CMP_HIDDEN)
    w2_big = jnp.einsum('hd,gk->ghkd', w2, eye_g).reshape(G * CMP_HIDDEN, G * DH)
    full = lambda shape: pl.BlockSpec(shape, lambda i: (0, 0))
    return pl.pallas_call(
        _compress_body,
        grid=(nr // rb,),
        in_specs=[pl.BlockSpec((rb, cols), lambda i: (i, 0)),
                  pl.BlockSpec((8, cols), lambda i: (jnp.minimum((i + 1) * (rb // 8), nr // 8 - 1), 0)),
                  full((1, 2 * cols)), full((cols, G * CMP_HIDDEN)), full((cols, G * CMP_HIDDEN)),
                  full((1, G * CMP_HIDDEN)), full((G * CMP_HIDDEN, G * DH))],
        out_specs=pl.BlockSpec((rb, G * DH), lambda i: (i, 0)),
        out_shape=jax.ShapeDtypeStruct((nr, G * DH), jnp.float32),
        compiler_params=pltpu.CompilerParams(dimension_semantics=("parallel",),
                                             vmem_limit_bytes=48 * 1024 * 1024),
        name="nsa_compress",
    )(r, r, pe_big, _bf(wa), _bf(wb), b1_big, _bf(w2_big))


NSA_QB = 128
NSA_KC = 512
NSA_WC = 128
NSA_GATE_PAD = LANES
NSA_PCOLS = NSA_WIDTH + 6 * NSA_KV + NSA_GATE_PAD
NSA_GATE_ROWS = 16
LOG2E = 1.4426950408889634
NSA_CSEG = 256
NSA_HOT = 16


def _softmax_cols(s, mask):
    s = jnp.where(mask, s, NEG_INF)
    m = jnp.max(s, axis=0, keepdims=True)
    p = jnp.where(mask, jnp.exp2(s - m), 0.0)
    l = jnp.sum(p, axis=0, keepdims=True)
    return p * (1.0 / jnp.where(l > 0.0, l, 1.0))


def _nsa_body(*refs):
    n_win = (WINDOW + NSA_QB) // NSA_WC
    qt_ref, kvc_ref, kvct_ref, kq_ref, vst_ref = refs[:5]
    kvw_refs = refs[5:5 + n_win]
    kvwt_refs = refs[5 + n_win:5 + 2 * n_win]
    ovl_ref, gate_ref, o_ref, bias_ref, acc_ref, sca_ref, scb_ref = refs[5 + 2 * n_win:]
    QB, HPG, DH = NSA_QB, NSA_HPG, NSA_HEAD
    f32 = jnp.float32
    qi = pl.program_id(1)
    q0 = qi * QB
    q_t = jnp.transpose(qt_ref[...] * (DH ** -0.5 * LOG2E))
    q_t = jnp.concatenate([q_t[h * DH:(h + 1) * DH] for h in range(HPG)], axis=1)
    qt = _bf(jnp.concatenate([q_t, jnp.zeros_like(q_t)], axis=0))
    ncmp = kvc_ref.shape[0]
    t_row = q0 + lax.broadcasted_iota(jnp.int32, (1, QB), 1)
    head = lambda a, h: a[:, h * QB:(h + 1) * QB]

    nsel = ovl_ref.shape[0]

    def cmp_prefix(nc):
        def run():
            s = jnp.dot(kvc_ref[0:nc, :], qt, preferred_element_type=f32)
            c_end = lax.broadcasted_iota(jnp.int32, (nc, QB), 0) * CMP_STRIDE + (CMP_LEN - 1)
            m_cmp = c_end <= t_row
            kvct = kvct_ref[:, 0:nc]
            p_sum = None
            o_parts = []
            for h in range(HPG):
                p = _softmax_cols(head(s, h), m_cmp)
                o_parts.append(jnp.dot(kvct, _bf(p), preferred_element_type=f32))
                p_sum = p if p_sum is None else p_sum + p
            return tuple(o_parts) + (_dot_exact_lhs(ovl_ref[:, 0:nc], p_sum, 2),)
        return run

    seg = min(NSA_CSEG, ncmp)
    n_seg = ncmp // seg
    visible = (q0 + QB - CMP_LEN) // CMP_STRIDE + 1
    seg_idx = jnp.clip((visible + seg - 1) // seg - 1, 0, n_seg - 1)
    res = lax.switch(seg_idx, [cmp_prefix((k + 1) * seg) for k in range(n_seg)])
    o_cmp, imp = res[:HPG], res[HPG]

    n_row = lax.broadcasted_iota(jnp.int32, (nsel, QB), 0)
    cur = t_row // SEL_BLOCK
    back = cur - n_row
    forced = jnp.where(n_row == 0, 1, back | 1) == 1
    score = jnp.where(forced, NEG_INF, jnp.where(back >= 0, imp, NEG_INF))
    bias = jnp.where(forced, 0.0, NEG_INF)
    for _ in range(min(SEL_TOPN, nsel) - 3):
        best = jnp.max(score, axis=0, keepdims=True)
        first = jnp.min(jnp.where(score == best, n_row, nsel), axis=0, keepdims=True)
        hit = n_row == first
        bias = jnp.where(hit, jnp.where(best > 0.5 * NEG_INF, 0.0, bias), bias)
        score = jnp.where(hit, -jnp.inf, score)
    bias_ref[...] = bias

    KC = NSA_KC
    n_sub = KC // SEL_BLOCK
    key_row = lax.broadcasted_iota(jnp.int32, (KC, QB), 0)
    acc_ref[...] = jnp.zeros_like(acc_ref)
    last_chunk = kq_ref.shape[0] // KC - 1
    q_rows = qt[0:DH]
    pad_rows = jnp.zeros((LANES - DH - NSA_HOT, HPG * QB), jnp.bfloat16)

    def chunk_scores(idx):
        c = jnp.minimum(idx, last_chunk)
        b8 = bias_ref[pl.ds(pl.multiple_of(c * n_sub, n_sub), n_sub), :]
        b_rows = b8 if n_sub == NSA_HOT else jnp.concatenate([b8, jnp.zeros((NSA_HOT - n_sub, QB), f32)], axis=0)
        q_aug = jnp.concatenate([q_rows, _bf(jnp.concatenate([b_rows] * HPG, axis=1)), pad_rows], axis=0)
        k0 = pl.multiple_of(c * KC, KC)
        return jnp.dot(kq_ref[pl.ds(k0, KC), :], q_aug, preferred_element_type=f32)

    def consume(sc_ref, idx, ms, ls, causal):
        c = jnp.minimum(idx, last_chunk)
        vt = vst_ref[:, pl.ds(pl.multiple_of(c * KC, KC), KC)]
        sh = sc_ref[...]
        if causal:
            madd = jnp.where(idx * KC + key_row <= t_row, 0.0, NEG_INF)
            sh = sh + jnp.concatenate([madd] * HPG, axis=1)
        new_m = jnp.maximum(ms, jnp.max(sh, axis=0, keepdims=True))
        alpha = jnp.exp2(ms - new_m)
        p = jnp.exp2(sh - new_m)
        new_l = alpha * ls + jnp.sum(p, axis=0, keepdims=True)
        acc_ref[...] = alpha * acc_ref[...] + jnp.dot(vt, _bf(p), preferred_element_type=f32)
        return new_m, new_l

    sca_ref[...] = chunk_scores(0)

    def pair_step(jj, carry):
        ms, ls = carry
        scb_ref[...] = chunk_scores(2 * jj + 1)
        ms, ls = consume(sca_ref, 2 * jj, ms, ls, False)
        sca_ref[...] = chunk_scores(2 * jj + 2)
        return consume(scb_ref, 2 * jj + 1, ms, ls, False)

    init = (jnp.full((1, HPG * QB), NEG_INF, f32), jnp.zeros((1, HPG * QB), f32))
    n_pairs = (q0 // KC) // 2
    ms, ls = lax.fori_loop(0, n_pairs, pair_step, init)
    scb_ref[...] = chunk_scores(2 * n_pairs + 1)
    ms, ls = consume(sca_ref, 2 * n_pairs, ms, ls, True)
    _, l_sel = consume(scb_ref, 2 * n_pairs + 1, ms, ls, True)

    WC = NSA_WC
    s_win, m_win = [], []
    for c in range(n_win):
        start = q0 - WINDOW + c * WC
        dist = t_row - (start + lax.broadcasted_iota(jnp.int32, (WC, QB), 0))
        dist = jnp.where(start >= 0, dist, -1)
        m_win.append(lax.bitcast_convert_type(dist, jnp.uint32) < WINDOW)
        s_win.append(jnp.dot(kvw_refs[c][...], qt, preferred_element_type=f32))
    m_all = jnp.concatenate([jnp.concatenate(m_win, axis=0)] * HPG, axis=1)
    s_all = jnp.where(m_all, jnp.concatenate(s_win, axis=0), NEG_INF)
    p_win = jnp.exp2(s_all - jnp.max(s_all, axis=0, keepdims=True))
    l_win = jnp.sum(p_win, axis=0, keepdims=True)
    p_win = _bf(p_win)
    o_win = None
    for c in range(n_win):
        t = jnp.dot(kvwt_refs[c][DH:, :], p_win[c * WC:(c + 1) * WC], preferred_element_type=f32)
        o_win = t if o_win is None else o_win + t
    o_win = o_win * (1.0 / l_win)

    gate = _sigmoid(gate_ref[...])
    outs = []
    for h in range(HPG):
        o_sel = head(acc_ref[...], h) * (1.0 / head(l_sel, h))
        gr = lambda b: gate[h * 3 + b:h * 3 + b + 1, :]
        outs.append(gr(0) * o_cmp[h][DH:, :] + gr(1) * o_sel + gr(2) * head(o_win, h))
    o_ref[...] = jnp.transpose(jnp.concatenate(outs, axis=0))


def _nsa(p, cmp_pe, cmp_w1, cmp_b1, cmp_w2):
    S = p.shape[0]
    G, HPG, DH, QB, WC = NSA_KV_GROUPS, NSA_HPG, NSA_HEAD, NSA_QB, NSA_WC
    W, KV = NSA_WIDTH, NSA_KV
    nqb = S // QB
    n_win = (WINDOW + QB) // WC
    kc, vc, ks, vs, kw, vw = (p[:, W + i * KV:W + (i + 1) * KV] for i in range(6))
    gate = p[:, W + 6 * KV:W + 6 * KV + 3 * NSA_HEADS]
    k_cmp = _nsa_compress(kc, cmp_pe[0], cmp_w1[0], cmp_b1[0], cmp_w2[0])
    v_cmp = _nsa_compress(vc, cmp_pe[1], cmp_w1[1], cmp_b1[1], cmp_w2[1])
    pack = lambda k, v: _bf(jnp.concatenate([k.reshape(-1, G, DH), v.reshape(-1, G, DH)], -1)).transpose(1, 0, 2)
    kvc, kvw = pack(k_cmp, v_cmp), pack(kw, vw)
    hot = jax.nn.one_hot((jnp.arange(S) // SEL_BLOCK) % (NSA_KC // SEL_BLOCK), LANES - DH, dtype=jnp.bfloat16)
    kq = jnp.concatenate([_bf(ks.reshape(S, G, DH)), jnp.broadcast_to(hot[:, None, :], (S, G, LANES - DH))], -1)
    kq = kq.transpose(1, 0, 2)
    vst = _bf(vs.reshape(S, G, DH)).transpose(1, 2, 0)
    tr = lambda a: a.transpose(0, 2, 1)
    gate_t = jnp.pad(gate.reshape(S, G, 3 * HPG).transpose(1, 2, 0), ((0, 0), (0, NSA_GATE_ROWS - 3 * HPG), (0, 0)))
    ncmp, nsel = S // CMP_STRIDE, S // SEL_BLOCK
    c_start = np.arange(ncmp)[None, :] * CMP_STRIDE
    s_start = np.arange(nsel)[:, None] * SEL_BLOCK
    overlap_t = jnp.asarray((c_start < s_start + SEL_BLOCK) & (c_start + CMP_LEN > s_start), jnp.bfloat16)
    win_blk = lambda c: (lambda g, i: (g, jnp.maximum(i - WINDOW // WC + c, 0), 0))
    win_blk_t = lambda c: (lambda g, i: (g, 0, jnp.maximum(i - WINDOW // WC + c, 0)))
    in_specs = ([pl.BlockSpec((QB, HPG * DH), lambda g, i: (i, g)),
                 pl.BlockSpec((None, ncmp, LANES), lambda g, i: (g, 0, 0)),
                 pl.BlockSpec((None, LANES, ncmp), lambda g, i: (g, 0, 0)),
                 pl.BlockSpec((None, S, LANES), lambda g, i: (g, 0, 0)),
                 pl.BlockSpec((None, DH, S), lambda g, i: (g, 0, 0))]
                + [pl.BlockSpec((None, WC, LANES), win_blk(c)) for c in range(n_win)]
                + [pl.BlockSpec((None, LANES, WC), win_blk_t(c)) for c in range(n_win)]
                + [pl.BlockSpec((nsel, ncmp), lambda g, i: (0, 0)),
                   pl.BlockSpec((None, NSA_GATE_ROWS, QB), lambda g, i: (g, 0, i))])
    kvwt = tr(kvw)
    return pl.pallas_call(
        _nsa_body,
        grid=(G, nqb),
        in_specs=in_specs,
        out_specs=pl.BlockSpec((QB, HPG * DH), lambda g, i: (i, g)),
        out_shape=jax.ShapeDtypeStruct((S, W), jnp.float32),
        scratch_shapes=[pltpu.VMEM((nsel, QB), jnp.float32),
                        pltpu.VMEM((DH, HPG * QB), jnp.float32),
                        pltpu.VMEM((NSA_KC, HPG * QB), jnp.float32),
                        pltpu.VMEM((NSA_KC, HPG * QB), jnp.float32)],
        compiler_params=pltpu.CompilerParams(dimension_semantics=("arbitrary", "arbitrary"),
                                             vmem_limit_bytes=56 * 1024 * 1024),
        name="nsa_attention",
    )(p, kvc, tr(kvc), kq, vst, *([kvw] * n_win), *([kvwt] * n_win), overlap_t, gate_t)


def _retention_body(p_ref, cc_ref, ss_ref, inner_ref, qd_ref, kd_ref, cd_ref, o_ref, r_ref):
    H, DK, W = RET_HEADS, RET_HEAD, RET_WIDTH
    f32 = jnp.float32

    @pl.when(pl.program_id(0) == 0)
    def _():
        r_ref[...] = jnp.zeros_like(r_ref)

    cc, ss = cc_ref[...], ss_ref[...]
    hh = range(H)
    col = lambda part, h: p_ref[:, part * W + h * DK:part * W + (h + 1) * DK].astype(f32)
    rot = lambda t: t * cc + pltpu.roll(t, DK // 2, axis=1) * ss
    q = [rot(col(0, h)) for h in hh]
    k = [rot(col(1, h)) * (DK ** -0.5) for h in hh]
    v = [col(2, h) for h in hh]
    r = [r_ref[h] for h in hh]
    att = [_dot(q[h], k[h], _NT) * inner_ref[h] for h in hh]
    o = [_dot(att[h], v[h]) + _dot(q[h], r[h]) * qd_ref[h] for h in hh]
    upd = [_dot(k[h] * kd_ref[h], v[h], _TN) for h in hh]
    for h in hh:
        r_ref[h] = r[h] * cd_ref[h] + upd[h]
        mu = jnp.mean(o[h], axis=-1, keepdims=True)
        oc = o[h] - mu
        var = jnp.mean(oc * oc, axis=-1, keepdims=True)
        g = col(3, h)
        o_ref[:, h * DK:(h + 1) * DK] = (g * _sigmoid(g)) * (oc * lax.rsqrt(var + RET_GN_EPS))


def _retention(p):
    S = p.shape[0]
    H, DK, C = RET_HEADS, RET_HEAD, RET_CHUNK
    f32 = jnp.float32
    inv_freq = 1.0 / (RET_THETA ** jnp.linspace(0.0, 1.0, DK // 2))
    ang = jnp.arange(S, dtype=f32)[:, None] * inv_freq[None, :]
    cos, sin = jnp.cos(ang), jnp.sin(ang)
    cc = jnp.concatenate([cos, cos], -1)
    ss = jnp.concatenate([-sin, sin], -1)
    log_g = jnp.log1p(-jnp.exp2(-5.0 - jnp.arange(H, dtype=f32)))
    i = jnp.arange(C, dtype=f32)
    diff = i[:, None] - i[None, :]
    inner_decay = jnp.where(diff >= 0, jnp.exp(log_g[:, None, None] * jnp.maximum(diff, 0.0)), 0.0)
    lanes = lambda t: jnp.broadcast_to(t, (H, C, DK))
    q_decay = lanes(jnp.exp(log_g[:, None] * (i + 1.0))[..., None])
    k_decay = lanes(jnp.exp(log_g[:, None] * (C - 1.0 - i))[..., None])
    c_decay = lanes(jnp.exp(log_g * C)[:, None, None])
    tok = lambda width: pl.BlockSpec((C, width), lambda n: (n, 0))
    const = pl.BlockSpec((H, C, DK), lambda n: (0, 0, 0))
    return pl.pallas_call(
        _retention_body,
        grid=(S // C,),
        in_specs=[tok(p.shape[1]), tok(DK), tok(DK), const, const, const, const],
        out_specs=tok(RET_WIDTH),
        out_shape=jax.ShapeDtypeStruct((S, RET_WIDTH), f32),
        scratch_shapes=[pltpu.VMEM((H, DK, DK), f32)],
        compiler_params=pltpu.CompilerParams(dimension_semantics=("arbitrary",),
                                             vmem_limit_bytes=48 * 1024 * 1024),
        name="retention_chunked",
    )(p, cc, ss, inner_decay, q_decay, k_decay, c_decay)


MOE_ROWS = 256


def _expert_body(blk_e_ref, n_used_ref, tok_ref, x_hbm, wg_ref, wu_ref, wd_ref, o_ref, xbuf, sem):
    i = pl.program_id(0)
    n_used = n_used_ref[0]
    mb = MOE_ROWS

    def row_copy(block, buf, r):
        tok = tok_ref[block * mb + r]
        return pltpu.make_async_copy(x_hbm.at[pl.ds(tok, 1), :], xbuf.at[buf, pl.ds(r, 1), :], sem.at[buf])

    def start_block(block, buf):
        def body(r, carry):
            row_copy(block, buf, r).start()
            return carry
        lax.fori_loop(0, mb, body, 0, unroll=8)

    def wait_block(block, buf):
        def body(r, carry):
            row_copy(block, buf, r).wait()
            return carry
        lax.fori_loop(0, mb, body, 0, unroll=8)

    @pl.when((i == 0) & (n_used > 0))
    def _():
        start_block(0, 0)

    @pl.when(i + 1 < n_used)
    def _():
        start_block(i + 1, (i + 1) % 2)

    @pl.when(i < n_used)
    def _():
        wait_block(i, i % 2)
        x = xbuf[i % 2]
        hg = _dot(x, wg_ref[...])
        h = (hg * _sigmoid(hg)) * _dot(x, wu_ref[...])
        o_ref[...] = _dot(h, wd_ref[...])

    @pl.when(i >= n_used)
    def _():
        o_ref[...] = jnp.zeros_like(o_ref)


def _moe_experts(x, slot_tok, blk_expert, n_used, layer, w_gate, w_up, w_down):
    D = x.shape[1]
    n_slots = slot_tok.shape[0]
    F = w_gate.shape[3]
    mb = MOE_ROWS
    grid_spec = pltpu.PrefetchScalarGridSpec(
        num_scalar_prefetch=3,
        grid=(n_slots // mb,),
        in_specs=[pl.BlockSpec(memory_space=pl.ANY),
                  pl.BlockSpec((None, None, D, F), lambda i, be, nu, tk: (layer, be[i], 0, 0)),
                  pl.BlockSpec((None, None, D, F), lambda i, be, nu, tk: (layer, be[i], 0, 0)),
                  pl.BlockSpec((None, None, F, D), lambda i, be, nu, tk: (layer, be[i], 0, 0))],
        out_specs=pl.BlockSpec((mb, D), lambda i, be, nu, tk: (i, 0)),
        scratch_shapes=[pltpu.VMEM((2, mb, D), jnp.float32), pltpu.SemaphoreType.DMA((2,))])
    return pl.pallas_call(
        _expert_body,
        grid_spec=grid_spec,
        out_shape=jax.ShapeDtypeStruct((n_slots, D), jnp.float32),
        compiler_params=pltpu.CompilerParams(dimension_semantics=("arbitrary",),
                                             vmem_limit_bytes=56 * 1024 * 1024),
        name="moe_experts",
    )(blk_expert, n_used, slot_tok, x, w_gate, w_up, w_down)


def _hier_moe(x1, logits, layer, w_gate, w_up, w_down):
    T, D = x1.shape
    f32 = jnp.float32
    grp_logits = logits[:, :N_GROUPS]
    grp = jnp.argmax(grp_logits, axis=-1)
    grp_w = jnp.take_along_axis(jax.nn.softmax(grp_logits, -1), grp[:, None], axis=-1)
    exp_logits = logits[:, N_GROUPS:N_GROUPS + N_EXPERTS].reshape(T, N_GROUPS, EXPERTS_PER_GROUP)
    in_grp = jnp.take_along_axis(exp_logits, grp[:, None, None], axis=1)[:, 0]
    e_iota = jnp.arange(EXPERTS_PER_GROUP, dtype=jnp.int32)[None, :]
    i1 = jnp.argmax(in_grp, axis=-1).astype(jnp.int32)
    rest = jnp.where(e_iota == i1[:, None], -jnp.inf, in_grp)
    i2 = jnp.argmax(rest, axis=-1).astype(jnp.int32)
    top_idx = jnp.stack([i1, i2], -1)
    top_val = jnp.stack([jnp.max(in_grp, -1), jnp.max(rest, -1)], -1)
    gate_w = (jax.nn.softmax(top_val, -1) * grp_w).reshape(-1)
    expert = (grp[:, None] * EXPERTS_PER_GROUP + top_idx).reshape(-1).astype(jnp.int32)
    token = jnp.repeat(jnp.arange(T, dtype=jnp.int32), EXPERT_TOPK)
    n_assign = T * EXPERT_TOPK
    mb = MOE_ROWS
    onehot = (expert[:, None] == jnp.arange(N_EXPERTS, dtype=jnp.int32)[None, :]).astype(jnp.int32)
    csum = jnp.cumsum(onehot, axis=0)
    pos = jnp.take_along_axis(csum, expert[:, None], axis=1)[:, 0] - 1
    counts = csum[-1]
    padded = (counts + mb - 1) // mb * mb
    pad_end = jnp.cumsum(padded)
    slot = (pad_end - padded)[expert] + pos
    n_blocks = -(-n_assign // mb) + N_EXPERTS
    n_slots = n_blocks * mb
    slot_tok = jnp.full((n_slots,), T, jnp.int32).at[slot].set(token)
    blk_start = jnp.arange(n_blocks, dtype=jnp.int32) * mb
    blk_expert = jnp.minimum(jnp.sum((pad_end[None, :] <= blk_start[:, None]).astype(jnp.int32), axis=1),
                             N_EXPERTS - 1)
    n_used = (pad_end[-1:] // mb).astype(jnp.int32)
    y = _moe_experts(x1, jnp.minimum(slot_tok, T - 1), blk_expert, n_used, layer, w_gate, w_up, w_down)
    slot2 = slot.reshape(T, EXPERT_TOPK)
    gate2 = gate_w.reshape(T, EXPERT_TOPK)
    return (y[slot2[:, 0]], y[slot2[:, 1]]), (gate2[:, 0:1], gate2[:, 1:2])


def kernel(x, w_in_first, w_in_deep, rw_mu_first, rw_mu_deep, rw_w0, rw_w_up, rw_a0, rw_a_up, rw_v0, rw_v_up, rw_g_up, rw_k_k, rw_k_a, rw_r_k, rw_lnx_g, rw_lnx_b, nsa_cmp_pe, nsa_cmp_w1, nsa_cmp_b1, nsa_cmp_w2, w_br_rw, w_br_nsa, w_br_ret, w_out, ln1_g, ln1_b, moe_w_grp, moe_b_grp, moe_w_exp, moe_b_exp, moe_w_gate, moe_w_up, moe_w_down, ln2_g, ln2_b):
    B, S, D = x.shape
    assert B == 1
    x = x.reshape(S, D)
    xb = _bf(x)
    v_first = None
    for l in range(DEPTH):
        first = l == 0
        w_in = w_in_first if first else w_in_deep[l - 1]
        mu = rw_mu_first if first else rw_mu_deep[l - 1]
        rw_cols = RW_COLS_FIRST if first else RW_COLS_DEEP
        rw_sizes = RW_BASE_SIZES if first else RW_BASE_SIZES + (RW_V_LORA,)
        rw_padded = RW_PAD_SIZES if first else RW_PAD_SIZES + (LORA_PAD,)
        rw_tn = 512 if first else 768
        tail = -sum(rw_padded) % rw_tn
        mu_p = jnp.pad(_pad_cols(mu, rw_sizes, rw_padded), (0, tail))
        sizes = rw_sizes + (0, NSA_COLS, RET_COLS, 3 * D_MODEL)
        padded = rw_padded + (tail, NSA_PCOLS, RET_COLS, 3 * D_MODEL)
        wt = lax.optimization_barrier(jnp.transpose(w_in))
        rows, r0 = [], 0
        for n, n_pad in zip(sizes, padded):
            rows.append(jnp.pad(wt[r0:r0 + n], ((0, n_pad - n), (0, 0))))
            r0 += n
        wt = lax.optimization_barrier(_bf(jnp.concatenate(rows, 0)))
        w_all = jnp.transpose(wt)
        c0 = sum(rw_padded) + tail
        w_rw = w_all[:, :c0]
        w_nsa = w_all[:, c0:c0 + NSA_PCOLS]
        c0 += NSA_PCOLS
        w_ret = w_all[:, c0:c0 + RET_COLS]
        w_gates = w_all[:, c0 + RET_COLS:]
        p_rw = _matmul(xb, w_rw, tn=rw_tn)
        p_nsa = _matmul(xb, w_nsa, tn=NSA_PCOLS // 3)
        p_ret = _matmul(xb, w_ret, tn=512)
        gates = _matmul(xb, w_gates, tn=512, out_dtype=jnp.bfloat16)
        y_rw, v_first = _rwkv7_time_mix(p_rw, mu_p, rw_w0[l], rw_w_up[l], rw_a0[l],
                                        rw_a_up[l], rw_g_up[l], rw_k_k[l], rw_k_a[l], rw_r_k[l], rw_lnx_g[l],
                                        rw_lnx_b[l], v_first,
                                        None if first else rw_v0[l - 1], None if first else rw_v_up[l - 1])
        y_nsa = _nsa(p_nsa, nsa_cmp_pe[l], nsa_cmp_w1[l], nsa_cmp_b1[l], nsa_cmp_w2[l])
        y_ret = _retention(p_ret)
        w_br = _bf(jnp.stack([w_br_rw[l], w_br_nsa[l], w_br_ret[l]]))
        merged = _merge_branches((y_rw, y_nsa, y_ret), gates, w_br)
        w_router = jnp.pad(jnp.concatenate([moe_w_grp[l], moe_w_exp[l]], 1),
                           ((0, 0), (0, ROUTER_PAD - N_GROUPS - N_EXPERTS)))
        b_router = jnp.pad(jnp.concatenate([moe_b_grp[l], moe_b_exp[l]]),
                           (0, ROUTER_PAD - N_GROUPS - N_EXPERTS)).reshape(1, ROUTER_PAD)
        x1, logits = _out_ln_router(merged, _bf(w_out[l]), x, ln1_g[l], ln1_b[l], _bf(w_router), b_router)
        ys, ws = _hier_moe(x1, logits, l, moe_w_gate, moe_w_up, moe_w_down)
        x, xb = _moe_combine_ln(x1, ys, ws, ln2_g[l], ln2_b[l])
    return x.reshape(B, S, D)
```

```python
import functools

import jax
import jax.numpy as jnp
from jax import lax
import numpy as np
from jax.experimental import pallas as pl
from jax.experimental.pallas import tpu as pltpu

D_MODEL = 2048
DEPTH = 2

RW_HEADS = 16
RW_HEAD = 64
RW_WIDTH = RW_HEADS * RW_HEAD
RW_DECAY_LORA = 96
RW_A_LORA = 96
RW_V_LORA = 64
RW_G_LORA = 256
RW_GN_EPS = 64e-5
RW_BASE_SIZES = (RW_WIDTH, RW_WIDTH, RW_WIDTH, RW_DECAY_LORA, RW_A_LORA, RW_G_LORA)
RW_COLS_FIRST = sum(RW_BASE_SIZES)
RW_COLS_DEEP = RW_COLS_FIRST + RW_V_LORA

NSA_HEADS = 16
NSA_KV_GROUPS = 4
NSA_HPG = NSA_HEADS // NSA_KV_GROUPS
NSA_HEAD = 64
NSA_WIDTH = NSA_HEADS * NSA_HEAD
NSA_KV = NSA_KV_GROUPS * NSA_HEAD
CMP_LEN = 32
CMP_STRIDE = 16
CMP_HIDDEN = 128
SEL_BLOCK = 64
SEL_TOPN = 16
WINDOW = 512
Q_BLOCK = 128
FORCE_BONUS = 1e4
NEG_INF = -1e30
NSA_SIZES = (NSA_WIDTH,) + (NSA_KV,) * 6 + (3 * NSA_HEADS,)
NSA_COLS = sum(NSA_SIZES)

RET_HEADS = 8
RET_HEAD = 128
RET_WIDTH = RET_HEADS * RET_HEAD
RET_CHUNK = 128
RET_THETA = 10000.0
RET_GN_EPS = 1e-5
RET_SIZES = (RET_WIDTH,) * 4
RET_COLS = sum(RET_SIZES)

N_GROUPS = 4
EXPERTS_PER_GROUP = 8
N_EXPERTS = N_GROUPS * EXPERTS_PER_GROUP
EXPERT_FF = 512
EXPERT_TOPK = 2
MOE_BLOCK = 128

DN_ALPHA = (2 * DEPTH) ** 0.25
LN_EPS = 1e-5

LANES = 128


MM_ROWS = 1024


def _matmul_body(a_ref, b_ref, o_ref):
    o_ref[...] = jnp.dot(a_ref[...], b_ref[...], preferred_element_type=jnp.float32).astype(o_ref.dtype)


def _matmul(a, b, tn, out_dtype=jnp.float32):
    m, k = a.shape
    n = b.shape[1]
    tm = min(MM_ROWS, m)
    return pl.pallas_call(
        _matmul_body,
        grid=(m // tm, n // tn),
        in_specs=[pl.BlockSpec((tm, k), lambda i, j: (i, 0)),
                  pl.BlockSpec((k, tn), lambda i, j: (0, j))],
        out_specs=pl.BlockSpec((tm, tn), lambda i, j: (i, j)),
        out_shape=jax.ShapeDtypeStruct((m, n), out_dtype),
        compiler_params=pltpu.CompilerParams(
            dimension_semantics=("parallel", "parallel"),
            vmem_limit_bytes=48 * 1024 * 1024),
        name="dense_matmul",
    )(a, b)


MERGE_ROWS = 512
MERGE_COLS = 512


def _merge_body(y0_ref, y1_ref, y2_ref, g0_ref, g1_ref, g2_ref, w_ref, o_ref):
    acc = None
    for b, (y_ref, g_ref) in enumerate(((y0_ref, g0_ref), (y1_ref, g1_ref), (y2_ref, g2_ref))):
        t = _sigmoid(g_ref[...].astype(jnp.float32)) * _dot(y_ref[...], w_ref[b])
        acc = t if acc is None else acc + t
    o_ref[...] = acc.astype(o_ref.dtype)


def _merge_branches(ys, gates, w_br):
    S, wb = ys[0].shape
    D = w_br.shape[2]
    tm, tn = min(MERGE_ROWS, S), MERGE_COLS
    nj = D // tn
    y_spec = pl.BlockSpec((tm, wb), lambda i, j: (i, 0))
    g_spec = lambda b: pl.BlockSpec((tm, tn), lambda i, j: (i, b * nj + j))
    return pl.pallas_call(
        _merge_body,
        grid=(S // tm, nj),
        in_specs=[y_spec, y_spec, y_spec, g_spec(0), g_spec(1), g_spec(2),
                  pl.BlockSpec((3, wb, tn), lambda i, j: (0, 0, j))],
        out_specs=pl.BlockSpec((tm, tn), lambda i, j: (i, j)),
        out_shape=jax.ShapeDtypeStruct((S, D), jnp.bfloat16),
        compiler_params=pltpu.CompilerParams(dimension_semantics=("parallel", "parallel"),
                                             vmem_limit_bytes=48 * 1024 * 1024),
        name="merge_branches",
    )(*ys, gates, gates, gates, w_br)


LN_ROWS = 256
ROUTER_PAD = LANES


def _ln_rows(z, g, b):
    mu = jnp.mean(z, axis=-1, keepdims=True)
    zc = z - mu
    var = jnp.mean(zc * zc, axis=-1, keepdims=True)
    return zc * lax.rsqrt(var + LN_EPS) * g + b


def _out_ln_body(m_ref, w_ref, x_ref, g_ref, b_ref, wr_ref, br_ref, x1_ref, lg_ref):
    z = DN_ALPHA * x_ref[...] + jnp.dot(m_ref[...], w_ref[...], preferred_element_type=jnp.float32)
    x1 = _ln_rows(z, g_ref[...], b_ref[...])
    x1_ref[...] = x1
    lg_ref[...] = jnp.dot(_bf(x1), wr_ref[...], preferred_element_type=jnp.float32) + br_ref[...]


def _out_ln_router(merged, w_out, x, ln_g, ln_b, w_router, b_router):
    S, D = x.shape
    tm = min(LN_ROWS, S)
    row = pl.BlockSpec((tm, D), lambda i: (i, 0))
    full = lambda shape: pl.BlockSpec(shape, lambda i: (0, 0))
    return pl.pallas_call(
        _out_ln_body,
        grid=(S // tm,),
        in_specs=[row, full((D, D)), row, full((1, D)), full((1, D)), full((D, ROUTER_PAD)), full((1, ROUTER_PAD))],
        out_specs=(row, pl.BlockSpec((tm, ROUTER_PAD), lambda i: (i, 0))),
        out_shape=(jax.ShapeDtypeStruct((S, D), jnp.float32), jax.ShapeDtypeStruct((S, ROUTER_PAD), jnp.float32)),
        compiler_params=pltpu.CompilerParams(dimension_semantics=("parallel",),
                                             vmem_limit_bytes=48 * 1024 * 1024),
        name="out_proj_ln_router",
    )(merged, w_out, x, ln_g.reshape(1, D), ln_b.reshape(1, D), w_router, b_router)


def _moe_combine_ln_body(x_ref, y0_ref, y1_ref, w0_ref, w1_ref, g_ref, b_ref, o_ref, ob_ref):
    moe = y0_ref[...] * w0_ref[...] + y1_ref[...] * w1_ref[...]
    y = _ln_rows(DN_ALPHA * x_ref[...] + moe, g_ref[...], b_ref[...])
    o_ref[...] = y
    ob_ref[...] = _bf(y)


def _moe_combine_ln(x, ys, ws, ln_g, ln_b):
    S, D = x.shape
    tm = min(LN_ROWS, S)
    row = pl.BlockSpec((tm, D), lambda i: (i, 0))
    col = pl.BlockSpec((tm, 1), lambda i: (i, 0))
    full = pl.BlockSpec((1, D), lambda i: (0, 0))
    return pl.pallas_call(
        _moe_combine_ln_body,
        grid=(S // tm,),
        in_specs=[row, row, row, col, col, full, full],
        out_specs=(row, row),
        out_shape=(jax.ShapeDtypeStruct((S, D), jnp.float32), jax.ShapeDtypeStruct((S, D), jnp.bfloat16)),
        compiler_params=pltpu.CompilerParams(dimension_semantics=("parallel",)),
        name="moe_combine_ln",
    )(x, ys[0], ys[1], ws[0], ws[1], ln_g.reshape(1, D), ln_b.reshape(1, D))


def _split(p, sizes):
    idx = [int(i) for i in np.cumsum(sizes)[:-1]]
    return jnp.split(p, idx, axis=-1)


RW_CHUNK = 64
LORA_PAD = LANES
RW_OFF_XW = 3 * RW_WIDTH
RW_OFF_XA = RW_OFF_XW + LORA_PAD
RW_OFF_XG = RW_OFF_XA + LORA_PAD
RW_OFF_XV = RW_OFF_XG + RW_G_LORA
RW_PAD_SIZES = (RW_WIDTH, RW_WIDTH, RW_WIDTH, LORA_PAD, LORA_PAD, RW_G_LORA)

_NT = (((1,), (1,)), ((), ()))
_TN = (((0,), (0,)), ((), ()))


def _bf(x):
    return x.astype(jnp.bfloat16)


def _dot(a, b, dims=None):
    if dims is None:
        return jnp.dot(_bf(a), _bf(b), preferred_element_type=jnp.float32)
    return lax.dot_general(_bf(a), _bf(b), dims, preferred_element_type=jnp.float32)


def _split3(x):
    h = _bf(x)
    r = x - h.astype(jnp.float32)
    m = _bf(r)
    l = _bf(r - m.astype(jnp.float32))
    return h, m, l


def _dot_exact_rhs(a, b_bf16, passes):
    acc = None
    for p in _split3(a)[:passes]:
        t = jnp.dot(p, b_bf16, preferred_element_type=jnp.float32)
        acc = t if acc is None else acc + t
    return acc


def _dot_exact_lhs(a_bf16, b, passes):
    acc = None
    for p in _split3(b)[:passes]:
        t = jnp.dot(a_bf16, p, preferred_element_type=jnp.float32)
        acc = t if acc is None else acc + t
    return acc


def _sigmoid(x):
    return 1.0 / (1.0 + jnp.exp(-x))


def _softplus(x):
    return jnp.maximum(x, 0.0) + jnp.log(1.0 + jnp.exp(-jnp.abs(x)))


def _rwkv_body(first, *refs):
    if first:
        (p_ref, mu_ref, w0_ref, wup_ref, a0_ref, aup_ref, gup_ref, kk_ref, ka_ref, rk_ref,
         lg_ref, lb_ref, ones_ref, y_ref, vf_out_ref, st_ref, prev_ref) = refs
    else:
        (p_ref, mu_ref, w0_ref, wup_ref, a0_ref, aup_ref, gup_ref, kk_ref, ka_ref, rk_ref,
         lg_ref, lb_ref, ones_ref, v0_ref, vup_ref, vf_in_ref, y_ref, st_ref, prev_ref) = refs
    C, H, N, W = RW_CHUNK, RW_HEADS, RW_HEAD, RW_WIDTH
    f32 = jnp.float32
    i = pl.program_id(0)

    @pl.when(i == 0)
    def _():
        st_ref[...] = jnp.zeros_like(st_ref)
        prev_ref[...] = jnp.zeros_like(prev_ref)

    z = p_ref[...]
    row = lax.broadcasted_iota(jnp.int32, z.shape, 0)
    prev = jnp.where(row == 0, prev_ref[...], pltpu.roll(z, 1, axis=0))
    prev_ref[...] = z[C - 1:C, :]
    xs = z + (prev - z) * mu_ref[...]

    r = xs[:, 0:W]
    k = xs[:, W:2 * W]
    v = xs[:, 2 * W:3 * W]
    xw = xs[:, RW_OFF_XW:RW_OFF_XW + LORA_PAD]
    xa = xs[:, RW_OFF_XA:RW_OFF_XA + LORA_PAD]
    xg = xs[:, RW_OFF_XG:RW_OFF_XG + RW_G_LORA]
    if first:
        vf_out_ref[...] = v
    else:
        xv = xs[:, RW_OFF_XV:RW_OFF_XV + LORA_PAD]
        v = v + (vf_in_ref[...] - v) * _sigmoid(v0_ref[...] + _dot(xv, vup_ref[...]))

    w_log = -_softplus(-(w0_ref[...] + _dot(jnp.tanh(xw), wup_ref[...]))) - 0.5
    logd = -jnp.exp(w_log)
    a = _sigmoid(a0_ref[...] + _dot(xa, aup_ref[...]))
    g = _dot(_sigmoid(xg), gup_ref[...])
    hsum = ones_ref[...]

    def head_sums(t):
        acc = None
        for piece in _split3(_dot_exact_rhs(t, hsum, 2)):
            u = lax.dot_general(piece, hsum, _NT, preferred_element_type=f32)
            acc = u if acc is None else acc + u
        return acc

    kk = k * kk_ref[...]
    kk_n2 = head_sums(kk * kk)
    kk = kk / jnp.maximum(jnp.sqrt(kk_n2), 1e-12)
    k = k * (1.0 + (a - 1.0) * ka_ref[...])
    b = kk * a
    bonus = head_sums(r * k * rk_ref[...]) * v

    tr = lax.broadcasted_iota(jnp.int32, (C, C), 0)
    tc = lax.broadcasted_iota(jnp.int32, (C, C), 1)
    low_incl = tr >= tc
    low_strict = tr > tc
    cum = _dot_exact_lhs(low_incl.astype(jnp.bfloat16), logd, 3)
    total = cum[C - 1:C, :]
    e_in = jnp.exp(cum)
    e_ex = jnp.exp(cum - logd)
    e_neg = jnp.exp(-cum)
    e_rem = jnp.exp(total - cum)
    kkd = kk * e_ex
    kh = k * e_neg
    bh = b * e_neg
    rd = r * e_in
    khg = k * e_rem
    bhg = b * e_rem
    gam = jnp.exp(total)

    eye = (tr == tc).astype(f32)
    heads = range(H)
    hs = lambda t: [t[:, h * N:(h + 1) * N] for h in heads]
    kkd_h, kh_h, bh_h, rd_h, v_h, khg_h, bhg_h = (hs(t) for t in (kkd, kh, bh, rd, v, khg, bhg))
    st = [st_ref[h] for h in heads]
    gm = [_dot(jnp.concatenate([kkd_h[h], rd_h[h]], axis=0),
               jnp.concatenate([kh_h[h], bh_h[h]], axis=0), _NT) for h in heads]
    a_k = [jnp.where(low_strict, gm[h][0:C, 0:C], 0.0) for h in heads]
    a_b = [jnp.where(low_strict, gm[h][0:C, C:2 * C], 0.0) for h in heads]
    b_k = [jnp.where(low_incl, gm[h][C:2 * C, 0:C], 0.0) for h in heads]
    b_b = [jnp.where(low_incl, gm[h][C:2 * C, C:2 * C], 0.0) for h in heads]
    rhs = [_dot(kkd_h[h], st[h]) + _dot(a_k[h], v_h[h]) for h in heads]
    y0 = [_dot(rd_h[h], st[h]) + _dot(b_k[h], v_h[h]) for h in heads]
    t_inv = [eye - jnp.where((tr == tc + 1) & (tr % 2 == 1), a_b[h], 0.0) for h in heads]
    m = 2
    while m < C:
        rb, cb = tr // m, tc // m
        sib = (rb == cb + 1) & (rb % 2 == 1)
        tmp = [_dot(jnp.where(sib, a_b[h], 0.0), t_inv[h]) for h in heads]
        t_inv = [t_inv[h] - _dot(t_inv[h], tmp[h]) for h in heads]
        m *= 2
    zz = [_dot(t_inv[h], rhs[h]) for h in heads]
    ys = [y0[h] - _dot(b_b[h], zz[h]) for h in heads]
    upd = [_dot(khg_h[h], v_h[h], _TN) - _dot(bhg_h[h], zz[h], _TN) for h in heads]
    for h in heads:
        st_ref[h] = st[h] * jnp.transpose(gam[:, h * N:(h + 1) * N]) + upd[h]
    y = jnp.concatenate(ys, axis=1)

    inv_n = 1.0 / N
    mean = head_sums(y) * inv_n
    yc = y - mean
    var = head_sums(yc * yc) * inv_n
    yn = yc * lax.rsqrt(var + RW_GN_EPS) * lg_ref[...] + lb_ref[...]
    y_ref[...] = (yn + bonus) * g


def _pad_rows(w, rows):
    return jnp.pad(w, ((0, rows - w.shape[0]), (0, 0)))


def _pad_cols(p, sizes, padded):
    parts = _split(p, sizes)
    return jnp.concatenate([jnp.pad(t, [(0, 0)] * (t.ndim - 1) + [(0, n - t.shape[-1])])
                            for t, n in zip(parts, padded)], -1)


def _rwkv7_time_mix(p, mu, w0, w_up, a0, a_up, g_up, k_k, k_a, r_k, lnx_g, lnx_b, v_first, v0, v_up):
    S, cols = p.shape
    first = v_first is None
    C, W = RW_CHUNK, RW_WIDTH
    row = lambda t: t.reshape(1, -1)
    hid = np.arange(W) // RW_HEAD
    ones_blk = jnp.asarray(hid[:, None] == np.arange(LANES)[None, :], jnp.bfloat16)
    full = lambda shape: pl.BlockSpec(shape, lambda i: (0,) * len(shape))
    tok = lambda width: pl.BlockSpec((C, width), lambda i: (i, 0))
    args = [p, row(mu), row(w0), _pad_rows(w_up, LORA_PAD), row(a0), _pad_rows(a_up, LORA_PAD), g_up,
            row(k_k), row(k_a), row(r_k), row(lnx_g), row(lnx_b), ones_blk]
    specs = [tok(cols), full((1, cols)), full((1, W)), full((LORA_PAD, W)), full((1, W)), full((LORA_PAD, W)),
             full((RW_G_LORA, W)), full((1, W)), full((1, W)), full((1, W)), full((1, W)), full((1, W)),
             full((W, LANES))]
    if first:
        out_shape = (jax.ShapeDtypeStruct((S, W), jnp.float32), jax.ShapeDtypeStruct((S, W), jnp.float32))
        out_specs = (tok(W), tok(W))
    else:
        args += [row(v0), _pad_rows(v_up, LORA_PAD), v_first]
        specs += [full((1, W)), full((LORA_PAD, W)), tok(W)]
        out_shape = jax.ShapeDtypeStruct((S, W), jnp.float32)
        out_specs = tok(W)
    res = pl.pallas_call(
        functools.partial(_rwkv_body, first),
        grid=(S // C,),
        in_specs=specs,
        out_specs=out_specs,
        out_shape=out_shape,
        scratch_shapes=[pltpu.VMEM((RW_HEADS, RW_HEAD, RW_HEAD), jnp.float32),
                        pltpu.VMEM((1, cols), jnp.float32)],
        compiler_params=pltpu.CompilerParams(dimension_semantics=("arbitrary",),
                                             vmem_limit_bytes=48 * 1024 * 1024),
        name="rwkv7_chunked",
    )(*args)
    if first:
        return res[0], res[1]
    return res, v_first


CMP_ROWS = 256


def _compress_body(r_ref, nx_ref, pe_ref, wa_ref, wb_ref, b1_ref, w2_ref, o_ref):
    r = r_ref[...]
    rb, half = r.shape
    row = lax.broadcasted_iota(jnp.int32, r.shape, 0)
    nxt = jnp.where(row == rb - 1, nx_ref[0:1, :], pltpu.roll(r, rb - 1, axis=0))
    pe = pe_ref[...]
    h = (_dot(r + pe[:, :half], wa_ref[...]) + _dot(nxt + pe[:, half:], wb_ref[...]) + b1_ref[...])
    h = 0.5 * h * (1.0 + jnp.tanh(0.7978845608028654 * (h + 0.044715 * (h * h * h))))
    o_ref[...] = _dot(h, w2_ref[...])


def _nsa_compress(t, pe, w1, b1, w2):
    S = t.shape[0]
    G, DH, L, ST = NSA_KV_GROUPS, NSA_HEAD, CMP_LEN, CMP_STRIDE
    nr = S // ST
    rb = min(CMP_ROWS, nr)
    cols = ST * G * DH
    r = t.reshape(nr, cols)
    eye_g = jnp.eye(G, dtype=w1.dtype)
    big = jnp.einsum('ldh,gk->lgdkh', w1.reshape(L, DH, CMP_HIDDEN), eye_g).reshape(L * G * DH, G * CMP_HIDDEN)
    wa, wb = big[:cols], big[cols:]
    pe_big = jnp.broadcast_to(pe[:, None, :], (L, G, DH)).reshape(1, L * G * DH)
    b1_big = jnp.tile(b1, G).reshape(1, G * CMP_HIDDEN)
    w2_big = jnp.einsum('hd,gk->ghkd', w2, eye_g).reshape(G * CMP_HIDDEN, G * DH)
    full = lambda shape: pl.BlockSpec(shape, lambda i: (0, 0))
    return pl.pallas_call(
        _compress_body,
        grid=(nr // rb,),
        in_specs=[pl.BlockSpec((rb, cols), lambda i: (i, 0)),
                  pl.BlockSpec((8, cols), lambda i: (jnp.minimum((i + 1) * (rb // 8), nr // 8 - 1), 0)),
                  full((1, 2 * cols)), full((cols, G * CMP_HIDDEN)), full((cols, G * CMP_HIDDEN)),
                  full((1, G * CMP_HIDDEN)), full((G * CMP_HIDDEN, G * DH))],
        out_specs=pl.BlockSpec((rb, G * DH), lambda i: (i, 0)),
        out_shape=jax.ShapeDtypeStruct((nr, G * DH), jnp.float32),
        compiler_params=pltpu.CompilerParams(dimension_semantics=("parallel",),
                                             vmem_limit_bytes=48 * 1024 * 1024),
        name="nsa_compress",
    )(r, r, pe_big, _bf(wa), _bf(wb), b1_big, _bf(w2_big))


NSA_QB = 128
NSA_KC = 512
NSA_WC = 128
NSA_GATE_PAD = LANES
NSA_PCOLS = NSA_WIDTH + 6 * NSA_KV + NSA_GATE_PAD
NSA_GATE_ROWS = 16
LOG2E = 1.4426950408889634
NSA_CSEG = 256
NSA_HOT = 16


def _softmax_cols(s, mask):
    s = jnp.where(mask, s, NEG_INF)
    m = jnp.max(s, axis=0, keepdims=True)
    p = jnp.where(mask, jnp.exp2(s - m), 0.0)
    l = jnp.sum(p, axis=0, keepdims=True)
    return p * (1.0 / jnp.where(l > 0.0, l, 1.0))


def _nsa_body(*refs):
    n_win = (WINDOW + NSA_QB) // NSA_WC
    qt_ref, kvc_ref, kvct_ref, kq_ref, vst_ref = refs[:5]
    kvw_refs = refs[5:5 + n_win]
    kvwt_refs = refs[5 + n_win:5 + 2 * n_win]
    ovl_ref, gate_ref, o_ref, bias_ref, acc_ref, sca_ref, scb_ref = refs[5 + 2 * n_win:]
    QB, HPG, DH = NSA_QB, NSA_HPG, NSA_HEAD
    f32 = jnp.float32
    qi = pl.program_id(1)
    q0 = qi * QB
    q_t = jnp.transpose(qt_ref[...] * (DH ** -0.5 * LOG2E))
    q_t = jnp.concatenate([q_t[h * DH:(h + 1) * DH] for h in range(HPG)], axis=1)
    qt = _bf(jnp.concatenate([q_t, jnp.zeros_like(q_t)], axis=0))
    ncmp = kvc_ref.shape[0]
    t_row = q0 + lax.broadcasted_iota(jnp.int32, (1, QB), 1)
    head = lambda a, h: a[:, h * QB:(h + 1) * QB]

    nsel = ovl_ref.shape[0]

    def cmp_prefix(nc):
        def run():
            c_end = lax.broadcasted_iota(jnp.int32, (nc, QB), 0) * CMP_STRIDE + (CMP_LEN - 1)
            m_cmp = c_end <= t_row
            kvc = kvc_ref[0:nc, :]
            kvct = kvct_ref[:, 0:nc]
            p_sum = None
            o_parts = []
            for h in range(HPG):
                s = jnp.where(m_cmp, jnp.dot(kvc, head(qt, h), preferred_element_type=f32), NEG_INF)
                p = jnp.where(m_cmp, jnp.exp2(s - jnp.max(s, axis=0, keepdims=True)), 0.0)
                l = jnp.sum(p, axis=0, keepdims=True)
                inv = 1.0 / jnp.where(l > 0.0, l, 1.0)
                o_parts.append(jnp.dot(kvct, _bf(p), preferred_element_type=f32) * inv)
                p_sum = p * inv if p_sum is None else p_sum + p * inv
            return tuple(o_parts) + (_dot_exact_lhs(ovl_ref[:, 0:nc], p_sum, 2),)
        return run

    seg = min(NSA_CSEG, ncmp)
    n_seg = ncmp // seg
    visible = (q0 + QB - CMP_LEN) // CMP_STRIDE + 1
    seg_idx = jnp.clip((visible + seg - 1) // seg - 1, 0, n_seg - 1)
    res = lax.switch(seg_idx, [cmp_prefix((k + 1) * seg) for k in range(n_seg)])
    o_cmp, imp = res[:HPG], res[HPG]

    n_row = lax.broadcasted_iota(jnp.int32, (nsel, QB), 0)
    cur = t_row // SEL_BLOCK
    back = cur - n_row
    forced = jnp.where(n_row == 0, 1, back | 1) == 1
    score = jnp.where(forced, NEG_INF, jnp.where(back >= 0, imp, NEG_INF))
    bias = jnp.where(forced, 0.0, NEG_INF)
    for _ in range(min(SEL_TOPN, nsel) - 3):
        best = jnp.max(score, axis=0, keepdims=True)
        first = jnp.min(jnp.where(score == best, n_row, nsel), axis=0, keepdims=True)
        hit = n_row == first
        bias = jnp.where(hit, jnp.where(best > 0.5 * NEG_INF, 0.0, bias), bias)
        score = jnp.where(hit, -jnp.inf, score)
    bias_ref[...] = bias

    KC = NSA_KC
    n_sub = KC // SEL_BLOCK
    key_row = lax.broadcasted_iota(jnp.int32, (KC, QB), 0)
    acc_ref[...] = jnp.zeros_like(acc_ref)
    last_chunk = kq_ref.shape[0] // KC - 1
    q_rows = qt[0:DH]
    pad_rows = jnp.zeros((LANES - DH - NSA_HOT, HPG * QB), jnp.bfloat16)

    def chunk_scores(idx):
        c = jnp.minimum(idx, last_chunk)
        b8 = bias_ref[pl.ds(pl.multiple_of(c * n_sub, n_sub), n_sub), :]
        b_rows = b8 if n_sub == NSA_HOT else jnp.concatenate([b8, jnp.zeros((NSA_HOT - n_sub, QB), f32)], axis=0)
        q_aug = jnp.concatenate([q_rows, _bf(jnp.concatenate([b_rows] * HPG, axis=1)), pad_rows], axis=0)
        k0 = pl.multiple_of(c * KC, KC)
        return jnp.dot(kq_ref[pl.ds(k0, KC), :], q_aug, preferred_element_type=f32)

    def consume(sc_ref, idx, ms, ls, causal):
        c = jnp.minimum(idx, last_chunk)
        vt = vst_ref[:, pl.ds(pl.multiple_of(c * KC, KC), KC)]
        sh = sc_ref[...]
        if causal:
            madd = jnp.where(idx * KC + key_row <= t_row, 0.0, NEG_INF)
            sh = sh + jnp.concatenate([madd] * HPG, axis=1)
        new_m = jnp.maximum(ms, jnp.max(sh, axis=0, keepdims=True))
        alpha = jnp.exp2(ms - new_m)
        p = jnp.exp2(sh - new_m)
        new_l = alpha * ls + jnp.sum(p, axis=0, keepdims=True)
        acc_ref[...] = alpha * acc_ref[...] + jnp.dot(vt, _bf(p), preferred_element_type=f32)
        return new_m, new_l

    sca_ref[...] = chunk_scores(0)

    def pair_step(jj, carry):
        ms, ls = carry
        scb_ref[...] = chunk_scores(2 * jj + 1)
        ms, ls = consume(sca_ref, 2 * jj, ms, ls, False)
        sca_ref[...] = chunk_scores(2 * jj + 2)
        return consume(scb_ref, 2 * jj + 1, ms, ls, False)

    init = (jnp.full((1, HPG * QB), NEG_INF, f32), jnp.zeros((1, HPG * QB), f32))
    n_pairs = (q0 // KC) // 2
    ms, ls = lax.fori_loop(0, n_pairs, pair_step, init)
    scb_ref[...] = chunk_scores(2 * n_pairs + 1)
    ms, ls = consume(sca_ref, 2 * n_pairs, ms, ls, True)
    _, l_sel = consume(scb_ref, 2 * n_pairs + 1, ms, ls, True)

    WC = NSA_WC
    s_win, m_win = [], []
    for c in range(n_win):
        start = q0 - WINDOW + c * WC
        dist = t_row - (start + lax.broadcasted_iota(jnp.int32, (WC, QB), 0))
        dist = jnp.where(start >= 0, dist, -1)
        m_win.append(lax.bitcast_convert_type(dist, jnp.uint32) < WINDOW)
        s_win.append(jnp.dot(kvw_refs[c][...], qt, preferred_element_type=f32))
    m_all = jnp.concatenate([jnp.concatenate(m_win, axis=0)] * HPG, axis=1)
    s_all = jnp.where(m_all, jnp.concatenate(s_win, axis=0), NEG_INF)
    p_win = jnp.exp2(s_all - jnp.max(s_all, axis=0, keepdims=True))
    l_win = jnp.sum(p_win, axis=0, keepdims=True)
    p_win = _bf(p_win)
    o_win = None
    for c in range(n_win):
        t = jnp.dot(kvwt_refs[c][DH:, :], p_win[c * WC:(c + 1) * WC], preferred_element_type=f32)
        o_win = t if o_win is None else o_win + t
    o_win = o_win * (1.0 / l_win)

    gate = _sigmoid(gate_ref[...])
    outs = []
    for h in range(HPG):
        o_sel = head(acc_ref[...], h) * (1.0 / head(l_sel, h))
        gr = lambda b: gate[h * 3 + b:h * 3 + b + 1, :]
        outs.append(gr(0) * o_cmp[h][DH:, :] + gr(1) * o_sel + gr(2) * head(o_win, h))
    o_ref[...] = jnp.transpose(jnp.concatenate(outs, axis=0))


def _nsa(p, cmp_pe, cmp_w1, cmp_b1, cmp_w2):
    S = p.shape[0]
    G, HPG, DH, QB, WC = NSA_KV_GROUPS, NSA_HPG, NSA_HEAD, NSA_QB, NSA_WC
    W, KV = NSA_WIDTH, NSA_KV
    nqb = S // QB
    n_win = (WINDOW + QB) // WC
    kc, vc, ks, vs, kw, vw = (p[:, W + i * KV:W + (i + 1) * KV] for i in range(6))
    gate = p[:, W + 6 * KV:W + 6 * KV + 3 * NSA_HEADS]
    k_cmp = _nsa_compress(kc, cmp_pe[0], cmp_w1[0], cmp_b1[0], cmp_w2[0])
    v_cmp = _nsa_compress(vc, cmp_pe[1], cmp_w1[1], cmp_b1[1], cmp_w2[1])
    pack = lambda k, v: _bf(jnp.concatenate([k.reshape(-1, G, DH), v.reshape(-1, G, DH)], -1)).transpose(1, 0, 2)
    kvc, kvw = pack(k_cmp, v_cmp), pack(kw, vw)
    hot = jax.nn.one_hot((jnp.arange(S) // SEL_BLOCK) % (NSA_KC // SEL_BLOCK), LANES - DH, dtype=jnp.bfloat16)
    kq = jnp.concatenate([_bf(ks.reshape(S, G, DH)), jnp.broadcast_to(hot[:, None, :], (S, G, LANES - DH))], -1)
    kq = kq.transpose(1, 0, 2)
    vst = _bf(vs.reshape(S, G, DH)).transpose(1, 2, 0)
    tr = lambda a: a.transpose(0, 2, 1)
    gate_t = jnp.pad(gate.reshape(S, G, 3 * HPG).transpose(1, 2, 0), ((0, 0), (0, NSA_GATE_ROWS - 3 * HPG), (0, 0)))
    ncmp, nsel = S // CMP_STRIDE, S // SEL_BLOCK
    c_start = np.arange(ncmp)[None, :] * CMP_STRIDE
    s_start = np.arange(nsel)[:, None] * SEL_BLOCK
    overlap_t = jnp.asarray((c_start < s_start + SEL_BLOCK) & (c_start + CMP_LEN > s_start), jnp.bfloat16)
    win_blk = lambda c: (lambda g, i: (g, jnp.maximum(i - WINDOW // WC + c, 0), 0))
    win_blk_t = lambda c: (lambda g, i: (g, 0, jnp.maximum(i - WINDOW // WC + c, 0)))
    in_specs = ([pl.BlockSpec((QB, HPG * DH), lambda g, i: (i, g)),
                 pl.BlockSpec((None, ncmp, LANES), lambda g, i: (g, 0, 0)),
                 pl.BlockSpec((None, LANES, ncmp), lambda g, i: (g, 0, 0)),
                 pl.BlockSpec((None, S, LANES), lambda g, i: (g, 0, 0)),
                 pl.BlockSpec((None, DH, S), lambda g, i: (g, 0, 0))]
                + [pl.BlockSpec((None, WC, LANES), win_blk(c)) for c in range(n_win)]
                + [pl.BlockSpec((None, LANES, WC), win_blk_t(c)) for c in range(n_win)]
                + [pl.BlockSpec((nsel, ncmp), lambda g, i: (0, 0)),
                   pl.BlockSpec((None, NSA_GATE_ROWS, QB), lambda g, i: (g, 0, i))])
    kvwt = tr(kvw)
    return pl.pallas_call(
        _nsa_body,
        grid=(G, nqb),
        in_specs=in_specs,
        out_specs=pl.BlockSpec((QB, HPG * DH), lambda g, i: (i, g)),
        out_shape=jax.ShapeDtypeStruct((S, W), jnp.float32),
        scratch_shapes=[pltpu.VMEM((nsel, QB), jnp.float32),
                        pltpu.VMEM((DH, HPG * QB), jnp.float32),
                        pltpu.VMEM((NSA_KC, HPG * QB), jnp.float32),
                        pltpu.VMEM((NSA_KC, HPG * QB), jnp.float32)],
        compiler_params=pltpu.CompilerParams(dimension_semantics=("arbitrary", "arbitrary"),
                                             vmem_limit_bytes=56 * 1024 * 1024),
        name="nsa_attention",
    )(p, kvc, tr(kvc), kq, vst, *([kvw] * n_win), *([kvwt] * n_win), overlap_t, gate_t)


def _retention_body(p_ref, cc_ref, ss_ref, inner_ref, qd_ref, kd_ref, cd_ref, o_ref, r_ref):
    H, DK, W = RET_HEADS, RET_HEAD, RET_WIDTH
    f32 = jnp.float32

    @pl.when(pl.program_id(0) == 0)
    def _():
        r_ref[...] = jnp.zeros_like(r_ref)

    cc, ss = cc_ref[...], ss_ref[...]
    hh = range(H)
    col = lambda part, h: p_ref[:, part * W + h * DK:part * W + (h + 1) * DK].astype(f32)
    rot = lambda t: t * cc + pltpu.roll(t, DK // 2, axis=1) * ss
    q = [rot(col(0, h)) for h in hh]
    k = [rot(col(1, h)) * (DK ** -0.5) for h in hh]
    v = [col(2, h) for h in hh]
    r = [r_ref[h] for h in hh]
    att = [_dot(q[h], k[h], _NT) * inner_ref[h] for h in hh]
    o = [_dot(att[h], v[h]) + _dot(q[h], r[h]) * qd_ref[h] for h in hh]
    upd = [_dot(k[h] * kd_ref[h], v[h], _TN) for h in hh]
    for h in hh:
        r_ref[h] = r[h] * cd_ref[h] + upd[h]
        mu = jnp.mean(o[h], axis=-1, keepdims=True)
        oc = o[h] - mu
        var = jnp.mean(oc * oc, axis=-1, keepdims=True)
        g = col(3, h)
        o_ref[:, h * DK:(h + 1) * DK] = (g * _sigmoid(g)) * (oc * lax.rsqrt(var + RET_GN_EPS))


def _retention(p):
    S = p.shape[0]
    H, DK, C = RET_HEADS, RET_HEAD, RET_CHUNK
    f32 = jnp.float32
    inv_freq = 1.0 / (RET_THETA ** jnp.linspace(0.0, 1.0, DK // 2))
    ang = jnp.arange(S, dtype=f32)[:, None] * inv_freq[None, :]
    cos, sin = jnp.cos(ang), jnp.sin(ang)
    cc = jnp.concatenate([cos, cos], -1)
    ss = jnp.concatenate([-sin, sin], -1)
    log_g = jnp.log1p(-jnp.exp2(-5.0 - jnp.arange(H, dtype=f32)))
    i = jnp.arange(C, dtype=f32)
    diff = i[:, None] - i[None, :]
    inner_decay = jnp.where(diff >= 0, jnp.exp(log_g[:, None, None] * jnp.maximum(diff, 0.0)), 0.0)
    lanes = lambda t: jnp.broadcast_to(t, (H, C, DK))
    q_decay = lanes(jnp.exp(log_g[:, None] * (i + 1.0))[..., None])
    k_decay = lanes(jnp.exp(log_g[:, None] * (C - 1.0 - i))[..., None])
    c_decay = lanes(jnp.exp(log_g * C)[:, None, None])
    tok = lambda width: pl.BlockSpec((C, width), lambda n: (n, 0))
    const = pl.BlockSpec((H, C, DK), lambda n: (0, 0, 0))
    return pl.pallas_call(
        _retention_body,
        grid=(S // C,),
        in_specs=[tok(p.shape[1]), tok(DK), tok(DK), const, const, const, const],
        out_specs=tok(RET_WIDTH),
        out_shape=jax.ShapeDtypeStruct((S, RET_WIDTH), f32),
        scratch_shapes=[pltpu.VMEM((H, DK, DK), f32)],
        compiler_params=pltpu.CompilerParams(dimension_semantics=("arbitrary",),
                                             vmem_limit_bytes=48 * 1024 * 1024),
        name="retention_chunked",
    )(p, cc, ss, inner_decay, q_decay, k_decay, c_decay)


MOE_ROWS = 256


def _expert_body(blk_e_ref, n_used_ref, tok_ref, x_hbm, wg_ref, wu_ref, wd_ref, o_ref, xbuf, sem):
    i = pl.program_id(0)
    n_used = n_used_ref[0]
    mb = MOE_ROWS

    def row_copy(block, buf, r):
        tok = tok_ref[block * mb + r]
        return pltpu.make_async_copy(x_hbm.at[pl.ds(tok, 1), :], xbuf.at[buf, pl.ds(r, 1), :], sem.at[buf])

    def start_block(block, buf):
        def body(r, carry):
            row_copy(block, buf, r).start()
            return carry
        lax.fori_loop(0, mb, body, 0, unroll=8)

    def wait_block(block, buf):
        def body(r, carry):
            row_copy(block, buf, r).wait()
            return carry
        lax.fori_loop(0, mb, body, 0, unroll=8)

    @pl.when((i == 0) & (n_used > 0))
    def _():
        start_block(0, 0)

    @pl.when(i + 1 < n_used)
    def _():
        start_block(i + 1, (i + 1) % 2)

    @pl.when(i < n_used)
    def _():
        wait_block(i, i % 2)
        x = xbuf[i % 2]
        hg = _dot(x, wg_ref[...])
        h = (hg * _sigmoid(hg)) * _dot(x, wu_ref[...])
        o_ref[...] = _dot(h, wd_ref[...])

    @pl.when(i >= n_used)
    def _():
        o_ref[...] = jnp.zeros_like(o_ref)


def _moe_experts(x, slot_tok, blk_expert, n_used, layer, w_gate, w_up, w_down):
    D = x.shape[1]
    n_slots = slot_tok.shape[0]
    F = w_gate.shape[3]
    mb = MOE_ROWS
    grid_spec = pltpu.PrefetchScalarGridSpec(
        num_scalar_prefetch=3,
        grid=(n_slots // mb,),
        in_specs=[pl.BlockSpec(memory_space=pl.ANY),
                  pl.BlockSpec((None, None, D, F), lambda i, be, nu, tk: (layer, be[i], 0, 0)),
                  pl.BlockSpec((None, None, D, F), lambda i, be, nu, tk: (layer, be[i], 0, 0)),
                  pl.BlockSpec((None, None, F, D), lambda i, be, nu, tk: (layer, be[i], 0, 0))],
        out_specs=pl.BlockSpec((mb, D), lambda i, be, nu, tk: (i, 0)),
        scratch_shapes=[pltpu.VMEM((2, mb, D), jnp.float32), pltpu.SemaphoreType.DMA((2,))])
    return pl.pallas_call(
        _expert_body,
        grid_spec=grid_spec,
        out_shape=jax.ShapeDtypeStruct((n_slots, D), jnp.float32),
        compiler_params=pltpu.CompilerParams(dimension_semantics=("arbitrary",),
                                             vmem_limit_bytes=56 * 1024 * 1024),
        name="moe_experts",
    )(blk_expert, n_used, slot_tok, x, w_gate, w_up, w_down)


def _hier_moe(x1, logits, layer, w_gate, w_up, w_down):
    T, D = x1.shape
    f32 = jnp.float32
    grp_logits = logits[:, :N_GROUPS]
    grp = jnp.argmax(grp_logits, axis=-1)
    grp_w = jnp.take_along_axis(jax.nn.softmax(grp_logits, -1), grp[:, None], axis=-1)
    exp_logits = logits[:, N_GROUPS:N_GROUPS + N_EXPERTS].reshape(T, N_GROUPS, EXPERTS_PER_GROUP)
    in_grp = jnp.take_along_axis(exp_logits, grp[:, None, None], axis=1)[:, 0]
    e_iota = jnp.arange(EXPERTS_PER_GROUP, dtype=jnp.int32)[None, :]
    i1 = jnp.argmax(in_grp, axis=-1).astype(jnp.int32)
    rest = jnp.where(e_iota == i1[:, None], -jnp.inf, in_grp)
    i2 = jnp.argmax(rest, axis=-1).astype(jnp.int32)
    top_idx = jnp.stack([i1, i2], -1)
    top_val = jnp.stack([jnp.max(in_grp, -1), jnp.max(rest, -1)], -1)
    gate_w = (jax.nn.softmax(top_val, -1) * grp_w).reshape(-1)
    expert = (grp[:, None] * EXPERTS_PER_GROUP + top_idx).reshape(-1).astype(jnp.int32)
    token = jnp.repeat(jnp.arange(T, dtype=jnp.int32), EXPERT_TOPK)
    n_assign = T * EXPERT_TOPK
    mb = MOE_ROWS
    onehot = (expert[:, None] == jnp.arange(N_EXPERTS, dtype=jnp.int32)[None, :]).astype(jnp.int32)
    csum = jnp.cumsum(onehot, axis=0)
    pos = jnp.take_along_axis(csum, expert[:, None], axis=1)[:, 0] - 1
    counts = csum[-1]
    padded = (counts + mb - 1) // mb * mb
    pad_end = jnp.cumsum(padded)
    slot = (pad_end - padded)[expert] + pos
    n_blocks = -(-n_assign // mb) + N_EXPERTS
    n_slots = n_blocks * mb
    slot_tok = jnp.full((n_slots,), T, jnp.int32).at[slot].set(token)
    blk_start = jnp.arange(n_blocks, dtype=jnp.int32) * mb
    blk_expert = jnp.minimum(jnp.sum((pad_end[None, :] <= blk_start[:, None]).astype(jnp.int32), axis=1),
                             N_EXPERTS - 1)
    n_used = (pad_end[-1:] // mb).astype(jnp.int32)
    y = _moe_experts(x1, jnp.minimum(slot_tok, T - 1), blk_expert, n_used, layer, w_gate, w_up, w_down)
    slot2 = slot.reshape(T, EXPERT_TOPK)
    gate2 = gate_w.reshape(T, EXPERT_TOPK)
    return (y[slot2[:, 0]], y[slot2[:, 1]]), (gate2[:, 0:1], gate2[:, 1:2])


def kernel(x, w_in_first, w_in_deep, rw_mu_first, rw_mu_deep, rw_w0, rw_w_up, rw_a0, rw_a_up, rw_v0, rw_v_up, rw_g_up, rw_k_k, rw_k_a, rw_r_k, rw_lnx_g, rw_lnx_b, nsa_cmp_pe, nsa_cmp_w1, nsa_cmp_b1, nsa_cmp_w2, w_br_rw, w_br_nsa, w_br_ret, w_out, ln1_g, ln1_b, moe_w_grp, moe_b_grp, moe_w_exp, moe_b_exp, moe_w_gate, moe_w_up, moe_w_down, ln2_g, ln2_b):
    B, S, D = x.shape
    assert B == 1
    x = x.reshape(S, D)
    xb = _bf(x)
    v_first = None
    for l in range(DEPTH):
        first = l == 0
        w_in = w_in_first if first else w_in_deep[l - 1]
        mu = rw_mu_first if first else rw_mu_deep[l - 1]
        rw_cols = RW_COLS_FIRST if first else RW_COLS_DEEP
        rw_sizes = RW_BASE_SIZES if first else RW_BASE_SIZES + (RW_V_LORA,)
        rw_padded = RW_PAD_SIZES if first else RW_PAD_SIZES + (LORA_PAD,)
        rw_tn = 512 if first else 768
        tail = -sum(rw_padded) % rw_tn
        mu_p = jnp.pad(_pad_cols(mu, rw_sizes, rw_padded), (0, tail))
        sizes = rw_sizes + (0, NSA_COLS, RET_COLS, 3 * D_MODEL)
        padded = rw_padded + (tail, NSA_PCOLS, RET_COLS, 3 * D_MODEL)
        wt = lax.optimization_barrier(jnp.transpose(w_in))
        rows, r0 = [], 0
        for n, n_pad in zip(sizes, padded):
            rows.append(jnp.pad(wt[r0:r0 + n], ((0, n_pad - n), (0, 0))))
            r0 += n
        wt = lax.optimization_barrier(_bf(jnp.concatenate(rows, 0)))
        w_all = jnp.transpose(wt)
        c0 = sum(rw_padded) + tail
        w_rw = w_all[:, :c0]
        w_nsa = w_all[:, c0:c0 + NSA_PCOLS]
        c0 += NSA_PCOLS
        w_ret = w_all[:, c0:c0 + RET_COLS]
        w_gates = w_all[:, c0 + RET_COLS:]
        p_rw = _matmul(xb, w_rw, tn=rw_tn)
        p_nsa = _matmul(xb, w_nsa, tn=NSA_PCOLS // 3)
        p_ret = _matmul(xb, w_ret, tn=512)
        gates = _matmul(xb, w_gates, tn=512, out_dtype=jnp.bfloat16)
        y_rw, v_first = _rwkv7_time_mix(p_rw, mu_p, rw_w0[l], rw_w_up[l], rw_a0[l],
                                        rw_a_up[l], rw_g_up[l], rw_k_k[l], rw_k_a[l], rw_r_k[l], rw_lnx_g[l],
                                        rw_lnx_b[l], v_first,
                                        None if first else rw_v0[l - 1], None if first else rw_v_up[l - 1])
        y_nsa = _nsa(p_nsa, nsa_cmp_pe[l], nsa_cmp_w1[l], nsa_cmp_b1[l], nsa_cmp_w2[l])
        y_ret = _retention(p_ret)
        w_br = _bf(jnp.stack([w_br_rw[l], w_br_nsa[l], w_br_ret[l]]))
        merged = _merge_branches((y_rw, y_nsa, y_ret), gates, w_br)
        w_router = jnp.pad(jnp.concatenate([moe_w_grp[l], moe_w_exp[l]], 1),
                           ((0, 0), (0, ROUTER_PAD - N_GROUPS - N_EXPERTS)))
        b_router = jnp.pad(jnp.concatenate([moe_b_grp[l], moe_b_exp[l]]),
                           (0, ROUTER_PAD - N_GROUPS - N_EXPERTS)).reshape(1, ROUTER_PAD)
        x1, logits = _out_ln_router(merged, _bf(w_out[l]), x, ln1_g[l], ln1_b[l], _bf(w_router), b_router)
        ys, ws = _hier_moe(x1, logits, l, moe_w_gate, moe_w_up, moe_w_down)
        x, xb = _moe_combine_ln(x1, ys, ws, ln2_g[l], ln2_b[l])
    return x.reshape(B, S, D)
```
